```python
import jax
import jax.numpy as jnp
from jax import lax
import numpy as np

D_MODEL = 4096
BATCH = 4
SEQ = 2048
DEPTH = 4
DEC_BATCH = 8
DEC_SEQ = 8
PAST_LEN = 8192
PAGE_SIZE = 128

HEAD_DIM = 128
D_MIX = D_MODEL
H_A = (3 * D_MIX) // (8 * HEAD_DIM)
H_B = (3 * D_MIX) // (8 * HEAD_DIM)
H_C = D_MIX // HEAD_DIM - H_A - H_B
KV_B = 4
GROUP_B = H_B // KV_B
IDX_HEADS = 32
IDX_DIM = 64
TOPK_MAX = 256
CONV_W = 4
N_CONV_A = 3 * H_A * HEAD_DIM
D_FF = 11008
FFN_RES = 0.5
N_SUB = 3
CHUNK = 64
Q_BLOCK = 128
ROPE_THETA = 10000.0
RET_GAMMA_BASE = -5.0
EPS = 1e-6
L2_EPS = 1e-6
IN_SIZES = (N_CONV_A, H_A * HEAD_DIM, H_A, H_A,
            H_B * HEAD_DIM, KV_B * HEAD_DIM, KV_B * HEAD_DIM, IDX_HEADS * IDX_DIM, IDX_DIM, IDX_HEADS,
            H_C * HEAD_DIM, H_C * HEAD_DIM, H_C * HEAD_DIM, H_C * HEAD_DIM)
N_IN = (N_CONV_A + H_A * HEAD_DIM + 2 * H_A + (H_B + 2 * KV_B) * HEAD_DIM
        + IDX_HEADS * IDX_DIM + IDX_DIM + IDX_HEADS + 4 * H_C * HEAD_DIM)

kernel_name = 'hymba_gdn_dsa_retention_macaron_decoder_step'


def _rms(x):
    xf = x.astype(jnp.float32)
    return xf * lax.rsqrt(jnp.mean(xf * xf, axis=-1, keepdims=True) + EPS)


def rms_norm(x, g):
    return (_rms(x) * g.astype(jnp.float32)).astype(x.dtype)


def l2_norm(x):
    xf = x.astype(jnp.float32)
    return xf * lax.rsqrt(jnp.sum(xf * xf, axis=-1, keepdims=True) + L2_EPS)


def rope(x, pos):
    half = x.shape[-1] // 2
    inv = ROPE_THETA ** (-jnp.arange(half, dtype=jnp.float32) / half)
    ang = pos.astype(jnp.float32)[:, None] * inv[None, :]
    cos = jnp.cos(ang)[None, :, None, :]
    sin = jnp.sin(ang)[None, :, None, :]
    xf = x.astype(jnp.float32)
    x1, x2 = xf[..., :half], xf[..., half:]
    return jnp.concatenate([x1 * cos - x2 * sin, x2 * cos + x1 * sin], axis=-1).astype(x.dtype)


def _split_cols(x, sizes):
    out, off = [], 0
    for s in sizes:
        out.append(x[..., off:off + s])
        off += s
    return out


def causal_conv(x, buf, w):
    T = x.shape[1]
    xp = jnp.concatenate([buf.astype(x.dtype), x], axis=1)
    y = xp[:, 0:T] * w[0]
    for j in range(1, CONV_W):
        y = y + xp[:, j:j + T] * w[j]
    return jax.nn.silu(y), xp[:, T:]


def _to_chunks(x, C, pad):
    x = jnp.pad(x, [(0, 0), (0, pad)] + [(0, 0)] * (x.ndim - 2))
    B, Tp = x.shape[:2]
    x = x.reshape((B, Tp // C, C) + x.shape[2:])
    return jnp.transpose(x, (1, 0, 3, 2) + tuple(range(4, x.ndim)))


def _from_chunks(o, T):
    N, B, H, C, d = o.shape
    return jnp.transpose(o, (1, 0, 3, 2, 4)).reshape(B, N * C, H, d)[:, :T]


def _chunk_decay(gc):
    C = gc.shape[-1]
    tri = jnp.tril(jnp.ones((C, C), bool))
    diff = gc[..., :, None] - gc[..., None, :]
    return jnp.where(tri, jnp.exp(jnp.where(tri, diff, 0.0)), 0.0)


def gated_delta_rule(q, k, v, g, beta, s0):
    f32 = jnp.float32
    T, dk = q.shape[1], q.shape[-1]
    dv = v.shape[-1]
    C = min(CHUNK, T)
    pad = (-T) % C
    q = _to_chunks(q.astype(f32) * dk ** -0.5, C, pad)
    k = _to_chunks(k.astype(f32), C, pad)
    v = _to_chunks(v.astype(f32), C, pad)
    beta = _to_chunks(beta.astype(f32), C, pad)
    gc = jnp.cumsum(_to_chunks(g.astype(f32), C, pad), axis=-1)
    decay = _chunk_decay(gc)
    kb = k * beta[..., None]
    strict = jnp.tril(jnp.ones((C, C), bool), -1)
    lower = jnp.where(strict, jnp.einsum('nbhik,nbhjk->nbhij', kb, k) * decay, 0.0) + jnp.eye(C, dtype=f32)
    rhs = jnp.concatenate([v * beta[..., None], kb * jnp.exp(gc)[..., None]], axis=-1)
    sol = lax.linalg.triangular_solve(lower, rhs, left_side=True, lower=True, unit_diagonal=True)
    u, w = sol[..., :dv], sol[..., dv:]

    def step(S, xs):
        qc, kc, uc, wc, gcc, dc = xs
        v_new = uc - jnp.einsum('bhik,bhkv->bhiv', wc, S)
        intra = jnp.einsum('bhik,bhjk->bhij', qc, kc) * dc
        o = (jnp.einsum('bhik,bhkv->bhiv', qc * jnp.exp(gcc)[..., None], S)
             + jnp.einsum('bhij,bhjv->bhiv', intra, v_new))
        gl = gcc[..., -1:]
        S = S * jnp.exp(gl)[..., None] + jnp.einsum('bhjk,bhjv->bhkv', kc * jnp.exp(gl - gcc)[..., None], v_new)
        return S, o

    S, o = lax.scan(step, s0.astype(f32), (q, k, u, w, gc, decay))
    return _from_chunks(o, T), S


def retention(q, k, v, logd, s0):
    f32 = jnp.float32
    T = q.shape[1]
    C = min(CHUNK, T)
    pad = (-T) % C
    q = _to_chunks(q.astype(f32), C, pad)
    k = _to_chunks(k.astype(f32), C, pad)
    v = _to_chunks(v.astype(f32), C, pad)
    gc = jnp.cumsum(_to_chunks(logd.astype(f32), C, pad), axis=-1)
    decay = _chunk_decay(gc)

    def step(S, xs):
        qc, kc, vc, gcc, dc = xs
        intra = jnp.einsum('bhik,bhjk->bhij', qc, kc) * dc
        o = (jnp.einsum('bhik,bhkv->bhiv', qc * jnp.exp(gcc)[..., None], S)
             + jnp.einsum('bhij,bhjv->bhiv', intra, vc))
        gl = gcc[..., -1:]
        S = S * jnp.exp(gl)[..., None] + jnp.einsum('bhjk,bhjv->bhkv', kc * jnp.exp(gl - gcc)[..., None], vc)
        return S, o

    S, o = lax.scan(step, s0.astype(f32), (q, k, v, gc, decay))
    return _from_chunks(o, T), S


def gather_rows(rows, idx):
    return jax.vmap(lambda r, i: r[i])(rows, idx)


def paged_gather(pool, page_table, new_rows, idx):
    n_past = page_table.shape[1] * PAGE_SIZE
    T = new_rows.shape[1]
    pidx = jnp.minimum(idx, n_past - 1)
    phys = jax.vmap(lambda pt, i: pt[i // PAGE_SIZE])(page_table, pidx)
    past_rows = pool[phys, pidx % PAGE_SIZE]
    new_r = gather_rows(new_rows, jnp.clip(idx - n_past, 0, T - 1))
    is_past = (idx < n_past).reshape(idx.shape + (1,) * (past_rows.ndim - idx.ndim))
    return jnp.where(is_past, past_rows, new_r.astype(past_rows.dtype))


def sparse_attention(q, iq, iw, kidx_all, qpos, gather_kv, k_sel):
    B, T = q.shape[:2]
    f32 = jnp.float32
    QB = min(Q_BLOCK, T)
    pad = (-T) % QB
    NB = (T + pad) // QB

    def blocks(a):
        a = jnp.pad(a, [(0, 0), (0, pad)] + [(0, 0)] * (a.ndim - 2))
        return jnp.moveaxis(a.reshape((B, NB, QB) + a.shape[2:]), 1, 0)

    qpos_b = jnp.pad(qpos, (0, pad), mode='edge').reshape(NB, QB)
    kpos = jnp.arange(kidx_all.shape[1], dtype=jnp.int32)
    kidx32 = kidx_all.astype(f32)

    def one_block(args):
        qb, iqb, iwb, pb = args
        rel = jax.nn.relu(jnp.einsum('bqhd,bsd->bqhs', iqb.astype(f32), kidx32))
        score = jnp.einsum('bqhs,bqh->bqs', rel, iwb.astype(f32))
        visible = kpos[None, :] <= pb[:, None]
        score = jnp.where(visible[None], score, -jnp.inf)
        _, idx = lax.top_k(score, k_sel)
        valid = idx <= pb[None, :, None]
        kr, vr = gather_kv(idx)
        qg = qb.reshape(B, QB, KV_B, GROUP_B, HEAD_DIM).astype(f32)
        logits = jnp.einsum('bqngd,bqsnd->bqngs', qg, kr.astype(f32)) * HEAD_DIM ** -0.5
        logits = jnp.where(valid[:, :, None, None, :], logits, -jnp.inf)
        p = jax.nn.softmax(logits, axis=-1)
        o = jnp.einsum('bqngs,bqsnd->bqngd', p, vr.astype(f32))
        return o.reshape(B, QB, H_B, HEAD_DIM).astype(q.dtype)

    o = lax.map(one_block, (blocks(q), blocks(iq), blocks(iw), qpos_b))
    return jnp.moveaxis(o, 0, 1).reshape(B, NB * QB, H_B, HEAD_DIM)[:, :T]


def token_mixer(u, w_in, conv_w, a_log, dt_bias, gdn_g, w_out, past):
    B, T, _ = u.shape
    dt = u.dtype
    (qkv_a, za, aa, ba, qb, kb, vb, iq, ik, iw, qc, kc, vc, gc) = _split_cols(u @ w_in, IN_SIZES)
    if past is None:
        conv_buf = jnp.zeros((B, CONV_W - 1, N_CONV_A), dt)
        s_gdn = jnp.zeros((B, H_A, HEAD_DIM, HEAD_DIM), jnp.float32)
        s_ret = jnp.zeros((B, H_C, HEAD_DIM, HEAD_DIM), jnp.float32)
        n_past = 0
    else:
        conv_buf, s_gdn, s_ret, pool_k, pool_v, pool_kidx, page_table = past
        n_past = page_table.shape[1] * PAGE_SIZE
    pos = n_past + jnp.arange(T, dtype=jnp.int32)

    conv_out, conv_new = causal_conv(qkv_a, conv_buf, conv_w)
    qa, ka, va = [t.reshape(B, T, H_A, HEAD_DIM) for t in jnp.split(conv_out, 3, axis=-1)]
    g = -jnp.exp(a_log) * jax.nn.softplus(aa + dt_bias)
    oa, s_gdn_new = gated_delta_rule(l2_norm(qa), l2_norm(ka), va, g, jax.nn.sigmoid(ba), s_gdn)
    oa = rms_norm(oa.astype(dt), gdn_g) * jax.nn.silu(za.reshape(B, T, H_A, HEAD_DIM))

    qb = rope(qb.reshape(B, T, H_B, HEAD_DIM), pos)
    kb = rope(kb.reshape(B, T, KV_B, HEAD_DIM), pos)
    vb = vb.reshape(B, T, KV_B, HEAD_DIM)
    iq = rope(iq.reshape(B, T, IDX_HEADS, IDX_DIM), pos)
    ik = rope(ik.reshape(B, T, 1, IDX_DIM), pos)[:, :, 0]
    iw = iw * (IDX_HEADS ** -0.5 * IDX_DIM ** -0.5)
    if past is None:
        kidx_all = ik
        gather_kv = lambda idx: (gather_rows(kb, idx), gather_rows(vb, idx))
    else:
        past_kidx = pool_kidx[page_table].reshape(B, n_past, IDX_DIM)
        kidx_all = jnp.concatenate([past_kidx, ik.astype(past_kidx.dtype)], axis=1)
        gather_kv = lambda idx: (paged_gather(pool_k, page_table, kb, idx), paged_gather(pool_v, page_table, vb, idx))
    k_sel = max(1, min(TOPK_MAX, kidx_all.shape[1] // 4))
    ob = sparse_attention(qb, iq, iw, kidx_all, pos, gather_kv, k_sel)

    qc = rope(qc.reshape(B, T, H_C, HEAD_DIM), pos)
    kc = rope(kc.reshape(B, T, H_C, HEAD_DIM), pos) * HEAD_DIM ** -0.5
    vc = vc.reshape(B, T, H_C, HEAD_DIM)
    log_gamma = jnp.log1p(-jnp.exp2(RET_GAMMA_BASE - jnp.arange(H_C, dtype=jnp.float32)))
    logd = jnp.broadcast_to(log_gamma, (B, T, H_C))
    oc, s_ret_new = retention(qc, kc, vc, logd, s_ret)
    oc = _rms(oc).astype(dt) * jax.nn.silu(gc.reshape(B, T, H_C, HEAD_DIM))

    o = jnp.concatenate([oa.reshape(B, T, -1), ob.reshape(B, T, -1), oc.reshape(B, T, -1)], axis=-1) @ w_out
    return o, (kb, vb, ik, s_gdn_new.astype(dt), conv_new, s_ret_new.astype(dt))


def swiglu(u, wg, wu, wd):
    return (jax.nn.silu(u @ wg) * (u @ wu)) @ wd


def _modulate(h, g, m):
    return rms_norm(h, g) * (1 + m[:, 1]) + m[:, 0]


def _forward(x, c, weights, past):
    (w_ada, b_ada, norm_pre, norm_post, w_in, gdn_conv_w, gdn_a_log, gdn_dt_bias, gdn_norm,
     w_out, ffn_w_gate, ffn_w_up, ffn_w_down) = weights
    B = x.shape[0]
    h = x
    states = []
    for l in range(DEPTH):
        mod = (jax.nn.silu(c) @ w_ada[l] + b_ada[l]).reshape(B, N_SUB, 3, 1, D_MODEL)
        u = _modulate(h, norm_pre[l, 0], mod[:, 0])
        y = swiglu(u, ffn_w_gate[l, 0], ffn_w_up[l, 0], ffn_w_down[l, 0])
        h = h + FFN_RES * mod[:, 0, 2] * rms_norm(y, norm_post[l, 0])
        u = _modulate(h, norm_pre[l, 1], mod[:, 1])
        if past is None:
            past_l = None
        else:
            cache_k, cache_v, cache_kidx, state_gdn, state_gdn_conv, state_ret, page_table = past
            past_l = (state_gdn_conv[l], state_gdn[l], state_ret[l], cache_k[l], cache_v[l], cache_kidx[l], page_table)
        o, st = token_mixer(u, w_in[l], gdn_conv_w[l], gdn_a_log[l], gdn_dt_bias[l], gdn_norm[l], w_out[l], past_l)
        h = h + mod[:, 1, 2] * rms_norm(o, norm_post[l, 1])
        u = _modulate(h, norm_pre[l, 2], mod[:, 2])
        y = swiglu(u, ffn_w_gate[l, 1], ffn_w_up[l, 1], ffn_w_down[l, 1])
        h = h + FFN_RES * mod[:, 2, 2] * rms_norm(y, norm_post[l, 2])
        states.append(st)
    new_state = [jnp.stack([s[i] for s in states]) for i in range(6)]
    return h, new_state


def setup_inputs(seed: int = 0) -> dict:
    key = jax.random.key(seed)
    ks = jax.random.split(key, 32)
    f32 = jnp.float32
    n_pages = PAST_LEN // PAGE_SIZE
    n_used = DEC_BATCH * n_pages
    n_pool = n_used + max(1, n_used // 4)

    def nrm(k, shape, s):
        return s * jax.random.normal(k, shape, f32)

    page_table = jax.random.permutation(ks[0], n_pool)[:n_used].reshape(DEC_BATCH, n_pages).astype(jnp.int32)
    return {
        'x_prompt': nrm(ks[1], (BATCH, SEQ, D_MODEL), 1.0),
        'x_sample': nrm(ks[2], (DEC_BATCH, DEC_SEQ, D_MODEL), 1.0),
        'cache_k': nrm(ks[3], (DEPTH, n_pool, PAGE_SIZE, KV_B, HEAD_DIM), 1.0),
        'cache_v': nrm(ks[4], (DEPTH, n_pool, PAGE_SIZE, KV_B, HEAD_DIM), 1.0),
        'cache_kidx': nrm(ks[5], (DEPTH, n_pool, PAGE_SIZE, IDX_DIM), 1.0),
        'state_gdn': nrm(ks[6], (DEPTH, DEC_BATCH, H_A, HEAD_DIM, HEAD_DIM), 0.1),
        'state_gdn_conv': nrm(ks[7], (DEPTH, DEC_BATCH, CONV_W - 1, N_CONV_A), 1.0),
        'state_ret': nrm(ks[8], (DEPTH, DEC_BATCH, H_C, HEAD_DIM, HEAD_DIM), 0.1),
        'page_table': page_table,
        'c_prompt': nrm(ks[9], (BATCH, D_MODEL), 1.0),
        'c_sample': nrm(ks[10], (DEC_BATCH, D_MODEL), 1.0),
        'w_ada': nrm(ks[11], (DEPTH, D_MODEL, N_SUB * 3 * D_MODEL), 0.5 * D_MODEL ** -0.5),
        'b_ada': nrm(ks[12], (DEPTH, N_SUB * 3 * D_MODEL), 0.02),
        'norm_pre': 1.0 + nrm(ks[13], (DEPTH, N_SUB, D_MODEL), 0.02),
        'norm_post': 1.0 + nrm(ks[14], (DEPTH, N_SUB, D_MODEL), 0.02),
        'w_in': nrm(ks[15], (DEPTH, D_MODEL, N_IN), D_MODEL ** -0.5),
        'gdn_conv_w': nrm(ks[16], (DEPTH, CONV_W, N_CONV_A), CONV_W ** -0.5),
        'gdn_a_log': jnp.log(jax.random.uniform(ks[17], (DEPTH, H_A), f32, 1.0, 16.0)),
        'gdn_dt_bias': nrm(ks[18], (DEPTH, H_A), 0.1),
        'gdn_norm': 1.0 + nrm(ks[19], (DEPTH, HEAD_DIM), 0.02),
        'w_out': nrm(ks[20], (DEPTH, D_MIX, D_MODEL), D_MIX ** -0.5),
        'ffn_w_gate': nrm(ks[21], (DEPTH, 2, D_MODEL, D_FF), D_MODEL ** -0.5),
        'ffn_w_up': nrm(ks[22], (DEPTH, 2, D_MODEL, D_FF), D_MODEL ** -0.5),
        'ffn_w_down': nrm(ks[23], (DEPTH, 2, D_FF, D_MODEL), D_FF ** -0.5),
    }


def reference(x_prompt, x_sample, cache_k, cache_v, cache_kidx, state_gdn, state_gdn_conv, state_ret, page_table,
              c_prompt, c_sample, w_ada, b_ada, norm_pre, norm_post, w_in, gdn_conv_w, gdn_a_log, gdn_dt_bias,
              gdn_norm, w_out, ffn_w_gate, ffn_w_up, ffn_w_down):
    weights = (w_ada, b_ada, norm_pre, norm_post, w_in, gdn_conv_w, gdn_a_log, gdn_dt_bias, gdn_norm,
               w_out, ffn_w_gate, ffn_w_up, ffn_w_down)
    y_prompt, (k_p, v_p, kidx_p, gdn_p, conv_p, ret_p) = _forward(x_prompt, c_prompt, weights, None)
    past = (cache_k, cache_v, cache_kidx, state_gdn, state_gdn_conv, state_ret, page_table)
    y_sample, (k_s, v_s, kidx_s, gdn_s, conv_s, ret_s) = _forward(x_sample, c_sample, weights, past)
    return (y_prompt, y_sample, k_p, v_p, kidx_p, gdn_p, conv_p, ret_p, k_s, v_s, kidx_s, gdn_s, conv_s, ret_s)
```

```python
import functools

import jax
import jax.numpy as jnp
from jax import lax
from jax.experimental import pallas as pl
from jax.experimental.pallas import tpu as pltpu

F32 = jnp.float32
BF16 = jnp.bfloat16

HEAD_DIM = 128
IDX_HEADS = 32
TOPK_MAX = 256
CONV_W = 4
FFN_RES = 0.5
N_SUB = 3
CHUNK = 64
Q_BLOCK = 128
ROPE_THETA = 10000.0
RET_GAMMA_BASE = -5.0
EPS = 1e-6
L2_EPS = 1e-6

V7X_VMEM_BYTES = 64 * 1024 * 1024
V7X_VMEM_BUDGET = V7X_VMEM_BYTES - 8 * 1024 * 1024
LANE = 128
MXU_COLS = 256


def _vmem_limit(block_bytes, scratch_bytes):
    need = 2 * block_bytes + scratch_bytes + 8 * 1024 * 1024
    assert need <= V7X_VMEM_BUDGET, need
    return need


def _nbytes(shape, dtype):
    n = 1
    for s in shape:
        n *= s
    return n * jnp.dtype(dtype).itemsize


def _adaln_kernel(c_ref, w_ref, b_ref, o_ref):
    c = c_ref[...]
    cs = (c * jax.nn.sigmoid(c)).astype(BF16)
    w = w_ref[...].astype(BF16)
    o_ref[...] = jnp.dot(cs, w, preferred_element_type=F32) + b_ref[...]


def _adaln(c_rows, w_ada, b_ada):
    depth, d, n = w_ada.shape
    rows = c_rows.shape[0]
    bn = 1024
    assert n % bn == 0
    blocks = _nbytes((rows, d), F32) + _nbytes((d, bn), F32) + _nbytes((1, bn), F32) + _nbytes((rows, bn), F32)
    return pl.pallas_call(
        _adaln_kernel,
        grid=(depth, n // bn),
        in_specs=[
            pl.BlockSpec((rows, d), lambda l, j: (0, 0)),
            pl.BlockSpec((None, d, bn), lambda l, j: (l, 0, j)),
            pl.BlockSpec((None, 1, bn), lambda l, j: (l, 0, j)),
        ],
        out_specs=pl.BlockSpec((None, rows, bn), lambda l, j: (l, 0, j)),
        out_shape=jax.ShapeDtypeStruct((depth, rows, n), F32),
        compiler_params=pltpu.CompilerParams(
            dimension_semantics=("arbitrary", "arbitrary"),
            vmem_limit_bytes=_vmem_limit(blocks, _nbytes((d, bn), BF16))),
        name="adaln",
    )(c_rows, w_ada, b_ada.reshape(depth, 1, n))


def _mm_kernel(x_ref, w_ref, o_ref, *scratch, cast_w):
    if cast_w:
        (wb_ref,) = scratch

        @pl.when(pl.program_id(1) == 0)
        def _():
            wb_ref[...] = w_ref[...].astype(BF16)

        w = wb_ref[...]
    else:
        w = w_ref[...]
    o_ref[...] = jnp.dot(x_ref[...], w, preferred_element_type=F32).astype(o_ref.dtype)


def _matmul(x, w, w_index, *, bm, bn, out_dtype, name):
    m, k = x.shape
    n = w.shape[-1]
    assert w.shape[-2] == k and m % bm == 0 and n % bn == 0, (x.shape, w.shape, bm, bn)
    cast_w = w.dtype != BF16
    lead = len(w_index)
    scratch = [pltpu.VMEM((k, bn), BF16)] if cast_w else []
    blocks = _nbytes((bm, k), x.dtype) + _nbytes((k, bn), w.dtype) + _nbytes((bm, bn), out_dtype)
    return pl.pallas_call(
        functools.partial(_mm_kernel, cast_w=cast_w),
        grid=(n // bn, m // bm),
        in_specs=[
            pl.BlockSpec((bm, k), lambda j, i: (i, 0)),
            pl.BlockSpec((None,) * lead + (k, bn), lambda j, i: tuple(w_index) + (0, j)),
        ],
        out_specs=pl.BlockSpec((bm, bn), lambda j, i: (i, j)),
        out_shape=jax.ShapeDtypeStruct((m, n), out_dtype),
        scratch_shapes=scratch,
        compiler_params=pltpu.CompilerParams(
            dimension_semantics=("arbitrary", "arbitrary"),
            vmem_limit_bytes=_vmem_limit(blocks, _nbytes((k, bn), BF16) if cast_w else 0)),
        name=name,
    )(x, w)


def _ffn_in_kernel(x_ref, wg_ref, wu_ref, o_ref, wgb_ref, wub_ref):
    @pl.when(pl.program_id(1) == 0)
    def _():
        wgb_ref[...] = wg_ref[...].astype(BF16)
        wub_ref[...] = wu_ref[...].astype(BF16)

    x = x_ref[...]
    g = jnp.dot(x, wgb_ref[...], preferred_element_type=F32)
    u = jnp.dot(x, wub_ref[...], preferred_element_type=F32)
    o_ref[...] = (g * jax.nn.sigmoid(g) * u).astype(o_ref.dtype)


def _ffn_in(x, wg, wu, w_index, *, bm, bn):
    m, k = x.shape
    n = wg.shape[-1]
    assert m % bm == 0 and n % bn == 0
    lead = len(w_index)
    wspec = pl.BlockSpec((None,) * lead + (k, bn), lambda j, i: tuple(w_index) + (0, j))
    blocks = _nbytes((bm, k), BF16) + 2 * _nbytes((k, bn), F32) + _nbytes((bm, bn), BF16)
    return pl.pallas_call(
        _ffn_in_kernel,
        grid=(n // bn, m // bm),
        in_specs=[pl.BlockSpec((bm, k), lambda j, i: (i, 0)), wspec, wspec],
        out_specs=pl.BlockSpec((bm, bn), lambda j, i: (i, j)),
        out_shape=jax.ShapeDtypeStruct((m, n), BF16),
        scratch_shapes=[pltpu.VMEM((k, bn), BF16), pltpu.VMEM((k, bn), BF16)],
        compiler_params=pltpu.CompilerParams(
            dimension_semantics=("arbitrary", "arbitrary"),
            vmem_limit_bytes=_vmem_limit(blocks, 2 * _nbytes((k, bn), BF16))),
        name="ffn_in",
    )(x, wg, wu)


def _rms_rows(x):
    return x * lax.rsqrt(jnp.mean(x * x, axis=-1, keepdims=True) + EPS)


def _norm_mod_kernel(h_ref, gpre_ref, scale_ref, shift_ref, u_ref):
    u = _rms_rows(h_ref[...]) * gpre_ref[...] * (1.0 + scale_ref[...]) + shift_ref[...]
    u_ref[...] = u.astype(u_ref.dtype)


def _resid_kernel(h_ref, y_ref, gate_ref, gpost_ref, *rest, res_w, with_next):
    yn = _rms_rows(y_ref[...]) * gpost_ref[...]
    h = h_ref[...] + res_w * gate_ref[...] * yn
    if with_next:
        gpre_ref, scale_ref, shift_ref, hn_ref, u_ref = rest
        u = _rms_rows(h) * gpre_ref[...] * (1.0 + scale_ref[...]) + shift_ref[...]
        u_ref[...] = u.astype(u_ref.dtype)
    else:
        (hn_ref,) = rest
    hn_ref[...] = h


def _row_block(rows_per_seq):
    return min(256, rows_per_seq)


def _seq_vec_spec(d, bm, rows_per_seq):
    return pl.BlockSpec((None, 1, d), lambda i: (i * bm // rows_per_seq, 0, 0))


def _norm_mod(h, gpre, scale, shift, rows_per_seq):
    m, d = h.shape
    bm = _row_block(rows_per_seq)
    row = pl.BlockSpec((bm, d), lambda i: (i, 0))
    vec = pl.BlockSpec((1, d), lambda i: (0, 0))
    sv = _seq_vec_spec(d, bm, rows_per_seq)
    return pl.pallas_call(
        _norm_mod_kernel,
        grid=(m // bm,),
        in_specs=[row, vec, sv, sv],
        out_specs=row,
        out_shape=jax.ShapeDtypeStruct((m, d), BF16),
        compiler_params=pltpu.CompilerParams(dimension_semantics=("arbitrary",)),
        name="norm_mod",
    )(h, gpre, scale, shift)


def _resid(h, y, gate, gpost, nxt, rows_per_seq, res_w):
    m, d = h.shape
    bm = _row_block(rows_per_seq)
    row = pl.BlockSpec((bm, d), lambda i: (i, 0))
    vec = pl.BlockSpec((1, d), lambda i: (0, 0))
    sv = _seq_vec_spec(d, bm, rows_per_seq)
    with_next = nxt is not None
    in_specs = [row, row, sv, vec] + ([vec, sv, sv] if with_next else [])
    out_specs = [row, row] if with_next else [row]
    out_shape = [jax.ShapeDtypeStruct((m, d), F32)] + ([jax.ShapeDtypeStruct((m, d), BF16)] if with_next else [])
    outs = pl.pallas_call(
        functools.partial(_resid_kernel, res_w=res_w, with_next=with_next),
        grid=(m // bm,),
        in_specs=in_specs,
        out_specs=out_specs,
        out_shape=out_shape,
        compiler_params=pltpu.CompilerParams(dimension_semantics=("arbitrary",)),
        name="resid",
    )(h, y, gate, gpost, *(nxt if with_next else ()))
    return (outs[0], outs[1]) if with_next else (outs[0], None)


def _l2_norm(x):
    return x * lax.rsqrt(jnp.sum(x * x, axis=-1, keepdims=True) + L2_EPS)


def _rope(x, pos):
    half = x.shape[-1] // 2
    inv = ROPE_THETA ** (-jnp.arange(half, dtype=F32) / half)
    ang = pos.astype(F32)[:, None] * inv[None, :]
    cos = jnp.cos(ang)[None, :, None, :]
    sin = jnp.sin(ang)[None, :, None, :]
    x1, x2 = x[..., :half], x[..., half:]
    return jnp.concatenate([x1 * cos - x2 * sin, x2 * cos + x1 * sin], axis=-1)


def _causal_conv(x, buf, w):
    t = x.shape[1]
    xp = jnp.concatenate([buf, x], axis=1)
    y = xp[:, 0:t] * w[0]
    for j in range(1, CONV_W):
        y = y + xp[:, j:j + t] * w[j]
    return jax.nn.silu(y), xp[:, t:]


def _to_chunks(x, c, pad):
    x = jnp.pad(x, [(0, 0), (0, pad)] + [(0, 0)] * (x.ndim - 2))
    b, tp = x.shape[:2]
    x = x.reshape((b, tp // c, c) + x.shape[2:])
    return jnp.transpose(x, (1, 0, 3, 2) + tuple(range(4, x.ndim)))


def _from_chunks(o, t):
    n, b, h, c, d = o.shape
    return jnp.transpose(o, (1, 0, 3, 2, 4)).reshape(b, n * c, h, d)[:, :t]


def _chunk_decay(gc):
    c = gc.shape[-1]
    tri = jnp.tril(jnp.ones((c, c), bool))
    diff = gc[..., :, None] - gc[..., None, :]
    return jnp.where(tri, jnp.exp(jnp.where(tri, diff, 0.0)), 0.0)


def _gated_delta_rule(q, k, v, g, beta, s0):
    t, dk = q.shape[1], q.shape[-1]
    dv = v.shape[-1]
    c = min(CHUNK, t)
    pad = (-t) % c
    q = _to_chunks(q * dk ** -0.5, c, pad)
    k = _to_chunks(k, c, pad)
    v = _to_chunks(v, c, pad)
    beta = _to_chunks(beta, c, pad)
    gc = jnp.cumsum(_to_chunks(g, c, pad), axis=-1)
    decay = _chunk_decay(gc)
    kb = k * beta[..., None]
    strict = jnp.tril(jnp.ones((c, c), bool), -1)
    lower = jnp.where(strict, jnp.einsum('nbhik,nbhjk->nbhij', kb, k) * decay, 0.0) + jnp.eye(c, dtype=F32)
    rhs = jnp.concatenate([v * beta[..., None], kb * jnp.exp(gc)[..., None]], axis=-1)
    sol = lax.linalg.triangular_solve(lower, rhs, left_side=True, lower=True, unit_diagonal=True)
    u, w = sol[..., :dv], sol[..., dv:]

    def step(S, xs):
        qc, kc, uc, wc, gcc, dc = xs
        v_new = uc - jnp.einsum('bhik,bhkv->bhiv', wc, S)
        intra = jnp.einsum('bhik,bhjk->bhij', qc, kc) * dc
        o = (jnp.einsum('bhik,bhkv->bhiv', qc * jnp.exp(gcc)[..., None], S)
             + jnp.einsum('bhij,bhjv->bhiv', intra, v_new))
        gl = gcc[..., -1:]
        S = S * jnp.exp(gl)[..., None] + jnp.einsum('bhjk,bhjv->bhkv', kc * jnp.exp(gl - gcc)[..., None], v_new)
        return S, o

    S, o = lax.scan(step, s0, (q, k, u, w, gc, decay))
    return _from_chunks(o, t), S


def _retention(q, k, v, logd, s0):
    t = q.shape[1]
    c = min(CHUNK, t)
    pad = (-t) % c
    q = _to_chunks(q, c, pad)
    k = _to_chunks(k, c, pad)
    v = _to_chunks(v, c, pad)
    gc = jnp.cumsum(_to_chunks(logd, c, pad), axis=-1)
    decay = _chunk_decay(gc)

    def step(S, xs):
        qc, kc, vc, gcc, dc = xs
        intra = jnp.einsum('bhik,bhjk->bhij', qc, kc) * dc
        o = (jnp.einsum('bhik,bhkv->bhiv', qc * jnp.exp(gcc)[..., None], S)
             + jnp.einsum('bhij,bhjv->bhiv', intra, vc))
        gl = gcc[..., -1:]
        S = S * jnp.exp(gl)[..., None] + jnp.einsum('bhjk,bhjv->bhkv', kc * jnp.exp(gl - gcc)[..., None], vc)
        return S, o

    S, o = lax.scan(step, s0, (q, k, v, gc, decay))
    return _from_chunks(o, t), S


def _gather_rows(rows, idx):
    return jax.vmap(lambda r, i: r[i])(rows, idx)


def _paged_gather(pool, page_table, new_rows, idx):
    page = pool.shape[1]
    n_past = page_table.shape[1] * page
    t = new_rows.shape[1]
    pidx = jnp.minimum(idx, n_past - 1)
    phys = jax.vmap(lambda pt, i: pt[i // page])(page_table, pidx)
    past_rows = pool[phys, pidx % page]
    new_r = _gather_rows(new_rows, jnp.clip(idx - n_past, 0, t - 1))
    is_past = (idx < n_past).reshape(idx.shape + (1,) * (past_rows.ndim - idx.ndim))
    return jnp.where(is_past, past_rows, new_r)


def _sparse_attention(q, iq, iw, kidx_all, qpos, gather_kv, k_sel):
    b, t, h_b, _ = q.shape
    qb_sz = min(Q_BLOCK, t)
    pad = (-t) % qb_sz
    nb = (t + pad) // qb_sz

    def blocks(a):
        a = jnp.pad(a, [(0, 0), (0, pad)] + [(0, 0)] * (a.ndim - 2))
        return jnp.moveaxis(a.reshape((b, nb, qb_sz) + a.shape[2:]), 1, 0)

    qpos_b = jnp.pad(qpos, (0, pad), mode='edge').reshape(nb, qb_sz)
    kpos = jnp.arange(kidx_all.shape[1], dtype=jnp.int32)

    def one_block(args):
        qb, iqb, iwb, pb = args
        rel = jax.nn.relu(jnp.einsum('bqhd,bsd->bqhs', iqb, kidx_all))
        score = jnp.einsum('bqhs,bqh->bqs', rel, iwb)
        visible = kpos[None, :] <= pb[:, None]
        score = jnp.where(visible[None], score, -jnp.inf)
        _, idx = lax.top_k(score, k_sel)
        valid = idx <= pb[None, :, None]
        kr, vr = gather_kv(idx)
        kv_b = kr.shape[3]
        qg = qb.reshape(b, qb_sz, kv_b, h_b // kv_b, HEAD_DIM)
        logits = jnp.einsum('bqngd,bqsnd->bqngs', qg, kr) * HEAD_DIM ** -0.5
        logits = jnp.where(valid[:, :, None, None, :], logits, -jnp.inf)
        p = jax.nn.softmax(logits, axis=-1)
        o = jnp.einsum('bqngs,bqsnd->bqngd', p, vr)
        return o.reshape(b, qb_sz, h_b, HEAD_DIM)

    o = lax.map(one_block, (blocks(q), blocks(iq), blocks(iw), qpos_b))
    return jnp.moveaxis(o, 0, 1).reshape(b, nb * qb_sz, h_b, HEAD_DIM)[:, :t]


def _token_mixer(parts, dims, conv_w, a_log, dt_bias, gdn_g, past):
    (qkv_a, za, qb, kb, vb, iq, qc, kc, vc, gc, aa, ba, ik, iw) = parts
    h_a, h_b, h_c, kv_b, idx_dim = dims
    b, t = qkv_a.shape[:2]
    if past is None:
        conv_buf = jnp.zeros((b, CONV_W - 1, qkv_a.shape[-1]), F32)
        s_gdn = jnp.zeros((b, h_a, HEAD_DIM, HEAD_DIM), F32)
        s_ret = jnp.zeros((b, h_c, HEAD_DIM, HEAD_DIM), F32)
        n_past = 0
    else:
        conv_buf, s_gdn, s_ret, pool_k, pool_v, pool_kidx, page_table = past
        n_past = page_table.shape[1] * pool_k.shape[1]
    pos = n_past + jnp.arange(t, dtype=jnp.int32)

    conv_out, conv_new = _causal_conv(qkv_a, conv_buf, conv_w)
    qa, ka, va = [x.reshape(b, t, h_a, HEAD_DIM) for x in jnp.split(conv_out, 3, axis=-1)]
    g = -jnp.exp(a_log) * jax.nn.softplus(aa + dt_bias)
    oa, s_gdn_new = _gated_delta_rule(_l2_norm(qa), _l2_norm(ka), va, g, jax.nn.sigmoid(ba), s_gdn)
    oa = _rms_rows(oa) * gdn_g * jax.nn.silu(za.reshape(b, t, h_a, HEAD_DIM))

    qb = _rope(qb.reshape(b, t, h_b, HEAD_DIM), pos)
    kb = _rope(kb.reshape(b, t, kv_b, HEAD_DIM), pos)
    vb = vb.reshape(b, t, kv_b, HEAD_DIM)
    iq = _rope(iq.reshape(b, t, IDX_HEADS, idx_dim), pos)
    ik = _rope(ik.reshape(b, t, 1, idx_dim), pos)[:, :, 0]
    iw = iw * (IDX_HEADS ** -0.5 * idx_dim ** -0.5)
    if past is None:
        kidx_all = ik
        gather_kv = lambda idx: (_gather_rows(kb, idx), _gather_rows(vb, idx))
    else:
        past_kidx = pool_kidx[page_table].reshape(b, n_past, idx_dim)
        kidx_all = jnp.concatenate([past_kidx, ik], axis=1)
        gather_kv = lambda idx: (_paged_gather(pool_k, page_table, kb, idx),
                                 _paged_gather(pool_v, page_table, vb, idx))
    k_sel = max(1, min(TOPK_MAX, kidx_all.shape[1] // 4))
    ob = _sparse_attention(qb, iq, iw, kidx_all, pos, gather_kv, k_sel)

    qc = _rope(qc.reshape(b, t, h_c, HEAD_DIM), pos)
    kc = _rope(kc.reshape(b, t, h_c, HEAD_DIM), pos) * HEAD_DIM ** -0.5
    vc = vc.reshape(b, t, h_c, HEAD_DIM)
    log_gamma = jnp.log1p(-jnp.exp2(RET_GAMMA_BASE - jnp.arange(h_c, dtype=F32)))
    logd = jnp.broadcast_to(log_gamma, (b, t, h_c))
    oc, s_ret_new = _retention(qc, kc, vc, logd, s_ret)
    oc = _rms_rows(oc) * jax.nn.silu(gc.reshape(b, t, h_c, HEAD_DIM))

    o = jnp.concatenate([oa.reshape(b, t, -1), ob.reshape(b, t, -1), oc.reshape(b, t, -1)], axis=-1)
    return o, (kb, vb, ik, s_gdn_new, conv_new, s_ret_new)


def _in_proj_layout(d_model, kv_b, idx_dim):
    d_mix = d_model
    h_a = (3 * d_mix) // (8 * HEAD_DIM)
    h_b = h_a
    h_c = d_mix // HEAD_DIM - h_a - h_b
    sizes = (3 * h_a * HEAD_DIM, h_a * HEAD_DIM, h_a, h_a,
             h_b * HEAD_DIM, kv_b * HEAD_DIM, kv_b * HEAD_DIM, IDX_HEADS * idx_dim, idx_dim, IDX_HEADS,
             h_c * HEAD_DIM, h_c * HEAD_DIM, h_c * HEAD_DIM, h_c * HEAD_DIM)
    names = ('qkv_a', 'za', 'aa', 'ba', 'qb', 'kb', 'vb', 'iq', 'ik', 'iw', 'qc', 'kc', 'vc', 'gc')
    offs, off = {}, 0
    for nm, s in zip(names, sizes):
        offs[nm] = (off, s)
        off += s
    order = ('qkv_a', 'za', 'qb', 'kb', 'vb', 'iq', 'qc', 'kc', 'vc', 'gc', 'aa', 'ba', 'ik', 'iw')
    return (h_a, h_b, h_c), offs, order, off


def _rearrange_w_in(w_in, offs, order, bn):
    cols = [w_in[..., offs[nm][0]:offs[nm][0] + offs[nm][1]] for nm in order]
    n = sum(offs[nm][1] for nm in order)
    n_pad = -(-n // bn) * bn
    if n_pad > n:
        cols.append(jnp.zeros(w_in.shape[:-1] + (n_pad - n,), w_in.dtype))
    return jnp.concatenate(cols, axis=-1).astype(BF16)


def _split_proj(p, offs, order, b, t):
    parts, off = [], 0
    for nm in order:
        s = offs[nm][1]
        parts.append(p[:, off:off + s].reshape(b, t, s))
        off += s
    return parts


def _forward(x, mod, weights, past, cfg):
    (norm_pre, norm_post, w_in_r, gdn_conv_w, gdn_a_log, gdn_dt_bias, gdn_norm,
     w_out, ffn_w_gate, ffn_w_up, ffn_w_down) = weights
    heads, offs, order, kv_b, idx_dim = cfg
    b, t, d = x.shape
    depth = w_out.shape[0]
    m = b * t
    bm = min(1024, m)
    bm_down = min(256, m)
    h = x.reshape(m, d)
    vec = lambda l, s, j: mod[l, :, s, j].reshape(b, 1, d)
    gpre = lambda l, s: norm_pre[l, s].reshape(1, d)
    gpost = lambda l, s: norm_post[l, s].reshape(1, d)
    states = []
    u = _norm_mod(h, gpre(0, 0), vec(0, 0, 1), vec(0, 0, 0), t)
    for l in range(depth):
        hf = _ffn_in(u, ffn_w_gate, ffn_w_up, (l, 0), bm=bm, bn=MXU_COLS)
        y = _matmul(hf, ffn_w_down, (l, 0), bm=bm_down, bn=MXU_COLS, out_dtype=F32, name="ffn_down")
        h, u = _resid(h, y, vec(l, 0, 2), gpost(l, 0), (gpre(l, 1), vec(l, 1, 1), vec(l, 1, 0)), t, FFN_RES)

        p = _matmul(u, w_in_r, (l,), bm=bm, bn=2 * MXU_COLS, out_dtype=F32, name="in_proj")
        parts = _split_proj(p, offs, order, b, t)
        if past is None:
            past_l = None
        else:
            cache_k, cache_v, cache_kidx, state_gdn, state_gdn_conv, state_ret, page_table = past
            past_l = (state_gdn_conv[l], state_gdn[l], state_ret[l], cache_k[l], cache_v[l], cache_kidx[l], page_table)
        o, st = _token_mixer(parts, heads + (kv_b, idx_dim), gdn_conv_w[l], gdn_a_log[l], gdn_dt_bias[l],
                             gdn_norm[l], past_l)
        states.append(st)
        y = _matmul(o.reshape(m, d).astype(BF16), w_out, (l,), bm=bm, bn=MXU_COLS, out_dtype=F32, name="out_proj")
        h, u = _resid(h, y, vec(l, 1, 2), gpost(l, 1), (gpre(l, 2), vec(l, 2, 1), vec(l, 2, 0)), t, 1.0)

        hf = _ffn_in(u, ffn_w_gate, ffn_w_up, (l, 1), bm=bm, bn=MXU_COLS)
        y = _matmul(hf, ffn_w_down, (l, 1), bm=bm_down, bn=MXU_COLS, out_dtype=F32, name="ffn_down")
        nxt = (gpre(l + 1, 0), vec(l + 1, 0, 1), vec(l + 1, 0, 0)) if l + 1 < depth else None
        h, u = _resid(h, y, vec(l, 2, 2), gpost(l, 2), nxt, t, FFN_RES)
    new_state = [jnp.stack([s[i] for s in states]) for i in range(6)]
    return h.reshape(b, t, d), new_state


def kernel(x_prompt, x_sample, cache_k, cache_v, cache_kidx, state_gdn, state_gdn_conv, state_ret, page_table,
           c_prompt, c_sample, w_ada, b_ada, norm_pre, norm_post, w_in, gdn_conv_w, gdn_a_log, gdn_dt_bias,
           gdn_norm, w_out, ffn_w_gate, ffn_w_up, ffn_w_down):
    d = x_prompt.shape[-1]
    depth = w_ada.shape[0]
    kv_b = cache_k.shape[3]
    idx_dim = cache_kidx.shape[-1]
    heads, offs, order, n_in = _in_proj_layout(d, kv_b, idx_dim)
    assert n_in == w_in.shape[-1]
    bp, bs = c_prompt.shape[0], c_sample.shape[0]

    rows = -(-(bp + bs) // 8) * 8
    c_rows = jnp.concatenate([c_prompt, c_sample, jnp.zeros((rows - bp - bs, d), F32)], axis=0)
    mod = _adaln(c_rows, w_ada, b_ada).reshape(depth, rows, N_SUB, 3, d)

    w_in_r = _rearrange_w_in(w_in, offs, order, 2 * MXU_COLS)
    weights = (norm_pre, norm_post, w_in_r, gdn_conv_w, gdn_a_log, gdn_dt_bias, gdn_norm,
               w_out, ffn_w_gate, ffn_w_up, ffn_w_down)
    cfg = (heads, offs, order, kv_b, idx_dim)
    y_p, st_p = _forward(x_prompt, mod[:, :bp], weights, None, cfg)
    past = (cache_k, cache_v, cache_kidx, state_gdn, state_gdn_conv, state_ret, page_table)
    y_s, st_s = _forward(x_sample, mod[:, bp:bp + bs], weights, past, cfg)
    return (y_p, y_s, *st_p, *st_s)
```

```python
import functools

import jax
import jax.numpy as jnp
from jax import lax
from jax.experimental import pallas as pl
from jax.experimental.pallas import tpu as pltpu

F32 = jnp.float32
BF16 = jnp.bfloat16

HEAD_DIM = 128
IDX_HEADS = 32
TOPK_MAX = 256
CONV_W = 4
FFN_RES = 0.5
N_SUB = 3
CHUNK = 64
Q_BLOCK = 128
ROPE_THETA = 10000.0
RET_GAMMA_BASE = -5.0
EPS = 1e-6
L2_EPS = 1e-6

V7X_VMEM_BYTES = 64 * 1024 * 1024
V7X_VMEM_BUDGET = V7X_VMEM_BYTES - 8 * 1024 * 1024
LANE = 128
MXU_COLS = 256


def _vmem_limit(block_bytes, scratch_bytes):
    need = 2 * block_bytes + scratch_bytes + 8 * 1024 * 1024
    assert need <= V7X_VMEM_BUDGET, need
    return need


def _nbytes(shape, dtype):
    n = 1
    for s in shape:
        n *= s
    return n * jnp.dtype(dtype).itemsize


def _adaln_kernel(c_ref, w_ref, b_ref, o_ref):
    c = c_ref[...]
    cs = (c * jax.nn.sigmoid(c)).astype(BF16)
    w = w_ref[...].astype(BF16)
    o_ref[...] = jnp.dot(cs, w, preferred_element_type=F32) + b_ref[...]


def _adaln(c_rows, w_ada, b_ada):
    depth, d, n = w_ada.shape
    rows = c_rows.shape[0]
    bn = 1024
    assert n % bn == 0
    blocks = _nbytes((rows, d), F32) + _nbytes((d, bn), F32) + _nbytes((1, bn), F32) + _nbytes((rows, bn), F32)
    return pl.pallas_call(
        _adaln_kernel,
        grid=(depth, n // bn),
        in_specs=[
            pl.BlockSpec((rows, d), lambda l, j: (0, 0)),
            pl.BlockSpec((None, d, bn), lambda l, j: (l, 0, j)),
            pl.BlockSpec((None, 1, bn), lambda l, j: (l, 0, j)),
        ],
        out_specs=pl.BlockSpec((None, rows, bn), lambda l, j: (l, 0, j)),
        out_shape=jax.ShapeDtypeStruct((depth, rows, n), F32),
        compiler_params=pltpu.CompilerParams(
            dimension_semantics=("arbitrary", "arbitrary"),
            vmem_limit_bytes=_vmem_limit(blocks, _nbytes((d, bn), BF16))),
        name="adaln",
    )(c_rows, w_ada, b_ada.reshape(depth, 1, n))


def _mm_kernel(x_ref, w_ref, o_ref, *scratch, cast_w):
    if cast_w:
        (wb_ref,) = scratch

        @pl.when(pl.program_id(1) == 0)
        def _():
            wb_ref[...] = w_ref[...].astype(BF16)

        w = wb_ref[...]
    else:
        w = w_ref[...]
    o_ref[...] = jnp.dot(x_ref[...], w, preferred_element_type=F32).astype(o_ref.dtype)


def _matmul(x, w, w_index, *, bm, bn, out_dtype, name):
    m, k = x.shape
    n = w.shape[-1]
    assert w.shape[-2] == k and m % bm == 0 and n % bn == 0, (x.shape, w.shape, bm, bn)
    cast_w = w.dtype != BF16
    lead = len(w_index)
    scratch = [pltpu.VMEM((k, bn), BF16)] if cast_w else []
    blocks = _nbytes((bm, k), x.dtype) + _nbytes((k, bn), w.dtype) + _nbytes((bm, bn), out_dtype)
    return pl.pallas_call(
        functools.partial(_mm_kernel, cast_w=cast_w),
        grid=(n // bn, m // bm),
        in_specs=[
            pl.BlockSpec((bm, k), lambda j, i: (i, 0)),
            pl.BlockSpec((None,) * lead + (k, bn), lambda j, i: tuple(w_index) + (0, j)),
        ],
        out_specs=pl.BlockSpec((bm, bn), lambda j, i: (i, j)),
        out_shape=jax.ShapeDtypeStruct((m, n), out_dtype),
        scratch_shapes=scratch,
        compiler_params=pltpu.CompilerParams(
            dimension_semantics=("arbitrary", "arbitrary"),
            vmem_limit_bytes=_vmem_limit(blocks, _nbytes((k, bn), BF16) if cast_w else 0)),
        name=name,
    )(x, w)


def _ffn_in_kernel(x_ref, wg_ref, wu_ref, o_ref, wgb_ref, wub_ref):
    @pl.when(pl.program_id(1) == 0)
    def _():
        wgb_ref[...] = wg_ref[...].astype(BF16)
        wub_ref[...] = wu_ref[...].astype(BF16)

    x = x_ref[...]
    g = jnp.dot(x, wgb_ref[...], preferred_element_type=F32)
    u = jnp.dot(x, wub_ref[...], preferred_element_type=F32)
    o_ref[...] = (g * jax.nn.sigmoid(g) * u).astype(o_ref.dtype)


def _ffn_in(x, wg, wu, w_index, *, bm, bn):
    m, k = x.shape
    n = wg.shape[-1]
    assert m % bm == 0 and n % bn == 0
    lead = len(w_index)
    wspec = pl.BlockSpec((None,) * lead + (k, bn), lambda j, i: tuple(w_index) + (0, j))
    blocks = _nbytes((bm, k), BF16) + 2 * _nbytes((k, bn), F32) + _nbytes((bm, bn), BF16)
    return pl.pallas_call(
        _ffn_in_kernel,
        grid=(n // bn, m // bm),
        in_specs=[pl.BlockSpec((bm, k), lambda j, i: (i, 0)), wspec, wspec],
        out_specs=pl.BlockSpec((bm, bn), lambda j, i: (i, j)),
        out_shape=jax.ShapeDtypeStruct((m, n), BF16),
        scratch_shapes=[pltpu.VMEM((k, bn), BF16), pltpu.VMEM((k, bn), BF16)],
        compiler_params=pltpu.CompilerParams(
            dimension_semantics=("arbitrary", "arbitrary"),
            vmem_limit_bytes=_vmem_limit(blocks, 2 * _nbytes((k, bn), BF16))),
        name="ffn_in",
    )(x, wg, wu)


def _rms_rows(x):
    return x * lax.rsqrt(jnp.mean(x * x, axis=-1, keepdims=True) + EPS)


def _norm_mod_kernel(h_ref, gpre_ref, scale_ref, shift_ref, u_ref):
    u = _rms_rows(h_ref[...]) * gpre_ref[...] * (1.0 + scale_ref[...]) + shift_ref[...]
    u_ref[...] = u.astype(u_ref.dtype)


def _resid_kernel(h_ref, y_ref, gate_ref, gpost_ref, *rest, res_w, with_next):
    yn = _rms_rows(y_ref[...]) * gpost_ref[...]
    h = h_ref[...] + res_w * gate_ref[...] * yn
    if with_next:
        gpre_ref, scale_ref, shift_ref, hn_ref, u_ref = rest
        u = _rms_rows(h) * gpre_ref[...] * (1.0 + scale_ref[...]) + shift_ref[...]
        u_ref[...] = u.astype(u_ref.dtype)
    else:
        (hn_ref,) = rest
    hn_ref[...] = h


def _row_block(rows_per_seq):
    return min(256, rows_per_seq)


def _seq_vec_spec(d, bm, rows_per_seq):
    return pl.BlockSpec((None, 1, d), lambda i: (i * bm // rows_per_seq, 0, 0))


def _norm_mod(h, gpre, scale, shift, rows_per_seq):
    m, d = h.shape
    bm = _row_block(rows_per_seq)
    row = pl.BlockSpec((bm, d), lambda i: (i, 0))
    vec = pl.BlockSpec((1, d), lambda i: (0, 0))
    sv = _seq_vec_spec(d, bm, rows_per_seq)
    return pl.pallas_call(
        _norm_mod_kernel,
        grid=(m // bm,),
        in_specs=[row, vec, sv, sv],
        out_specs=row,
        out_shape=jax.ShapeDtypeStruct((m, d), BF16),
        compiler_params=pltpu.CompilerParams(dimension_semantics=("arbitrary",)),
        name="norm_mod",
    )(h, gpre, scale, shift)


def _resid(h, y, gate, gpost, nxt, rows_per_seq, res_w):
    m, d = h.shape
    bm = _row_block(rows_per_seq)
    row = pl.BlockSpec((bm, d), lambda i: (i, 0))
    vec = pl.BlockSpec((1, d), lambda i: (0, 0))
    sv = _seq_vec_spec(d, bm, rows_per_seq)
    with_next = nxt is not None
    in_specs = [row, row, sv, vec] + ([vec, sv, sv] if with_next else [])
    out_specs = [row, row] if with_next else [row]
    out_shape = [jax.ShapeDtypeStruct((m, d), F32)] + ([jax.ShapeDtypeStruct((m, d), BF16)] if with_next else [])
    outs = pl.pallas_call(
        functools.partial(_resid_kernel, res_w=res_w, with_next=with_next),
        grid=(m // bm,),
        in_specs=in_specs,
        out_specs=out_specs,
        out_shape=out_shape,
        compiler_params=pltpu.CompilerParams(dimension_semantics=("arbitrary",)),
        name="resid",
    )(h, y, gate, gpost, *(nxt if with_next else ()))
    return (outs[0], outs[1]) if with_next else (outs[0], None)


def _l2_norm(x):
    return x * lax.rsqrt(jnp.sum(x * x, axis=-1, keepdims=True) + L2_EPS)


def _rope(x, pos):
    half = x.shape[-1] // 2
    inv = ROPE_THETA ** (-jnp.arange(half, dtype=F32) / half)
    ang = pos.astype(F32)[:, None] * inv[None, :]
    cos = jnp.cos(ang)[None, :, None, :]
    sin = jnp.sin(ang)[None, :, None, :]
    x1, x2 = x[..., :half], x[..., half:]
    return jnp.concatenate([x1 * cos - x2 * sin, x2 * cos + x1 * sin], axis=-1)


def _causal_conv(x, buf, w):
    t = x.shape[1]
    xp = jnp.concatenate([buf, x], axis=1)
    y = xp[:, 0:t] * w[0]
    for j in range(1, CONV_W):
        y = y + xp[:, j:j + t] * w[j]
    return jax.nn.silu(y), xp[:, t:]


def _to_chunks(x, c, pad):
    x = jnp.pad(x, [(0, 0), (0, pad)] + [(0, 0)] * (x.ndim - 2))
    b, tp = x.shape[:2]
    x = x.reshape((b, tp // c, c) + x.shape[2:])
    return jnp.transpose(x, (1, 0, 3, 2) + tuple(range(4, x.ndim)))


def _from_chunks(o, t):
    n, b, h, c, d = o.shape
    return jnp.transpose(o, (1, 0, 3, 2, 4)).reshape(b, n * c, h, d)[:, :t]


def _chunk_decay(gc):
    c = gc.shape[-1]
    tri = jnp.tril(jnp.ones((c, c), bool))
    diff = gc[..., :, None] - gc[..., None, :]
    return jnp.where(tri, jnp.exp(jnp.where(tri, diff, 0.0)), 0.0)


def _gated_delta_rule(q, k, v, g, beta, s0):
    t, dk = q.shape[1], q.shape[-1]
    dv = v.shape[-1]
    c = min(CHUNK, t)
    pad = (-t) % c
    q = _to_chunks(q * dk ** -0.5, c, pad)
    k = _to_chunks(k, c, pad)
    v = _to_chunks(v, c, pad)
    beta = _to_chunks(beta, c, pad)
    gc = jnp.cumsum(_to_chunks(g, c, pad), axis=-1)
    decay = _chunk_decay(gc)
    kb = k * beta[..., None]
    strict = jnp.tril(jnp.ones((c, c), bool), -1)
    lower = jnp.where(strict, jnp.einsum('nbhik,nbhjk->nbhij', kb, k) * decay, 0.0) + jnp.eye(c, dtype=F32)
    rhs = jnp.concatenate([v * beta[..., None], kb * jnp.exp(gc)[..., None]], axis=-1)
    sol = lax.linalg.triangular_solve(lower, rhs, left_side=True, lower=True, unit_diagonal=True)
    u, w = sol[..., :dv], sol[..., dv:]

    def step(S, xs):
        qc, kc, uc, wc, gcc, dc = xs
        v_new = uc - jnp.einsum('bhik,bhkv->bhiv', wc, S)
        intra = jnp.einsum('bhik,bhjk->bhij', qc, kc) * dc
        o = (jnp.einsum('bhik,bhkv->bhiv', qc * jnp.exp(gcc)[..., None], S)
             + jnp.einsum('bhij,bhjv->bhiv', intra, v_new))
        gl = gcc[..., -1:]
        S = S * jnp.exp(gl)[..., None] + jnp.einsum('bhjk,bhjv->bhkv', kc * jnp.exp(gl - gcc)[..., None], v_new)
        return S, o

    S, o = lax.scan(step, s0, (q, k, u, w, gc, decay))
    return _from_chunks(o, t), S


def _scan_retention(q, k, v, logd, s0):
    t = q.shape[1]
    c = min(CHUNK, t)
    pad = (-t) % c
    q = _to_chunks(q, c, pad)
    k = _to_chunks(k, c, pad)
    v = _to_chunks(v, c, pad)
    gc = jnp.cumsum(_to_chunks(logd, c, pad), axis=-1)
    decay = _chunk_decay(gc)

    def step(S, xs):
        qc, kc, vc, gcc, dc = xs
        intra = jnp.einsum('bhik,bhjk->bhij', qc, kc) * dc
        o = (jnp.einsum('bhik,bhkv->bhiv', qc * jnp.exp(gcc)[..., None], S)
             + jnp.einsum('bhij,bhjv->bhiv', intra, vc))
        gl = gcc[..., -1:]
        S = S * jnp.exp(gl)[..., None] + jnp.einsum('bhjk,bhjv->bhkv', kc * jnp.exp(gl - gcc)[..., None], vc)
        return S, o

    S, o = lax.scan(step, s0, (q, k, v, gc, decay))
    return _from_chunks(o, t), S


def _gather_rows(rows, idx):
    return jax.vmap(lambda r, i: r[i])(rows, idx)


def _paged_gather(pool, page_table, new_rows, idx):
    page = pool.shape[1]
    n_past = page_table.shape[1] * page
    t = new_rows.shape[1]
    pidx = jnp.minimum(idx, n_past - 1)
    phys = jax.vmap(lambda pt, i: pt[i // page])(page_table, pidx)
    past_rows = pool[phys, pidx % page]
    new_r = _gather_rows(new_rows, jnp.clip(idx - n_past, 0, t - 1))
    is_past = (idx < n_past).reshape(idx.shape + (1,) * (past_rows.ndim - idx.ndim))
    return jnp.where(is_past, past_rows, new_r)


def _sparse_attention(q, iq, iw, kidx_all, qpos, gather_kv, k_sel):
    b, t, h_b, _ = q.shape
    qb_sz = min(Q_BLOCK, t)
    pad = (-t) % qb_sz
    nb = (t + pad) // qb_sz

    def blocks(a):
        a = jnp.pad(a, [(0, 0), (0, pad)] + [(0, 0)] * (a.ndim - 2))
        return jnp.moveaxis(a.reshape((b, nb, qb_sz) + a.shape[2:]), 1, 0)

    qpos_b = jnp.pad(qpos, (0, pad), mode='edge').reshape(nb, qb_sz)
    kpos = jnp.arange(kidx_all.shape[1], dtype=jnp.int32)

    def one_block(args):
        qb, iqb, iwb, pb = args
        rel = jax.nn.relu(jnp.einsum('bqhd,bsd->bqhs', iqb, kidx_all))
        score = jnp.einsum('bqhs,bqh->bqs', rel, iwb)
        visible = kpos[None, :] <= pb[:, None]
        score = jnp.where(visible[None], score, -jnp.inf)
        _, idx = lax.top_k(score, k_sel)
        valid = idx <= pb[None, :, None]
        kr, vr = gather_kv(idx)
        kv_b = kr.shape[3]
        qg = qb.reshape(b, qb_sz, kv_b, h_b // kv_b, HEAD_DIM)
        logits = jnp.einsum('bqngd,bqsnd->bqngs', qg, kr) * HEAD_DIM ** -0.5
        logits = jnp.where(valid[:, :, None, None, :], logits, -jnp.inf)
        p = jax.nn.softmax(logits, axis=-1)
        o = jnp.einsum('bqngs,bqsnd->bqngd', p, vr)
        return o.reshape(b, qb_sz, h_b, HEAD_DIM)

    o = lax.map(one_block, (blocks(q), blocks(iq), blocks(iw), qpos_b))
    return jnp.moveaxis(o, 0, 1).reshape(b, nb * qb_sz, h_b, HEAD_DIM)[:, :t]


def _token_mixer(parts, dims, conv_w, a_log, dt_bias, gdn_g, past):
    (qkv_a, za, qb, kb, vb, iq, qc, kc, vc, gc, aa, ba, ik, iw) = parts
    h_a, h_b, h_c, kv_b, idx_dim = dims
    b, t = qkv_a.shape[:2]
    if past is None:
        conv_buf = jnp.zeros((b, CONV_W - 1, qkv_a.shape[-1]), F32)
        s_gdn = jnp.zeros((b, h_a, HEAD_DIM, HEAD_DIM), F32)
        s_ret = jnp.zeros((b, h_c, HEAD_DIM, HEAD_DIM), F32)
        n_past = 0
    else:
        conv_buf, s_gdn, s_ret, pool_k, pool_v, pool_kidx, page_table = past
        n_past = page_table.shape[1] * pool_k.shape[1]
    pos = n_past + jnp.arange(t, dtype=jnp.int32)

    conv_out, conv_new = _causal_conv(qkv_a, conv_buf, conv_w)
    qa, ka, va = [x.reshape(b, t, h_a, HEAD_DIM) for x in jnp.split(conv_out, 3, axis=-1)]
    g = -jnp.exp(a_log) * jax.nn.softplus(aa + dt_bias)
    oa, s_gdn_new = _gated_delta_rule(_l2_norm(qa), _l2_norm(ka), va, g, jax.nn.sigmoid(ba), s_gdn)
    oa = _rms_rows(oa) * gdn_g * jax.nn.silu(za.reshape(b, t, h_a, HEAD_DIM))

    qb = _rope(qb.reshape(b, t, h_b, HEAD_DIM), pos)
    kb = _rope(kb.reshape(b, t, kv_b, HEAD_DIM), pos)
    vb = vb.reshape(b, t, kv_b, HEAD_DIM)
    iq = _rope(iq.reshape(b, t, IDX_HEADS, idx_dim), pos)
    ik = _rope(ik.reshape(b, t, 1, idx_dim), pos)[:, :, 0]
    iw = iw * (IDX_HEADS ** -0.5 * idx_dim ** -0.5)
    if past is None:
        kidx_all = ik
        gather_kv = lambda idx: (_gather_rows(kb, idx), _gather_rows(vb, idx))
    else:
        past_kidx = pool_kidx[page_table].reshape(b, n_past, idx_dim)
        kidx_all = jnp.concatenate([past_kidx, ik], axis=1)
        gather_kv = lambda idx: (_paged_gather(pool_k, page_table, kb, idx),
                                 _paged_gather(pool_v, page_table, vb, idx))
    k_sel = max(1, min(TOPK_MAX, kidx_all.shape[1] // 4))
    ob = _sparse_attention(qb, iq, iw, kidx_all, pos, gather_kv, k_sel)

    qc = _rope(qc.reshape(b, t, h_c, HEAD_DIM), pos)
    kc = _rope(kc.reshape(b, t, h_c, HEAD_DIM), pos) * HEAD_DIM ** -0.5
    vc = vc.reshape(b, t, h_c, HEAD_DIM)
    log_gamma = jnp.log1p(-jnp.exp2(RET_GAMMA_BASE - jnp.arange(h_c, dtype=F32)))
    logd = jnp.broadcast_to(log_gamma, (b, t, h_c))
    oc, s_ret_new = _scan_retention(qc, kc, vc, logd, s_ret)
    oc = _rms_rows(oc) * jax.nn.silu(gc.reshape(b, t, h_c, HEAD_DIM))

    o = jnp.concatenate([oa.reshape(b, t, -1), ob.reshape(b, t, -1), oc.reshape(b, t, -1)], axis=-1)
    return o, (kb, vb, ik, s_gdn_new, conv_new, s_ret_new)


def _rope_tables(pos, head_dim, lanes):
    half = head_dim // 2
    inv = ROPE_THETA ** (-jnp.arange(half, dtype=F32) / half)
    ang = pos.astype(F32)[:, None] * inv[None, :]
    cos, sin = jnp.cos(ang), jnp.sin(ang)
    reps = lanes // head_dim
    return (jnp.tile(jnp.concatenate([cos, cos], axis=-1), (1, reps)),
            jnp.tile(jnp.concatenate([-sin, sin], axis=-1), (1, reps)))


def _rope_head(x, cos_t, sin_t):
    return x * cos_t + pltpu.roll(x, HEAD_DIM // 2, axis=1) * sin_t


def _rope_half_heads(x, cos_t, sin_t):
    lane = lax.broadcasted_iota(jnp.int32, x.shape, 1)
    fwd = pltpu.roll(x, 32, axis=1)
    bwd = pltpu.roll(x, LANE - 32, axis=1)
    return x * cos_t + jnp.where(lane % 64 < 32, bwd, fwd) * sin_t


def _dsa_prep_kernel(qb_ref, kb_ref, vb_ref, iq_ref, sm_ref, cos_ref, sin_ref, cosi_ref, sini_ref,
                     q_ref, kf_ref, kh_ref, vh_ref, iqlo_ref, iqhi_ref, kif_ref, ki2_ref):
    cos_t, sin_t = cos_ref[...], sin_ref[...]
    cosi_t, sini_t = cosi_ref[...], sini_ref[...]
    for h in range(qb_ref.shape[1] // HEAD_DIM):
        sl = slice(h * HEAD_DIM, (h + 1) * HEAD_DIM)
        q_ref[:, sl] = _rope_head(qb_ref[:, sl], cos_t, sin_t).astype(BF16)
    for h in range(kb_ref.shape[1] // HEAD_DIM):
        sl = slice(h * HEAD_DIM, (h + 1) * HEAD_DIM)
        kr = _rope_head(kb_ref[:, sl], cos_t, sin_t)
        kf_ref[:, sl] = kr
        kh_ref[:, sl] = kr.astype(BF16)
    vh_ref[...] = vb_ref[...].astype(BF16)
    lane = lax.broadcasted_iota(jnp.int32, cos_t.shape, 1)
    low = lane < 64
    for c in range(iq_ref.shape[1] // LANE):
        sl = slice(c * LANE, (c + 1) * LANE)
        r = _rope_half_heads(iq_ref[:, sl], cosi_t, sini_t)
        iqlo_ref[:, sl] = jnp.where(low, r, 0.0).astype(BF16)
        iqhi_ref[:, sl] = jnp.where(low, 0.0, r).astype(BF16)
    r = _rope_half_heads(sm_ref[...], cosi_t, sini_t)
    kif_ref[...] = r[:, :64]
    ki2_ref[...] = jnp.where(low, r, pltpu.roll(r, 64, axis=1)).astype(BF16)


def _col_spec(bt, width, off):
    assert off % width == 0, (off, width)
    return pl.BlockSpec((bt, width), lambda i: (i, off // width))


def _dsa_prep(p, lay, t, tables):
    m = p.shape[0]
    bt = min(256, t)
    nt = t // bt
    pos, wid = lay['pos'], lay['size']
    tab = pl.BlockSpec((bt, LANE), lambda i: (i % nt, 0))
    row = lambda w: pl.BlockSpec((bt, w), lambda i: (i, 0))
    wq, wk, wi = wid['qb'], wid['kb'], wid['iq']
    outs = pl.pallas_call(
        _dsa_prep_kernel,
        grid=(m // bt,),
        in_specs=[_col_spec(bt, wq, pos['qb']), _col_spec(bt, wk, pos['kb']), _col_spec(bt, wk, pos['vb']),
                  _col_spec(bt, wi, pos['iq']), _col_spec(bt, LANE, lay['small']), tab, tab, tab, tab],
        out_specs=[row(wq), row(wk), row(wk), row(wk), row(wi), row(wi), row(64), row(LANE)],
        out_shape=[jax.ShapeDtypeStruct((m, wq), BF16), jax.ShapeDtypeStruct((m, wk), F32),
                   jax.ShapeDtypeStruct((m, wk), BF16), jax.ShapeDtypeStruct((m, wk), BF16),
                   jax.ShapeDtypeStruct((m, wi), BF16), jax.ShapeDtypeStruct((m, wi), BF16),
                   jax.ShapeDtypeStruct((m, 64), F32), jax.ShapeDtypeStruct((m, LANE), BF16)],
        compiler_params=pltpu.CompilerParams(dimension_semantics=("arbitrary",)),
        name="dsa_prep",
    )(p, p, p, p, p, *tables)
    return outs


INT32_MIN = -2 ** 31


def _sortable_key(x):
    b = lax.bitcast_convert_type(x, jnp.int32)
    return b ^ ((b >> 31) & jnp.int32(0x7FFFFFFF))


def _kth_largest_key(keys_ref, k_sel):
    kf = jnp.float32(k_sel)

    def count_ge(t):
        return jnp.sum(jnp.where(keys_ref[...] >= t, 1.0, 0.0), axis=1, keepdims=True)

    rows = keys_ref.shape[0]
    zero = jnp.zeros((rows, 1), jnp.int32)
    t0 = jnp.where(count_ge(zero) >= kf, zero, jnp.int32(INT32_MIN))

    def body(i, t):
        cand = t | (jnp.int32(1) << (30 - i))
        return jnp.where(count_ge(cand) >= kf, cand, t)

    return lax.fori_loop(0, 31, body, t0)


def _dsa_kernel(q_ref, k_ref, v_ref, iqlo_ref, iqhi_ref, ki2_ref, sm_ref, o_ref, score_ref, keys_ref,
                *, k_sel, kv_b, idx_dim, iw_lane):
    qb = q_ref.shape[0]
    n_keys = k_ref.shape[0]
    group = q_ref.shape[1] // HEAD_DIM // kv_b
    q0 = pl.program_id(1) * qb
    ki2 = ki2_ref[...]
    w_scale = IDX_HEADS ** -0.5 * idx_dim ** -0.5
    heads_per_dot = 8
    score_ref[...] = jnp.zeros_like(score_ref)
    for gi in range(IDX_HEADS // heads_per_dot):
        rows = []
        for c in range(heads_per_dot // 2):
            sl = slice((gi * heads_per_dot // 2 + c) * LANE, (gi * heads_per_dot // 2 + c + 1) * LANE)
            rows.append(iqlo_ref[:, sl])
            rows.append(iqhi_ref[:, sl])
        lhs = jnp.concatenate(rows, axis=0)
        d = lax.dot_general(lhs, ki2, (((1,), (1,)), ((), ())), preferred_element_type=F32)
        acc = score_ref[...]
        for hh in range(heads_per_dot):
            lane = iw_lane + gi * heads_per_dot + hh
            w = sm_ref[:, lane:lane + 1] * w_scale
            acc = acc + jnp.maximum(d[hh * qb:(hh + 1) * qb], 0.0) * w
        score_ref[...] = acc

    qpos = q0 + lax.broadcasted_iota(jnp.int32, (qb, n_keys), 0)
    kpos = lax.broadcasted_iota(jnp.int32, (qb, n_keys), 1)
    visible = kpos <= qpos
    keys_ref[...] = jnp.where(visible, _sortable_key(score_ref[...]), jnp.int32(INT32_MIN))
    thr = _kth_largest_key(keys_ref, k_sel)
    neg = jnp.float32(-jnp.inf)
    score_ref[...] = jnp.where(visible, jnp.where(keys_ref[...] >= thr, 0.0, neg), neg)

    scale = HEAD_DIM ** -0.5
    for n in range(kv_b):
        q3 = jnp.concatenate(
            [q_ref[:, (n * group + g) * HEAD_DIM:(n * group + g + 1) * HEAD_DIM] for g in range(group)], axis=0)
        kn = k_ref[:, n * HEAD_DIM:(n + 1) * HEAD_DIM]
        lg = lax.dot_general(q3, kn, (((1,), (1,)), ((), ())), preferred_element_type=F32) * scale
        lg = lg.reshape(group, qb, n_keys) + score_ref[...][None]
        mx = jnp.max(lg, axis=-1, keepdims=True)
        pr = jnp.exp(lg - mx)
        den = jnp.sum(pr, axis=-1, keepdims=True)
        o = jnp.dot(pr.reshape(group * qb, n_keys).astype(BF16), v_ref[:, n * HEAD_DIM:(n + 1) * HEAD_DIM],
                    preferred_element_type=F32)
        o = o / den.reshape(group * qb, 1)
        for g in range(group):
            sl = slice((n * group + g) * HEAD_DIM, (n * group + g + 1) * HEAD_DIM)
            o_ref[:, sl] = o[g * qb:(g + 1) * qb].astype(o_ref.dtype)


def _dsa_prompt(q, kh, vh, iqlo, iqhi, ki2, p, lay, b, t, kv_b, idx_dim):
    m, wq = q.shape
    wk, wi = kh.shape[1], iqlo.shape[1]
    qb = min(Q_BLOCK, t)
    nq = t // qb
    k_sel = max(1, min(TOPK_MAX, t // 4))
    qrow = lambda w: pl.BlockSpec((qb, w), lambda bi, qi: (bi * nq + qi, 0))
    seq = lambda w: pl.BlockSpec((t, w), lambda bi, qi: (bi, 0))
    small = pl.BlockSpec((qb, LANE), lambda bi, qi: (bi * nq + qi, lay['small'] // LANE))
    blocks = (_nbytes((qb, wq), BF16) * 2 + 2 * _nbytes((t, wk), BF16) + 2 * _nbytes((qb, wi), BF16)
              + _nbytes((t, LANE), BF16) + _nbytes((qb, LANE), F32))
    temps = 2 * _nbytes((qb, t), F32) + 2 * _nbytes((8 * qb, t), F32)
    return pl.pallas_call(
        functools.partial(_dsa_kernel, k_sel=k_sel, kv_b=kv_b, idx_dim=idx_dim,
                          iw_lane=lay['pos']['iw'] - lay['small']),
        grid=(b, nq),
        in_specs=[qrow(wq), seq(wk), seq(wk), qrow(wi), qrow(wi), seq(LANE), small],
        out_specs=qrow(wq),
        out_shape=jax.ShapeDtypeStruct((m, wq), BF16),
        scratch_shapes=[pltpu.VMEM((qb, t), F32), pltpu.VMEM((qb, t), jnp.int32)],
        compiler_params=pltpu.CompilerParams(
            dimension_semantics=("arbitrary", "arbitrary"),
            vmem_limit_bytes=_vmem_limit(blocks, temps)),
        name="dsa",
    )(q, kh, vh, iqlo, iqhi, ki2, p)


def _gdn_prep_kernel(x_ref, prev_ref, buf_ref, w_ref, q_ref, k_ref, v_ref, xp_ref, *, blocks_per_seq):
    bt = x_ref.shape[0]
    first = pl.program_id(0) % blocks_per_seq == 0
    xp_ref[0:8, :] = jnp.where(first, buf_ref[...], prev_ref[...])
    xp_ref[8:, :] = x_ref[...]
    n_heads = q_ref.shape[1] // HEAD_DIM
    for part, out_ref in enumerate((q_ref, k_ref, v_ref)):
        for h in range(n_heads):
            src = slice((part * n_heads + h) * HEAD_DIM, (part * n_heads + h + 1) * HEAD_DIM)
            y = xp_ref[8 - (CONV_W - 1):8 - (CONV_W - 1) + bt, src] * w_ref[0:1, src]
            for j in range(1, CONV_W):
                y = y + xp_ref[8 - (CONV_W - 1) + j:8 - (CONV_W - 1) + j + bt, src] * w_ref[j:j + 1, src]
            y = y * jax.nn.sigmoid(y)
            if part < 2:
                y = y * lax.rsqrt(jnp.sum(y * y, axis=-1, keepdims=True) + L2_EPS)
            if part == 0:
                y = y * HEAD_DIM ** -0.5
            out_ref[:, h * HEAD_DIM:(h + 1) * HEAD_DIM] = y


def _gdn_prep(p, lay, t, buf8, conv_w):
    m = p.shape[0]
    bt = min(256, t)
    nt = t // bt
    c = lay['size']['qkv_a']
    off = lay['pos']['qkv_a']
    assert off % c == 0
    cb = off // c
    wq = c // 3
    row = pl.BlockSpec((bt, wq), lambda i: (i, 0))
    blocks = _nbytes((bt, c), F32) + 2 * _nbytes((8, c), F32) + _nbytes((CONV_W, c), F32) + 3 * _nbytes((bt, wq), F32)
    return pl.pallas_call(
        functools.partial(_gdn_prep_kernel, blocks_per_seq=nt),
        grid=(m // bt,),
        in_specs=[pl.BlockSpec((bt, c), lambda i: (i, cb)),
                  pl.BlockSpec((8, c), lambda i: (jnp.maximum(i * (bt // 8) - 1, 0), cb)),
                  pl.BlockSpec((None, 8, c), lambda i: (i // nt, 0, 0)),
                  pl.BlockSpec((CONV_W, c), lambda i: (0, 0))],
        out_specs=[row, row, row],
        out_shape=[jax.ShapeDtypeStruct((m, wq), F32)] * 3,
        scratch_shapes=[pltpu.VMEM((bt + 8, c), F32)],
        compiler_params=pltpu.CompilerParams(
            dimension_semantics=("arbitrary",),
            vmem_limit_bytes=_vmem_limit(blocks, _nbytes((bt + 8, c), F32))),
        name="gdn_prep",
    )(p, p, buf8, conv_w)


def _dot_hi(a, b):
    return jnp.dot(a, b, precision=lax.Precision.HIGHEST, preferred_element_type=F32)


def _dot_nt(a, b):
    return lax.dot_general(a, b, (((1,), (1,)), ((), ())), preferred_element_type=F32)


def _dot_tn(a, b):
    return lax.dot_general(a, b, (((0,), (0,)), ((), ())), preferred_element_type=F32)


def _gdn_kernel(q_ref, k_ref, v_ref, g_ref, beta_ref, z_ref, gn_ref, s0_ref, o_ref, s_ref, *, chunk):
    t = q_ref.shape[0]
    hp = q_ref.shape[1] // HEAD_DIM
    c = chunk
    ii = lax.broadcasted_iota(jnp.int32, (c, c), 0)
    jj = lax.broadcasted_iota(jnp.int32, (c, c), 1)
    tril, strict, eye = ii >= jj, ii > jj, ii == jj

    @pl.when(pl.program_id(1) == 0)
    def _():
        s_ref[...] = s0_ref[...]

    gn = gn_ref[...]

    heads = range(hp)
    cols = [slice(j * HEAD_DIM, (j + 1) * HEAD_DIM) for j in heads]

    def body(i, carry):
        r0 = pl.multiple_of(i * c, c)
        g_row = [g_ref[j, pl.ds(i, 1), :] for j in heads]
        gc_col = [jnp.sum(jnp.where(tril, g_row[j], 0.0), axis=1, keepdims=True) for j in heads]
        gc_row = [jnp.sum(jnp.where(eye, gc_col[j], 0.0), axis=0, keepdims=True) for j in heads]
        beta_col = [jnp.sum(jnp.where(eye, beta_ref[j, pl.ds(i, 1), :], 0.0), axis=1, keepdims=True) for j in heads]
        decay = [jnp.where(tril, jnp.exp(jnp.where(tril, gc_col[j] - gc_row[j], 0.0)), 0.0) for j in heads]
        k = [k_ref[pl.ds(r0, c), cols[j]] for j in heads]
        kb = [k[j] * beta_col[j] for j in heads]
        kh = [k[j].astype(BF16) for j in heads]
        a = [jnp.where(strict, _dot_nt(kb[j].astype(BF16), kh[j]) * decay[j], 0.0) for j in heads]
        inv = [jnp.where(eye, 1.0, 0.0) - a[j] for j in heads]
        pw = a
        span = 2
        while span < c:
            pw = [_dot_hi(pw[j], pw[j]) for j in heads]
            inv = [inv[j] + _dot_hi(inv[j], pw[j]) for j in heads]
            span *= 2
        egc = [jnp.exp(gc_col[j]) for j in heads]
        u = [_dot_hi(inv[j], v_ref[pl.ds(r0, c), cols[j]] * beta_col[j]) for j in heads]
        w = [_dot_hi(inv[j], kb[j] * egc[j]) for j in heads]
        s = [s_ref[j] for j in heads]
        sh = [s[j].astype(BF16) for j in heads]
        vnh = [(u[j] - jnp.dot(w[j].astype(BF16), sh[j], preferred_element_type=F32)).astype(BF16) for j in heads]
        gl = [jnp.sum(g_row[j], axis=1, keepdims=True) for j in heads]
        for j in heads:
            s_ref[j] = s[j] * jnp.exp(gl[j]) + _dot_tn((k[j] * jnp.exp(gl[j] - gc_col[j])).astype(BF16), vnh[j])
        q = [q_ref[pl.ds(r0, c), cols[j]] for j in heads]
        intra = [_dot_nt(q[j].astype(BF16), kh[j]) * decay[j] for j in heads]
        o = [jnp.dot((q[j] * egc[j]).astype(BF16), sh[j], preferred_element_type=F32)
             + jnp.dot(intra[j].astype(BF16), vnh[j], preferred_element_type=F32) for j in heads]
        for j in heads:
            z = z_ref[pl.ds(r0, c), cols[j]]
            o_ref[pl.ds(r0, c), cols[j]] = (_rms_rows(o[j]) * gn * (z * jax.nn.sigmoid(z))).astype(o_ref.dtype)
        return carry

    lax.fori_loop(0, t // c, body, 0)


def _gdn_time_block(t):
    return min(4 * CHUNK, t)


def _gdn_gate_rows(x, b, t, h_a, c):
    tb = _gdn_time_block(t)
    x = jnp.transpose(x, (0, 2, 1)).reshape(b, h_a, t // tb, tb // c, c)
    return jnp.transpose(x, (0, 2, 1, 3, 4))


def _gdn(qn, kn, vv, g, beta, p, lay, gdn_g, s0, b, t):
    m, w = qn.shape
    h_a = w // HEAD_DIM
    c = g.shape[-1]
    tb = _gdn_time_block(t)
    nt = t // tb
    zoff = lay['pos']['za']
    assert zoff % w == 0
    seq = pl.BlockSpec((tb, w), lambda bi, ti: (bi * nt + ti, 0))
    gspec = pl.BlockSpec((None, None, h_a, tb // c, c), lambda bi, ti: (bi, ti, 0, 0, 0))
    sspec = pl.BlockSpec((None, h_a, HEAD_DIM, HEAD_DIM), lambda bi, ti: (bi, 0, 0, 0))
    blocks = 4 * _nbytes((tb, w), F32) + _nbytes((tb, w), BF16) + 2 * _nbytes((h_a, HEAD_DIM, HEAD_DIM), F32)
    return pl.pallas_call(
        functools.partial(_gdn_kernel, chunk=c),
        grid=(b, nt),
        in_specs=[seq, seq, seq, gspec, gspec,
                  pl.BlockSpec((tb, w), lambda bi, ti: (bi * nt + ti, zoff // w)),
                  pl.BlockSpec((1, HEAD_DIM), lambda bi, ti: (0, 0)), sspec],
        out_specs=[seq, sspec],
        out_shape=[jax.ShapeDtypeStruct((m, w), BF16), jax.ShapeDtypeStruct(s0.shape, F32)],
        compiler_params=pltpu.CompilerParams(
            dimension_semantics=("arbitrary", "arbitrary"),
            vmem_limit_bytes=_vmem_limit(blocks, 0)),
        name="gdn",
    )(qn, kn, vv, g, beta, p, gdn_g.reshape(1, HEAD_DIM), s0)


def _retention_kernel(q_ref, k_ref, v_ref, gate_ref, cos_ref, sin_ref, lg_ref, s0_ref, o_ref, s_ref, *, chunk):
    t = q_ref.shape[0]
    hp = q_ref.shape[1] // HEAD_DIM
    c = chunk
    ii = lax.broadcasted_iota(jnp.int32, (c, c), 0)
    jj = lax.broadcasted_iota(jnp.int32, (c, c), 1)
    tril = ii >= jj
    dist = (ii - jj).astype(F32)
    step = lax.broadcasted_iota(jnp.int32, (c, 1), 0).astype(F32)
    s_ref[...] = s0_ref[...]

    def body(i, carry):
        r0 = pl.multiple_of(i * c, c)
        cos_t = cos_ref[pl.ds(r0, c), :]
        sin_t = sin_ref[pl.ds(r0, c), :]
        for j in range(hp):
            sl = slice(j * HEAD_DIM, (j + 1) * HEAD_DIM)
            lg = lg_ref[j, 0:1, 0:1]
            decay = jnp.where(tril, jnp.exp(jnp.where(tril, dist * lg, 0.0)), 0.0)
            q = _rope_head(q_ref[pl.ds(r0, c), sl], cos_t, sin_t)
            k = _rope_head(k_ref[pl.ds(r0, c), sl], cos_t, sin_t) * HEAD_DIM ** -0.5
            vh = v_ref[pl.ds(r0, c), sl].astype(BF16)
            s = s_ref[j]
            intra = _dot_nt(q.astype(BF16), k.astype(BF16)) * decay
            o = (jnp.dot((q * jnp.exp((step + 1.0) * lg)).astype(BF16), s.astype(BF16), preferred_element_type=F32)
                 + jnp.dot(intra.astype(BF16), vh, preferred_element_type=F32))
            s_ref[j] = s * jnp.exp(c * lg) + _dot_tn((k * jnp.exp((c - 1.0 - step) * lg)).astype(BF16), vh)
            gt = gate_ref[pl.ds(r0, c), sl]
            o_ref[pl.ds(r0, c), sl] = (_rms_rows(o) * (gt * jax.nn.sigmoid(gt))).astype(o_ref.dtype)
        return carry

    lax.fori_loop(0, t // c, body, 0)


def _retention(p, lay, tables, s0, b, t, hp, chunk):
    m = p.shape[0]
    h_c = s0.shape[1]
    assert h_c % hp == 0 and t % chunk == 0
    wb = hp * HEAD_DIM
    pos = lay['pos']
    log_gamma = jnp.log1p(-jnp.exp2(RET_GAMMA_BASE - jnp.arange(h_c, dtype=F32)))
    lg = jnp.broadcast_to(log_gamma[:, None, None], (h_c, 8, LANE))

    def col(nm):
        assert pos[nm] % wb == 0
        return pl.BlockSpec((t, wb), lambda bi, hi: (bi, pos[nm] // wb + hi))

    tab = pl.BlockSpec((t, LANE), lambda bi, hi: (0, 0))
    sspec = pl.BlockSpec((None, hp, HEAD_DIM, HEAD_DIM), lambda bi, hi: (bi, hi, 0, 0))
    blocks = (4 * _nbytes((t, wb), F32) + 2 * _nbytes((t, LANE), F32) + _nbytes((t, wb), BF16)
              + 2 * _nbytes((hp, HEAD_DIM, HEAD_DIM), F32))
    return pl.pallas_call(
        functools.partial(_retention_kernel, chunk=chunk),
        grid=(b, h_c // hp),
        in_specs=[col('qc'), col('kc'), col('vc'), col('gc'), tab, tab,
                  pl.BlockSpec((hp, 8, LANE), lambda bi, hi: (hi, 0, 0)), sspec],
        out_specs=[pl.BlockSpec((t, wb), lambda bi, hi: (bi, hi)), sspec],
        out_shape=[jax.ShapeDtypeStruct((m, h_c * HEAD_DIM), BF16), jax.ShapeDtypeStruct(s0.shape, F32)],
        compiler_params=pltpu.CompilerParams(
            dimension_semantics=("arbitrary", "arbitrary"),
            vmem_limit_bytes=_vmem_limit(blocks, 0)),
        name="retention",
    )(p, p, p, p, tables[0], tables[1], lg, s0)


def _in_proj_layout(d_model, kv_b, idx_dim):
    d_mix = d_model
    h_a = (3 * d_mix) // (8 * HEAD_DIM)
    h_b = h_a
    h_c = d_mix // HEAD_DIM - h_a - h_b
    sizes = (3 * h_a * HEAD_DIM, h_a * HEAD_DIM, h_a, h_a,
             h_b * HEAD_DIM, kv_b * HEAD_DIM, kv_b * HEAD_DIM, IDX_HEADS * idx_dim, idx_dim, IDX_HEADS,
             h_c * HEAD_DIM, h_c * HEAD_DIM, h_c * HEAD_DIM, h_c * HEAD_DIM)
    names = ('qkv_a', 'za', 'aa', 'ba', 'qb', 'kb', 'vb', 'iq', 'ik', 'iw', 'qc', 'kc', 'vc', 'gc')
    src, size, off = {}, {}, 0
    for nm, s in zip(names, sizes):
        src[nm] = off
        size[nm] = s
        off += s
    wide = ('iq', 'qc', 'kc', 'vc', 'gc', 'qb', 'za', 'qkv_a', 'kb', 'vb')
    narrow = ('ik', 'iw', 'aa', 'ba')
    pos, cur = {}, 0
    for nm in wide:
        cur = -(-cur // size[nm]) * size[nm]
        pos[nm] = cur
        cur += size[nm]
    small = -(-cur // LANE) * LANE
    cur = small
    for nm in narrow:
        pos[nm] = cur
        cur += size[nm]
    assert cur <= small + LANE
    lay = dict(src=src, size=size, pos=pos, small=small, order=wide + narrow, n_src=off)
    return (h_a, h_b, h_c), lay


def _rearrange_w_in(w_in, lay, bn):
    cols, cur = [], 0
    for nm in lay['order']:
        gap = lay['pos'][nm] - cur
        if gap:
            cols.append(jnp.zeros(w_in.shape[:-1] + (gap,), w_in.dtype))
        cols.append(w_in[..., lay['src'][nm]:lay['src'][nm] + lay['size'][nm]])
        cur = lay['pos'][nm] + lay['size'][nm]
    n_pad = -(-(lay['small'] + LANE) // bn) * bn
    cols.append(jnp.zeros(w_in.shape[:-1] + (n_pad - cur,), w_in.dtype))
    return jnp.concatenate(cols, axis=-1).astype(BF16)


def _proj_part(p, lay, nm, b, t):
    return p[:, lay['pos'][nm]:lay['pos'][nm] + lay['size'][nm]].reshape(b, t, lay['size'][nm])


def _prompt_mixer(p, lay, heads, kv_b, idx_dim, b, t, conv_w, a_log, dt_bias, gdn_g):
    h_a, h_b, h_c = heads
    pos_ids = jnp.arange(t, dtype=jnp.int32)
    tab_head = _rope_tables(pos_ids, HEAD_DIM, LANE)
    tab_idx = _rope_tables(pos_ids, idx_dim, LANE)

    q, kf, kh, vh, iqlo, iqhi, kif, ki2 = _dsa_prep(p, lay, t, tab_head + tab_idx)
    ob = _dsa_prompt(q, kh, vh, iqlo, iqhi, ki2, p, lay, b, t, kv_b, idx_dim)

    n_conv = lay['size']['qkv_a']
    qn, kn, vv = _gdn_prep(p, lay, t, jnp.zeros((b, 8, n_conv), F32), conv_w)
    c = min(CHUNK, t)
    aa = _proj_part(p, lay, 'aa', b, t)
    ba = _proj_part(p, lay, 'ba', b, t)
    g = -jnp.exp(a_log) * jax.nn.softplus(aa + dt_bias)
    oa, s_gdn = _gdn(qn, kn, vv, _gdn_gate_rows(g, b, t, h_a, c), _gdn_gate_rows(jax.nn.sigmoid(ba), b, t, h_a, c),
                     p, lay, gdn_g, jnp.zeros((b, h_a, HEAD_DIM, HEAD_DIM), F32), b, t)

    oc, s_ret = _retention(p, lay, tab_head, jnp.zeros((b, h_c, HEAD_DIM, HEAD_DIM), F32), b, t,
                           hp=2, chunk=min(2 * CHUNK, t))

    o = jnp.concatenate([oa, ob, oc], axis=-1)
    vb = _proj_part(p, lay, 'vb', b, t).reshape(b, t, kv_b, HEAD_DIM)
    conv_new = _proj_part(p, lay, 'qkv_a', b, t)[:, t - (CONV_W - 1):]
    state = (kf.reshape(b, t, kv_b, HEAD_DIM), vb, kif.reshape(b, t, idx_dim), s_gdn, conv_new, s_ret)
    return o, state


def _forward(x, mod, weights, past, cfg):
    (norm_pre, norm_post, w_in_r, gdn_conv_w, gdn_a_log, gdn_dt_bias, gdn_norm,
     w_out, ffn_w_gate, ffn_w_up, ffn_w_down) = weights
    heads, lay, kv_b, idx_dim = cfg
    b, t, d = x.shape
    depth = w_out.shape[0]
    m = b * t
    bm = min(1024, m)
    bm_down = min(256, m)
    h = x.reshape(m, d)
    vec = lambda l, s, j: mod[l, :, s, j].reshape(b, 1, d)
    gpre = lambda l, s: norm_pre[l, s].reshape(1, d)
    gpost = lambda l, s: norm_post[l, s].reshape(1, d)
    states = []
    u = _norm_mod(h, gpre(0, 0), vec(0, 0, 1), vec(0, 0, 0), t)
    for l in range(depth):
        hf = _ffn_in(u, ffn_w_gate, ffn_w_up, (l, 0), bm=bm, bn=MXU_COLS)
        y = _matmul(hf, ffn_w_down, (l, 0), bm=bm_down, bn=MXU_COLS, out_dtype=F32, name="ffn_down")
        h, u = _resid(h, y, vec(l, 0, 2), gpost(l, 0), (gpre(l, 1), vec(l, 1, 1), vec(l, 1, 0)), t, FFN_RES)

        p = _matmul(u, w_in_r, (l,), bm=bm, bn=2 * MXU_COLS, out_dtype=F32, name="in_proj")
        if past is None:
            o, st = _prompt_mixer(p, lay, heads, kv_b, idx_dim, b, t, gdn_conv_w[l], gdn_a_log[l], gdn_dt_bias[l],
                                  gdn_norm[l])
        else:
            cache_k, cache_v, cache_kidx, state_gdn, state_gdn_conv, state_ret, page_table = past
            past_l = (state_gdn_conv[l], state_gdn[l], state_ret[l], cache_k[l], cache_v[l], cache_kidx[l], page_table)
            names = ('qkv_a', 'za', 'qb', 'kb', 'vb', 'iq', 'qc', 'kc', 'vc', 'gc', 'aa', 'ba', 'ik', 'iw')
            parts = [_proj_part(p, lay, nm, b, t) for nm in names]
            o, st = _token_mixer(parts, heads + (kv_b, idx_dim), gdn_conv_w[l], gdn_a_log[l], gdn_dt_bias[l],
                                 gdn_norm[l], past_l)
            o = o.reshape(m, d).astype(BF16)
        states.append(st)
        y = _matmul(o, w_out, (l,), bm=bm, bn=MXU_COLS, out_dtype=F32, name="out_proj")
        h, u = _resid(h, y, vec(l, 1, 2), gpost(l, 1), (gpre(l, 2), vec(l, 2, 1), vec(l, 2, 0)), t, 1.0)

        hf = _ffn_in(u, ffn_w_gate, ffn_w_up, (l, 1), bm=bm, bn=MXU_COLS)
        y = _matmul(hf, ffn_w_down, (l, 1), bm=bm_down, bn=MXU_COLS, out_dtype=F32, name="ffn_down")
        nxt = (gpre(l + 1, 0), vec(l + 1, 0, 1), vec(l + 1, 0, 0)) if l + 1 < depth else None
        h, u = _resid(h, y, vec(l, 2, 2), gpost(l, 2), nxt, t, FFN_RES)
    new_state = [jnp.stack([s[i] for s in states]) for i in range(6)]
    return h.reshape(b, t, d), new_state


def kernel(x_prompt, x_sample, cache_k, cache_v, cache_kidx, state_gdn, state_gdn_conv, state_ret, page_table,
           c_prompt, c_sample, w_ada, b_ada, norm_pre, norm_post, w_in, gdn_conv_w, gdn_a_log, gdn_dt_bias,
           gdn_norm, w_out, ffn_w_gate, ffn_w_up, ffn_w_down):
    d = x_prompt.shape[-1]
    depth = w_ada.shape[0]
    kv_b = cache_k.shape[3]
    idx_dim = cache_kidx.shape[-1]
    heads, lay = _in_proj_layout(d, kv_b, idx_dim)
    assert lay['n_src'] == w_in.shape[-1]
    bp, bs = c_prompt.shape[0], c_sample.shape[0]

    rows = -(-(bp + bs) // 8) * 8
    c_rows = jnp.concatenate([c_prompt, c_sample, jnp.zeros((rows - bp - bs, d), F32)], axis=0)
    mod = _adaln(c_rows, w_ada, b_ada).reshape(depth, rows, N_SUB, 3, d)

    w_in_r = _rearrange_w_in(w_in, lay, 2 * MXU_COLS)
    weights = (norm_pre, norm_post, w_in_r, gdn_conv_w, gdn_a_log, gdn_dt_bias, gdn_norm,
               w_out, ffn_w_gate, ffn_w_up, ffn_w_down)
    cfg = (heads, lay, kv_b, idx_dim)
    y_p, st_p = _forward(x_prompt, mod[:, :bp], weights, None, cfg)
    past = (cache_k, cache_v, cache_kidx, state_gdn, state_gdn_conv, state_ret, page_table)
    y_s, st_s = _forward(x_sample, mod[:, bp:bp + bs], weights, past, cfg)
    return (y_p, y_s, *st_p, *st_s)
```

```python
import functools

import jax
import jax.numpy as jnp
from jax import lax
from jax.experimental import pallas as pl
from jax.experimental.pallas import tpu as pltpu

F32 = jnp.float32
BF16 = jnp.bfloat16

HEAD_DIM = 128
IDX_HEADS = 32
TOPK_MAX = 256
CONV_W = 4
FFN_RES = 0.5
N_SUB = 3
CHUNK = 64
Q_BLOCK = 128
ROPE_THETA = 10000.0
RET_GAMMA_BASE = -5.0
EPS = 1e-6
L2_EPS = 1e-6

V7X_VMEM_BYTES = 64 * 1024 * 1024
V7X_VMEM_BUDGET = V7X_VMEM_BYTES - 8 * 1024 * 1024
LANE = 128
MXU_COLS = 256


def _vmem_limit(block_bytes, scratch_bytes):
    need = 2 * block_bytes + scratch_bytes + 8 * 1024 * 1024
    assert need <= V7X_VMEM_BUDGET, need
    return need


def _nbytes(shape, dtype):
    n = 1
    for s in shape:
        n *= s
    return n * jnp.dtype(dtype).itemsize


def _adaln_kernel(c_ref, w_ref, b_ref, o_ref):
    c = c_ref[...]
    cs = (c * jax.nn.sigmoid(c)).astype(BF16)
    w = w_ref[...].astype(BF16)
    o_ref[...] = jnp.dot(cs, w, preferred_element_type=F32) + b_ref[...]


def _adaln(c_rows, w_ada, b_ada):
    depth, d, n = w_ada.shape
    rows = c_rows.shape[0]
    bn = 1024
    assert n % bn == 0
    blocks = _nbytes((rows, d), F32) + _nbytes((d, bn), F32) + _nbytes((1, bn), F32) + _nbytes((rows, bn), F32)
    return pl.pallas_call(
        _adaln_kernel,
        grid=(depth, n // bn),
        in_specs=[
            pl.BlockSpec((rows, d), lambda l, j: (0, 0)),
            pl.BlockSpec((None, d, bn), lambda l, j: (l, 0, j)),
            pl.BlockSpec((None, 1, bn), lambda l, j: (l, 0, j)),
        ],
        out_specs=pl.BlockSpec((None, rows, bn), lambda l, j: (l, 0, j)),
        out_shape=jax.ShapeDtypeStruct((depth, rows, n), F32),
        compiler_params=pltpu.CompilerParams(
            dimension_semantics=("arbitrary", "arbitrary"),
            vmem_limit_bytes=_vmem_limit(blocks, _nbytes((d, bn), BF16))),
        name="adaln",
    )(c_rows, w_ada, b_ada.reshape(depth, 1, n))


def _mm_kernel(x_ref, w_ref, o_ref, *scratch, cast_w):
    if cast_w:
        (wb_ref,) = scratch

        @pl.when(pl.program_id(1) == 0)
        def _():
            wb_ref[...] = w_ref[...].astype(BF16)

        w = wb_ref[...]
    else:
        w = w_ref[...]
    o_ref[...] = jnp.dot(x_ref[...], w, preferred_element_type=F32).astype(o_ref.dtype)


def _matmul(x, w, w_index, *, bm, bn, out_dtype, name, single_buffer_w=False):
    m, k = x.shape
    n = w.shape[-1]
    assert w.shape[-2] == k and m % bm == 0 and n % bn == 0, (x.shape, w.shape, bm, bn)
    cast_w = w.dtype != BF16
    lead = len(w_index)
    scratch = [pltpu.VMEM((k, bn), BF16)] if cast_w else []
    w_bytes = _nbytes((k, bn), w.dtype)
    blocks = _nbytes((bm, k), x.dtype) + _nbytes((bm, bn), out_dtype) + (0 if single_buffer_w else w_bytes)
    fixed = (_nbytes((k, bn), BF16) if cast_w else 0) + (w_bytes if single_buffer_w else 0)
    w_mode = dict(pipeline_mode=pl.Buffered(1)) if single_buffer_w else {}
    return pl.pallas_call(
        functools.partial(_mm_kernel, cast_w=cast_w),
        grid=(n // bn, m // bm),
        in_specs=[
            pl.BlockSpec((bm, k), lambda j, i: (i, 0)),
            pl.BlockSpec((None,) * lead + (k, bn), lambda j, i: tuple(w_index) + (0, j), **w_mode),
        ],
        out_specs=pl.BlockSpec((bm, bn), lambda j, i: (i, j)),
        out_shape=jax.ShapeDtypeStruct((m, n), out_dtype),
        scratch_shapes=scratch,
        compiler_params=pltpu.CompilerParams(
            dimension_semantics=("arbitrary", "arbitrary"),
            vmem_limit_bytes=_vmem_limit(blocks, fixed)),
        name=name,
    )(x, w)


def _ffn_in_kernel(x_ref, wg_ref, wu_ref, o_ref, wgb_ref, wub_ref):
    @pl.when(pl.program_id(1) == 0)
    def _():
        wgb_ref[...] = wg_ref[...].astype(BF16)
        wub_ref[...] = wu_ref[...].astype(BF16)

    x = x_ref[...]
    g = jnp.dot(x, wgb_ref[...], preferred_element_type=F32)
    u = jnp.dot(x, wub_ref[...], preferred_element_type=F32)
    o_ref[...] = (g * jax.nn.sigmoid(g) * u).astype(o_ref.dtype)


def _ffn_in(x, wg, wu, w_index, *, bm, bn):
    m, k = x.shape
    n = wg.shape[-1]
    assert m % bm == 0 and n % bn == 0
    lead = len(w_index)
    wspec = pl.BlockSpec((None,) * lead + (k, bn), lambda j, i: tuple(w_index) + (0, j))
    blocks = _nbytes((bm, k), BF16) + 2 * _nbytes((k, bn), F32) + _nbytes((bm, bn), BF16)
    return pl.pallas_call(
        _ffn_in_kernel,
        grid=(n // bn, m // bm),
        in_specs=[pl.BlockSpec((bm, k), lambda j, i: (i, 0)), wspec, wspec],
        out_specs=pl.BlockSpec((bm, bn), lambda j, i: (i, j)),
        out_shape=jax.ShapeDtypeStruct((m, n), BF16),
        scratch_shapes=[pltpu.VMEM((k, bn), BF16), pltpu.VMEM((k, bn), BF16)],
        compiler_params=pltpu.CompilerParams(
            dimension_semantics=("arbitrary", "arbitrary"),
            vmem_limit_bytes=_vmem_limit(blocks, 2 * _nbytes((k, bn), BF16))),
        name="ffn_in",
    )(x, wg, wu)


def _rms_rows(x):
    return x * lax.rsqrt(jnp.mean(x * x, axis=-1, keepdims=True) + EPS)


def _norm_mod_kernel(h_ref, gpre_ref, scale_ref, shift_ref, u_ref):
    u = _rms_rows(h_ref[...]) * gpre_ref[...] * (1.0 + scale_ref[...]) + shift_ref[...]
    u_ref[...] = u.astype(u_ref.dtype)


def _resid_kernel(h_ref, y_ref, gate_ref, gpost_ref, *rest, res_w, with_next):
    yn = _rms_rows(y_ref[...]) * gpost_ref[...]
    h = h_ref[...] + res_w * gate_ref[...] * yn
    if with_next:
        gpre_ref, scale_ref, shift_ref, hn_ref, u_ref = rest
        u = _rms_rows(h) * gpre_ref[...] * (1.0 + scale_ref[...]) + shift_ref[...]
        u_ref[...] = u.astype(u_ref.dtype)
    else:
        (hn_ref,) = rest
    hn_ref[...] = h


def _row_block(rows_per_seq):
    return min(256, rows_per_seq)


def _seq_vec_spec(d, bm, rows_per_seq):
    return pl.BlockSpec((None, 1, d), lambda i: (i * bm // rows_per_seq, 0, 0))


def _norm_mod(h, gpre, scale, shift, rows_per_seq):
    m, d = h.shape
    bm = _row_block(rows_per_seq)
    row = pl.BlockSpec((bm, d), lambda i: (i, 0))
    vec = pl.BlockSpec((1, d), lambda i: (0, 0))
    sv = _seq_vec_spec(d, bm, rows_per_seq)
    return pl.pallas_call(
        _norm_mod_kernel,
        grid=(m // bm,),
        in_specs=[row, vec, sv, sv],
        out_specs=row,
        out_shape=jax.ShapeDtypeStruct((m, d), BF16),
        compiler_params=pltpu.CompilerParams(dimension_semantics=("arbitrary",)),
        name="norm_mod",
    )(h, gpre, scale, shift)


def _resid(h, y, gate, gpost, nxt, rows_per_seq, res_w):
    m, d = h.shape
    bm = _row_block(rows_per_seq)
    row = pl.BlockSpec((bm, d), lambda i: (i, 0))
    vec = pl.BlockSpec((1, d), lambda i: (0, 0))
    sv = _seq_vec_spec(d, bm, rows_per_seq)
    with_next = nxt is not None
    in_specs = [row, row, sv, vec] + ([vec, sv, sv] if with_next else [])
    out_specs = [row, row] if with_next else [row]
    out_shape = [jax.ShapeDtypeStruct((m, d), F32)] + ([jax.ShapeDtypeStruct((m, d), BF16)] if with_next else [])
    outs = pl.pallas_call(
        functools.partial(_resid_kernel, res_w=res_w, with_next=with_next),
        grid=(m // bm,),
        in_specs=in_specs,
        out_specs=out_specs,
        out_shape=out_shape,
        compiler_params=pltpu.CompilerParams(dimension_semantics=("arbitrary",)),
        name="resid",
    )(h, y, gate, gpost, *(nxt if with_next else ()))
    return (outs[0], outs[1]) if with_next else (outs[0], None)


def _l2_norm(x):
    return x * lax.rsqrt(jnp.sum(x * x, axis=-1, keepdims=True) + L2_EPS)


def _rope(x, pos):
    half = x.shape[-1] // 2
    inv = ROPE_THETA ** (-jnp.arange(half, dtype=F32) / half)
    ang = pos.astype(F32)[:, None] * inv[None, :]
    cos = jnp.cos(ang)[None, :, None, :]
    sin = jnp.sin(ang)[None, :, None, :]
    x1, x2 = x[..., :half], x[..., half:]
    return jnp.concatenate([x1 * cos - x2 * sin, x2 * cos + x1 * sin], axis=-1)


def _causal_conv(x, buf, w):
    t = x.shape[1]
    xp = jnp.concatenate([buf, x], axis=1)
    y = xp[:, 0:t] * w[0]
    for j in range(1, CONV_W):
        y = y + xp[:, j:j + t] * w[j]
    return jax.nn.silu(y), xp[:, t:]


def _to_chunks(x, c, pad):
    x = jnp.pad(x, [(0, 0), (0, pad)] + [(0, 0)] * (x.ndim - 2))
    b, tp = x.shape[:2]
    x = x.reshape((b, tp // c, c) + x.shape[2:])
    return jnp.transpose(x, (1, 0, 3, 2) + tuple(range(4, x.ndim)))


def _from_chunks(o, t):
    n, b, h, c, d = o.shape
    return jnp.transpose(o, (1, 0, 3, 2, 4)).reshape(b, n * c, h, d)[:, :t]


def _chunk_decay(gc):
    c = gc.shape[-1]
    tri = jnp.tril(jnp.ones((c, c), bool))
    diff = gc[..., :, None] - gc[..., None, :]
    return jnp.where(tri, jnp.exp(jnp.where(tri, diff, 0.0)), 0.0)


def _gated_delta_rule(q, k, v, g, beta, s0):
    t, dk = q.shape[1], q.shape[-1]
    dv = v.shape[-1]
    c = min(CHUNK, t)
    pad = (-t) % c
    q = _to_chunks(q * dk ** -0.5, c, pad)
    k = _to_chunks(k, c, pad)
    v = _to_chunks(v, c, pad)
    beta = _to_chunks(beta, c, pad)
    gc = jnp.cumsum(_to_chunks(g, c, pad), axis=-1)
    decay = _chunk_decay(gc)
    kb = k * beta[..., None]
    strict = jnp.tril(jnp.ones((c, c), bool), -1)
    lower = jnp.where(strict, jnp.einsum('nbhik,nbhjk->nbhij', kb, k) * decay, 0.0) + jnp.eye(c, dtype=F32)
    rhs = jnp.concatenate([v * beta[..., None], kb * jnp.exp(gc)[..., None]], axis=-1)
    sol = lax.linalg.triangular_solve(lower, rhs, left_side=True, lower=True, unit_diagonal=True)
    u, w = sol[..., :dv], sol[..., dv:]

    def step(S, xs):
        qc, kc, uc, wc, gcc, dc = xs
        v_new = uc - jnp.einsum('bhik,bhkv->bhiv', wc, S)
        intra = jnp.einsum('bhik,bhjk->bhij', qc, kc) * dc
        o = (jnp.einsum('bhik,bhkv->bhiv', qc * jnp.exp(gcc)[..., None], S)
             + jnp.einsum('bhij,bhjv->bhiv', intra, v_new))
        gl = gcc[..., -1:]
        S = S * jnp.exp(gl)[..., None] + jnp.einsum('bhjk,bhjv->bhkv', kc * jnp.exp(gl - gcc)[..., None], v_new)
        return S, o

    S, o = lax.scan(step, s0, (q, k, u, w, gc, decay))
    return _from_chunks(o, t), S


def _scan_retention(q, k, v, logd, s0):
    t = q.shape[1]
    c = min(CHUNK, t)
    pad = (-t) % c
    q = _to_chunks(q, c, pad)
    k = _to_chunks(k, c, pad)
    v = _to_chunks(v, c, pad)
    gc = jnp.cumsum(_to_chunks(logd, c, pad), axis=-1)
    decay = _chunk_decay(gc)

    def step(S, xs):
        qc, kc, vc, gcc, dc = xs
        intra = jnp.einsum('bhik,bhjk->bhij', qc, kc) * dc
        o = (jnp.einsum('bhik,bhkv->bhiv', qc * jnp.exp(gcc)[..., None], S)
             + jnp.einsum('bhij,bhjv->bhiv', intra, vc))
        gl = gcc[..., -1:]
        S = S * jnp.exp(gl)[..., None] + jnp.einsum('bhjk,bhjv->bhkv', kc * jnp.exp(gl - gcc)[..., None], vc)
        return S, o

    S, o = lax.scan(step, s0, (q, k, v, gc, decay))
    return _from_chunks(o, t), S


def _gather_rows(rows, idx):
    return jax.vmap(lambda r, i: r[i])(rows, idx)


def _paged_gather(pool, page_table, new_rows, idx):
    page = pool.shape[1]
    n_past = page_table.shape[1] * page
    t = new_rows.shape[1]
    pidx = jnp.minimum(idx, n_past - 1)
    phys = jax.vmap(lambda pt, i: pt[i // page])(page_table, pidx)
    past_rows = pool[phys, pidx % page]
    new_r = _gather_rows(new_rows, jnp.clip(idx - n_past, 0, t - 1))
    is_past = (idx < n_past).reshape(idx.shape + (1,) * (past_rows.ndim - idx.ndim))
    return jnp.where(is_past, past_rows, new_r)


def _sparse_attention(q, iq, iw, kidx_all, qpos, gather_kv, k_sel):
    b, t, h_b, _ = q.shape
    qb_sz = min(Q_BLOCK, t)
    pad = (-t) % qb_sz
    nb = (t + pad) // qb_sz

    def blocks(a):
        a = jnp.pad(a, [(0, 0), (0, pad)] + [(0, 0)] * (a.ndim - 2))
        return jnp.moveaxis(a.reshape((b, nb, qb_sz) + a.shape[2:]), 1, 0)

    qpos_b = jnp.pad(qpos, (0, pad), mode='edge').reshape(nb, qb_sz)
    kpos = jnp.arange(kidx_all.shape[1], dtype=jnp.int32)

    def one_block(args):
        qb, iqb, iwb, pb = args
        rel = jax.nn.relu(jnp.einsum('bqhd,bsd->bqhs', iqb, kidx_all))
        score = jnp.einsum('bqhs,bqh->bqs', rel, iwb)
        visible = kpos[None, :] <= pb[:, None]
        score = jnp.where(visible[None], score, -jnp.inf)
        _, idx = lax.top_k(score, k_sel)
        valid = idx <= pb[None, :, None]
        kr, vr = gather_kv(idx)
        kv_b = kr.shape[3]
        qg = qb.reshape(b, qb_sz, kv_b, h_b // kv_b, HEAD_DIM)
        logits = jnp.einsum('bqngd,bqsnd->bqngs', qg, kr) * HEAD_DIM ** -0.5
        logits = jnp.where(valid[:, :, None, None, :], logits, -jnp.inf)
        p = jax.nn.softmax(logits, axis=-1)
        o = jnp.einsum('bqngs,bqsnd->bqngd', p, vr)
        return o.reshape(b, qb_sz, h_b, HEAD_DIM)

    o = lax.map(one_block, (blocks(q), blocks(iq), blocks(iw), qpos_b))
    return jnp.moveaxis(o, 0, 1).reshape(b, nb * qb_sz, h_b, HEAD_DIM)[:, :t]


def _token_mixer(parts, dims, conv_w, a_log, dt_bias, gdn_g, past):
    (qkv_a, za, qb, kb, vb, iq, qc, kc, vc, gc, aa, ba, ik, iw) = parts
    h_a, h_b, h_c, kv_b, idx_dim = dims
    b, t = qkv_a.shape[:2]
    if past is None:
        conv_buf = jnp.zeros((b, CONV_W - 1, qkv_a.shape[-1]), F32)
        s_gdn = jnp.zeros((b, h_a, HEAD_DIM, HEAD_DIM), F32)
        s_ret = jnp.zeros((b, h_c, HEAD_DIM, HEAD_DIM), F32)
        n_past = 0
    else:
        conv_buf, s_gdn, s_ret, pool_k, pool_v, pool_kidx, page_table = past
        n_past = page_table.shape[1] * pool_k.shape[1]
    pos = n_past + jnp.arange(t, dtype=jnp.int32)

    conv_out, conv_new = _causal_conv(qkv_a, conv_buf, conv_w)
    qa, ka, va = [x.reshape(b, t, h_a, HEAD_DIM) for x in jnp.split(conv_out, 3, axis=-1)]
    g = -jnp.exp(a_log) * jax.nn.softplus(aa + dt_bias)
    oa, s_gdn_new = _gated_delta_rule(_l2_norm(qa), _l2_norm(ka), va, g, jax.nn.sigmoid(ba), s_gdn)
    oa = _rms_rows(oa) * gdn_g * jax.nn.silu(za.reshape(b, t, h_a, HEAD_DIM))

    qb = _rope(qb.reshape(b, t, h_b, HEAD_DIM), pos)
    kb = _rope(kb.reshape(b, t, kv_b, HEAD_DIM), pos)
    vb = vb.reshape(b, t, kv_b, HEAD_DIM)
    iq = _rope(iq.reshape(b, t, IDX_HEADS, idx_dim), pos)
    ik = _rope(ik.reshape(b, t, 1, idx_dim), pos)[:, :, 0]
    iw = iw * (IDX_HEADS ** -0.5 * idx_dim ** -0.5)
    if past is None:
        kidx_all = ik
        gather_kv = lambda idx: (_gather_rows(kb, idx), _gather_rows(vb, idx))
    else:
        past_kidx = pool_kidx[page_table].reshape(b, n_past, idx_dim)
        kidx_all = jnp.concatenate([past_kidx, ik], axis=1)
        gather_kv = lambda idx: (_paged_gather(pool_k, page_table, kb, idx),
                                 _paged_gather(pool_v, page_table, vb, idx))
    k_sel = max(1, min(TOPK_MAX, kidx_all.shape[1] // 4))
    ob = _sparse_attention(qb, iq, iw, kidx_all, pos, gather_kv, k_sel)

    qc = _rope(qc.reshape(b, t, h_c, HEAD_DIM), pos)
    kc = _rope(kc.reshape(b, t, h_c, HEAD_DIM), pos) * HEAD_DIM ** -0.5
    vc = vc.reshape(b, t, h_c, HEAD_DIM)
    log_gamma = jnp.log1p(-jnp.exp2(RET_GAMMA_BASE - jnp.arange(h_c, dtype=F32)))
    logd = jnp.broadcast_to(log_gamma, (b, t, h_c))
    oc, s_ret_new = _scan_retention(qc, kc, vc, logd, s_ret)
    oc = _rms_rows(oc) * jax.nn.silu(gc.reshape(b, t, h_c, HEAD_DIM))

    o = jnp.concatenate([oa.reshape(b, t, -1), ob.reshape(b, t, -1), oc.reshape(b, t, -1)], axis=-1)
    return o, (kb, vb, ik, s_gdn_new, conv_new, s_ret_new)


def _rope_tables(pos, head_dim, lanes):
    half = head_dim // 2
    inv = ROPE_THETA ** (-jnp.arange(half, dtype=F32) / half)
    ang = pos.astype(F32)[:, None] * inv[None, :]
    cos, sin = jnp.cos(ang), jnp.sin(ang)
    reps = lanes // head_dim
    return (jnp.tile(jnp.concatenate([cos, cos], axis=-1), (1, reps)),
            jnp.tile(jnp.concatenate([-sin, sin], axis=-1), (1, reps)))


def _rope_head(x, cos_t, sin_t):
    return x * cos_t + pltpu.roll(x, HEAD_DIM // 2, axis=1) * sin_t


def _rope_half_heads(x, cos_t, sin_t):
    lane = lax.broadcasted_iota(jnp.int32, x.shape, 1)
    fwd = pltpu.roll(x, 32, axis=1)
    bwd = pltpu.roll(x, LANE - 32, axis=1)
    return x * cos_t + jnp.where(lane % 64 < 32, bwd, fwd) * sin_t


def _dsa_prep_kernel(qb_ref, kb_ref, vb_ref, iq_ref, sm_ref, cos_ref, sin_ref, cosi_ref, sini_ref,
                     q_ref, kf_ref, kh_ref, vh_ref, iqlo_ref, iqhi_ref, kif_ref, ki2_ref):
    cos_t, sin_t = cos_ref[...], sin_ref[...]
    cosi_t, sini_t = cosi_ref[...], sini_ref[...]
    for h in range(qb_ref.shape[1] // HEAD_DIM):
        sl = slice(h * HEAD_DIM, (h + 1) * HEAD_DIM)
        q_ref[:, sl] = _rope_head(qb_ref[:, sl], cos_t, sin_t).astype(BF16)
    for h in range(kb_ref.shape[1] // HEAD_DIM):
        sl = slice(h * HEAD_DIM, (h + 1) * HEAD_DIM)
        kr = _rope_head(kb_ref[:, sl], cos_t, sin_t)
        kf_ref[:, sl] = kr
        kh_ref[:, sl] = kr.astype(BF16)
    vh_ref[...] = vb_ref[...].astype(BF16)
    lane = lax.broadcasted_iota(jnp.int32, cos_t.shape, 1)
    low = lane < 64
    for c in range(iq_ref.shape[1] // LANE):
        sl = slice(c * LANE, (c + 1) * LANE)
        r = _rope_half_heads(iq_ref[:, sl], cosi_t, sini_t)
        iqlo_ref[:, sl] = jnp.where(low, r, 0.0).astype(BF16)
        iqhi_ref[:, sl] = jnp.where(low, 0.0, r).astype(BF16)
    r = _rope_half_heads(sm_ref[...], cosi_t, sini_t)
    kif_ref[...] = r[:, :64]
    ki2_ref[...] = jnp.where(low, r, pltpu.roll(r, 64, axis=1)).astype(BF16)


def _col_spec(bt, width, off):
    assert off % width == 0, (off, width)
    return pl.BlockSpec((bt, width), lambda i: (i, off // width))


def _dsa_prep(p, lay, t, tables):
    m = p.shape[0]
    bt = min(256, t)
    nt = t // bt
    pos, wid = lay['pos'], lay['size']
    tab = pl.BlockSpec((bt, LANE), lambda i: (i % nt, 0))
    row = lambda w: pl.BlockSpec((bt, w), lambda i: (i, 0))
    wq, wk, wi = wid['qb'], wid['kb'], wid['iq']
    outs = pl.pallas_call(
        _dsa_prep_kernel,
        grid=(m // bt,),
        in_specs=[_col_spec(bt, wq, pos['qb']), _col_spec(bt, wk, pos['kb']), _col_spec(bt, wk, pos['vb']),
                  _col_spec(bt, wi, pos['iq']), _col_spec(bt, LANE, lay['small']), tab, tab, tab, tab],
        out_specs=[row(wq), row(wk), row(wk), row(wk), row(wi), row(wi), row(64), row(LANE)],
        out_shape=[jax.ShapeDtypeStruct((m, wq), BF16), jax.ShapeDtypeStruct((m, wk), F32),
                   jax.ShapeDtypeStruct((m, wk), BF16), jax.ShapeDtypeStruct((m, wk), BF16),
                   jax.ShapeDtypeStruct((m, wi), BF16), jax.ShapeDtypeStruct((m, wi), BF16),
                   jax.ShapeDtypeStruct((m, 64), F32), jax.ShapeDtypeStruct((m, LANE), BF16)],
        compiler_params=pltpu.CompilerParams(dimension_semantics=("arbitrary",)),
        name="dsa_prep",
    )(p, p, p, p, p, *tables)
    return outs


INT32_MIN = -2 ** 31


def _sortable_key(x):
    b = lax.bitcast_convert_type(x, jnp.int32)
    return b ^ ((b >> 31) & jnp.int32(0x7FFFFFFF))


def _kth_largest_key(keys_ref, k_sel):
    kf = jnp.float32(k_sel)

    def count_ge(t):
        return jnp.sum(jnp.where(keys_ref[...] >= t, 1.0, 0.0), axis=1, keepdims=True)

    rows = keys_ref.shape[0]
    zero = jnp.zeros((rows, 1), jnp.int32)
    t0 = jnp.where(count_ge(zero) >= kf, zero, jnp.int32(INT32_MIN))

    def body(i, t):
        cand = t | (jnp.int32(1) << (30 - i))
        return jnp.where(count_ge(cand) >= kf, cand, t)

    return lax.fori_loop(0, 31, body, t0)


def _dsa_kernel(q_ref, k_ref, v_ref, iqlo_ref, iqhi_ref, ki2_ref, sm_ref, o_ref, score_ref, keys_ref,
                *, k_sel, kv_b, idx_dim, iw_lane, q_block0):
    qb = q_ref.shape[0]
    n_keys = k_ref.shape[0]
    group = q_ref.shape[1] // HEAD_DIM // kv_b
    q0 = (q_block0 + pl.program_id(1)) * qb
    ki2 = ki2_ref[...]
    w_scale = IDX_HEADS ** -0.5 * idx_dim ** -0.5
    heads_per_dot = 8
    score_ref[...] = jnp.zeros_like(score_ref)
    for gi in range(IDX_HEADS // heads_per_dot):
        rows = []
        for c in range(heads_per_dot // 2):
            sl = slice((gi * heads_per_dot // 2 + c) * LANE, (gi * heads_per_dot // 2 + c + 1) * LANE)
            rows.append(iqlo_ref[:, sl])
            rows.append(iqhi_ref[:, sl])
        lhs = jnp.concatenate(rows, axis=0)
        d = lax.dot_general(lhs, ki2, (((1,), (1,)), ((), ())), preferred_element_type=F32)
        acc = score_ref[...]
        for hh in range(heads_per_dot):
            lane = iw_lane + gi * heads_per_dot + hh
            w = sm_ref[:, lane:lane + 1] * w_scale
            acc = acc + jnp.maximum(d[hh * qb:(hh + 1) * qb], 0.0) * w
        score_ref[...] = acc

    qpos = q0 + lax.broadcasted_iota(jnp.int32, (qb, n_keys), 0)
    kpos = lax.broadcasted_iota(jnp.int32, (qb, n_keys), 1)
    visible = kpos <= qpos
    keys_ref[...] = jnp.where(visible, _sortable_key(score_ref[...]), jnp.int32(INT32_MIN))
    thr = _kth_largest_key(keys_ref, k_sel)
    neg = jnp.float32(-jnp.inf)
    score_ref[...] = jnp.where(visible, jnp.where(keys_ref[...] >= thr, 0.0, neg), neg)

    scale = HEAD_DIM ** -0.5
    for n in range(kv_b):
        q3 = jnp.concatenate(
            [q_ref[:, (n * group + g) * HEAD_DIM:(n * group + g + 1) * HEAD_DIM] for g in range(group)], axis=0)
        kn = k_ref[:, n * HEAD_DIM:(n + 1) * HEAD_DIM]
        lg = lax.dot_general(q3, kn, (((1,), (1,)), ((), ())), preferred_element_type=F32) * scale
        lg = lg.reshape(group, qb, n_keys) + score_ref[...][None]
        mx = jnp.max(lg, axis=-1, keepdims=True)
        pr = jnp.exp(lg - mx)
        den = jnp.sum(pr, axis=-1, keepdims=True)
        o = jnp.dot(pr.reshape(group * qb, n_keys).astype(BF16), v_ref[:, n * HEAD_DIM:(n + 1) * HEAD_DIM],
                    preferred_element_type=F32)
        o = o / den.reshape(group * qb, 1)
        for g in range(group):
            sl = slice((n * group + g) * HEAD_DIM, (n * group + g + 1) * HEAD_DIM)
            o_ref[:, sl] = o[g * qb:(g + 1) * qb].astype(o_ref.dtype)


def _dsa_prompt(q, kh, vh, iqlo, iqhi, ki2, p, lay, b, t, kv_b, idx_dim):
    m, wq = q.shape
    wk, wi = kh.shape[1], iqlo.shape[1]
    qb = min(Q_BLOCK, t)
    nq = t // qb
    k_sel = max(1, min(TOPK_MAX, t // 4))
    n_bands = 4 if nq % 4 == 0 else 1
    per_band = nq // n_bands
    kh3, vh3, ki3 = (x.reshape(b, t, x.shape[1]) for x in (kh, vh, ki2))
    outs = []
    for band in range(n_bands):
        first = band * per_band
        n_keys = (first + per_band) * qb
        qrow = lambda w, first=first: pl.BlockSpec((qb, w), lambda bi, qi: (bi * nq + first + qi, 0))
        seq = lambda w, n_keys=n_keys: pl.BlockSpec((None, n_keys, w), lambda bi, qi: (bi, 0, 0))
        small = pl.BlockSpec((qb, LANE), lambda bi, qi, first=first: (bi * nq + first + qi, lay['small'] // LANE))
        blocks = (_nbytes((qb, wq), BF16) * 2 + 2 * _nbytes((n_keys, wk), BF16) + 2 * _nbytes((qb, wi), BF16)
                  + _nbytes((n_keys, LANE), BF16) + _nbytes((qb, LANE), F32))
        temps = 2 * _nbytes((qb, n_keys), F32) + 2 * _nbytes((8 * qb, n_keys), F32)
        outs.append(pl.pallas_call(
            functools.partial(_dsa_kernel, k_sel=k_sel, kv_b=kv_b, idx_dim=idx_dim,
                              iw_lane=lay['pos']['iw'] - lay['small'], q_block0=first),
            grid=(b, per_band),
            in_specs=[qrow(wq), seq(wk), seq(wk), qrow(wi), qrow(wi), seq(LANE), small],
            out_specs=pl.BlockSpec((None, qb, wq), lambda bi, qi: (bi, qi, 0)),
            out_shape=jax.ShapeDtypeStruct((b, per_band * qb, wq), BF16),
            scratch_shapes=[pltpu.VMEM((qb, n_keys), F32), pltpu.VMEM((qb, n_keys), jnp.int32)],
            compiler_params=pltpu.CompilerParams(
                dimension_semantics=("arbitrary", "arbitrary"),
                vmem_limit_bytes=_vmem_limit(blocks, temps)),
            name="dsa",
        )(q, kh3, vh3, iqlo, iqhi, ki3, p))
    return jnp.concatenate(outs, axis=1).reshape(m, wq)


def _gdn_prep_kernel(x_ref, prev_ref, buf_ref, w_ref, q_ref, k_ref, v_ref, xp_ref, *, blocks_per_seq):
    bt = x_ref.shape[0]
    first = pl.program_id(0) % blocks_per_seq == 0
    xp_ref[0:8, :] = jnp.where(first, buf_ref[...], prev_ref[...])
    xp_ref[8:, :] = x_ref[...]
    n_heads = q_ref.shape[1] // HEAD_DIM
    for part, out_ref in enumerate((q_ref, k_ref, v_ref)):
        for h in range(n_heads):
            src = slice((part * n_heads + h) * HEAD_DIM, (part * n_heads + h + 1) * HEAD_DIM)
            y = xp_ref[8 - (CONV_W - 1):8 - (CONV_W - 1) + bt, src] * w_ref[0:1, src]
            for j in range(1, CONV_W):
                y = y + xp_ref[8 - (CONV_W - 1) + j:8 - (CONV_W - 1) + j + bt, src] * w_ref[j:j + 1, src]
            y = y * jax.nn.sigmoid(y)
            if part < 2:
                y = y * lax.rsqrt(jnp.sum(y * y, axis=-1, keepdims=True) + L2_EPS)
            if part == 0:
                y = y * HEAD_DIM ** -0.5
            out_ref[:, h * HEAD_DIM:(h + 1) * HEAD_DIM] = y


def _gdn_prep(p, lay, t, buf8, conv_w):
    m = p.shape[0]
    bt = min(256, t)
    nt = t // bt
    c = lay['size']['qkv_a']
    off = lay['pos']['qkv_a']
    assert off % c == 0
    cb = off // c
    wq = c // 3
    row = pl.BlockSpec((bt, wq), lambda i: (i, 0))
    blocks = _nbytes((bt, c), F32) + 2 * _nbytes((8, c), F32) + _nbytes((CONV_W, c), F32) + 3 * _nbytes((bt, wq), F32)
    return pl.pallas_call(
        functools.partial(_gdn_prep_kernel, blocks_per_seq=nt),
        grid=(m // bt,),
        in_specs=[pl.BlockSpec((bt, c), lambda i: (i, cb)),
                  pl.BlockSpec((8, c), lambda i: (jnp.maximum(i * (bt // 8) - 1, 0), cb)),
                  pl.BlockSpec((None, 8, c), lambda i: (i // nt, 0, 0)),
                  pl.BlockSpec((CONV_W, c), lambda i: (0, 0))],
        out_specs=[row, row, row],
        out_shape=[jax.ShapeDtypeStruct((m, wq), F32)] * 3,
        scratch_shapes=[pltpu.VMEM((bt + 8, c), F32)],
        compiler_params=pltpu.CompilerParams(
            dimension_semantics=("arbitrary",),
            vmem_limit_bytes=_vmem_limit(blocks, _nbytes((bt + 8, c), F32))),
        name="gdn_prep",
    )(p, p, buf8, conv_w)


def _dot_hi(a, b):
    return jnp.dot(a, b, precision=lax.Precision.HIGHEST, preferred_element_type=F32)


def _dot_nt(a, b):
    return lax.dot_general(a, b, (((1,), (1,)), ((), ())), preferred_element_type=F32)


def _dot_tn(a, b):
    return lax.dot_general(a, b, (((0,), (0,)), ((), ())), preferred_element_type=F32)


def _gdn_kernel(q_ref, k_ref, v_ref, g_ref, beta_ref, z_ref, gn_ref, s0_ref, o_ref, s_ref, *, chunk):
    t = q_ref.shape[0]
    hp = q_ref.shape[1] // HEAD_DIM
    c = chunk
    ii = lax.broadcasted_iota(jnp.int32, (c, c), 0)
    jj = lax.broadcasted_iota(jnp.int32, (c, c), 1)
    tril, strict, eye = ii >= jj, ii > jj, ii == jj

    @pl.when(pl.program_id(1) == 0)
    def _():
        s_ref[...] = s0_ref[...]

    gn = gn_ref[...]

    heads = range(hp)
    cols = [slice(j * HEAD_DIM, (j + 1) * HEAD_DIM) for j in heads]

    def body(i, carry):
        r0 = pl.multiple_of(i * c, c)
        g_row = [g_ref[j, pl.ds(i, 1), :] for j in heads]
        gc_col = [jnp.sum(jnp.where(tril, g_row[j], 0.0), axis=1, keepdims=True) for j in heads]
        gc_row = [jnp.sum(jnp.where(eye, gc_col[j], 0.0), axis=0, keepdims=True) for j in heads]
        beta_col = [jnp.sum(jnp.where(eye, beta_ref[j, pl.ds(i, 1), :], 0.0), axis=1, keepdims=True) for j in heads]
        decay = [jnp.where(tril, jnp.exp(jnp.where(tril, gc_col[j] - gc_row[j], 0.0)), 0.0) for j in heads]
        k = [k_ref[pl.ds(r0, c), cols[j]] for j in heads]
        kb = [k[j] * beta_col[j] for j in heads]
        kh = [k[j].astype(BF16) for j in heads]
        a = [jnp.where(strict, _dot_nt(kb[j].astype(BF16), kh[j]) * decay[j], 0.0) for j in heads]
        inv = [jnp.where(eye, 1.0, 0.0) - a[j] for j in heads]
        pw = a
        span = 2
        while span < c:
            pw = [_dot_hi(pw[j], pw[j]) for j in heads]
            inv = [inv[j] + _dot_hi(inv[j], pw[j]) for j in heads]
            span *= 2
        egc = [jnp.exp(gc_col[j]) for j in heads]
        u = [_dot_hi(inv[j], v_ref[pl.ds(r0, c), cols[j]] * beta_col[j]) for j in heads]
        w = [_dot_hi(inv[j], kb[j] * egc[j]) for j in heads]
        s = [s_ref[j] for j in heads]
        sh = [s[j].astype(BF16) for j in heads]
        vnh = [(u[j] - jnp.dot(w[j].astype(BF16), sh[j], preferred_element_type=F32)).astype(BF16) for j in heads]
        gl = [jnp.sum(g_row[j], axis=1, keepdims=True) for j in heads]
        for j in heads:
            s_ref[j] = s[j] * jnp.exp(gl[j]) + _dot_tn((k[j] * jnp.exp(gl[j] - gc_col[j])).astype(BF16), vnh[j])
        q = [q_ref[pl.ds(r0, c), cols[j]] for j in heads]
        intra = [_dot_nt(q[j].astype(BF16), kh[j]) * decay[j] for j in heads]
        o = [jnp.dot((q[j] * egc[j]).astype(BF16), sh[j], preferred_element_type=F32)
             + jnp.dot(intra[j].astype(BF16), vnh[j], preferred_element_type=F32) for j in heads]
        for j in heads:
            z = z_ref[pl.ds(r0, c), cols[j]]
            o_ref[pl.ds(r0, c), cols[j]] = (_rms_rows(o[j]) * gn * (z * jax.nn.sigmoid(z))).astype(o_ref.dtype)
        return carry

    lax.fori_loop(0, t // c, body, 0)


def _gdn_time_block(t):
    return min(4 * CHUNK, t)


def _gdn_gate_rows(x, b, t, h_a, c):
    tb = _gdn_time_block(t)
    x = jnp.transpose(x, (0, 2, 1)).reshape(b, h_a, t // tb, tb // c, c)
    return jnp.transpose(x, (0, 2, 1, 3, 4))


def _gdn(qn, kn, vv, g, beta, p, lay, gdn_g, s0, b, t):
    m, w = qn.shape
    h_a = w // HEAD_DIM
    c = g.shape[-1]
    tb = _gdn_time_block(t)
    nt = t // tb
    zoff = lay['pos']['za']
    assert zoff % w == 0
    seq = pl.BlockSpec((tb, w), lambda bi, ti: (bi * nt + ti, 0))
    gspec = pl.BlockSpec((None, None, h_a, tb // c, c), lambda bi, ti: (bi, ti, 0, 0, 0))
    sspec = pl.BlockSpec((None, h_a, HEAD_DIM, HEAD_DIM), lambda bi, ti: (bi, 0, 0, 0))
    blocks = 4 * _nbytes((tb, w), F32) + _nbytes((tb, w), BF16) + 2 * _nbytes((h_a, HEAD_DIM, HEAD_DIM), F32)
    return pl.pallas_call(
        functools.partial(_gdn_kernel, chunk=c),
        grid=(b, nt),
        in_specs=[seq, seq, seq, gspec, gspec,
                  pl.BlockSpec((tb, w), lambda bi, ti: (bi * nt + ti, zoff // w)),
                  pl.BlockSpec((1, HEAD_DIM), lambda bi, ti: (0, 0)), sspec],
        out_specs=[seq, sspec],
        out_shape=[jax.ShapeDtypeStruct((m, w), BF16), jax.ShapeDtypeStruct(s0.shape, F32)],
        compiler_params=pltpu.CompilerParams(
            dimension_semantics=("arbitrary", "arbitrary"),
            vmem_limit_bytes=_vmem_limit(blocks, 0)),
        name="gdn",
    )(qn, kn, vv, g, beta, p, gdn_g.reshape(1, HEAD_DIM), s0)


def _retention_kernel(q_ref, k_ref, v_ref, gate_ref, cos_ref, sin_ref, lg_ref, s0_ref, o_ref, s_ref, *, chunk):
    t = q_ref.shape[0]
    hp = q_ref.shape[1] // HEAD_DIM
    c = chunk
    ii = lax.broadcasted_iota(jnp.int32, (c, c), 0)
    jj = lax.broadcasted_iota(jnp.int32, (c, c), 1)
    tril = ii >= jj
    dist = (ii - jj).astype(F32)
    step = lax.broadcasted_iota(jnp.int32, (c, 1), 0).astype(F32)
    s_ref[...] = s0_ref[...]

    def body(i, carry):
        r0 = pl.multiple_of(i * c, c)
        cos_t = cos_ref[pl.ds(r0, c), :]
        sin_t = sin_ref[pl.ds(r0, c), :]
        for j in range(hp):
            sl = slice(j * HEAD_DIM, (j + 1) * HEAD_DIM)
            lg = lg_ref[j, 0:1, 0:1]
            decay = jnp.where(tril, jnp.exp(jnp.where(tril, dist * lg, 0.0)), 0.0)
            q = _rope_head(q_ref[pl.ds(r0, c), sl], cos_t, sin_t)
            k = _rope_head(k_ref[pl.ds(r0, c), sl], cos_t, sin_t) * HEAD_DIM ** -0.5
            vh = v_ref[pl.ds(r0, c), sl].astype(BF16)
            s = s_ref[j]
            intra = _dot_nt(q.astype(BF16), k.astype(BF16)) * decay
            o = (jnp.dot((q * jnp.exp((step + 1.0) * lg)).astype(BF16), s.astype(BF16), preferred_element_type=F32)
                 + jnp.dot(intra.astype(BF16), vh, preferred_element_type=F32))
            s_ref[j] = s * jnp.exp(c * lg) + _dot_tn((k * jnp.exp((c - 1.0 - step) * lg)).astype(BF16), vh)
            gt = gate_ref[pl.ds(r0, c), sl]
            o_ref[pl.ds(r0, c), sl] = (_rms_rows(o) * (gt * jax.nn.sigmoid(gt))).astype(o_ref.dtype)
        return carry

    lax.fori_loop(0, t // c, body, 0)


def _retention(p, lay, tables, s0, b, t, hp, chunk):
    m = p.shape[0]
    h_c = s0.shape[1]
    assert h_c % hp == 0 and t % chunk == 0
    wb = hp * HEAD_DIM
    pos = lay['pos']
    log_gamma = jnp.log1p(-jnp.exp2(RET_GAMMA_BASE - jnp.arange(h_c, dtype=F32)))
    lg = jnp.broadcast_to(log_gamma[:, None, None], (h_c, 8, LANE))

    def col(nm):
        assert pos[nm] % wb == 0
        return pl.BlockSpec((t, wb), lambda bi, hi: (bi, pos[nm] // wb + hi))

    tab = pl.BlockSpec((t, LANE), lambda bi, hi: (0, 0))
    sspec = pl.BlockSpec((None, hp, HEAD_DIM, HEAD_DIM), lambda bi, hi: (bi, hi, 0, 0))
    blocks = (4 * _nbytes((t, wb), F32) + 2 * _nbytes((t, LANE), F32) + _nbytes((t, wb), BF16)
              + 2 * _nbytes((hp, HEAD_DIM, HEAD_DIM), F32))
    return pl.pallas_call(
        functools.partial(_retention_kernel, chunk=chunk),
        grid=(b, h_c // hp),
        in_specs=[col('qc'), col('kc'), col('vc'), col('gc'), tab, tab,
                  pl.BlockSpec((hp, 8, LANE), lambda bi, hi: (hi, 0, 0)), sspec],
        out_specs=[pl.BlockSpec((t, wb), lambda bi, hi: (bi, hi)), sspec],
        out_shape=[jax.ShapeDtypeStruct((m, h_c * HEAD_DIM), BF16), jax.ShapeDtypeStruct(s0.shape, F32)],
        compiler_params=pltpu.CompilerParams(
            dimension_semantics=("arbitrary", "arbitrary"),
            vmem_limit_bytes=_vmem_limit(blocks, 0)),
        name="retention",
    )(p, p, p, p, tables[0], tables[1], lg, s0)


PAGES_PER_STEP = 4
MASKED = -1e30


def _stack_idx_heads(iqlo_ref, iqhi_ref):
    rows = []
    for c in range(iqlo_ref.shape[1] // LANE):
        sl = slice(c * LANE, (c + 1) * LANE)
        rows.append(iqlo_ref[:, sl])
        rows.append(iqhi_ref[:, sl])
    return jnp.concatenate(rows, axis=0).astype(BF16)


def _idx_scores(lhs, ki2, sm_ref, rows, iw_lane, w_scale):
    d = _dot_nt(lhs, ki2.astype(BF16))
    acc = jnp.zeros((rows, ki2.shape[0]), F32)
    for h in range(IDX_HEADS):
        w = sm_ref[:, iw_lane + h:iw_lane + h + 1] * w_scale
        acc = acc + jnp.maximum(d[h * rows:(h + 1) * rows], 0.0) * w
    return acc


def _dsa_s_scores_kernel(pt_ref, iqlo_ref, iqhi_ref, sm_ref, *rest, idx_dim, iw_lane):
    page_refs, score_ref = rest[:-1], rest[-1]
    rows = iqlo_ref.shape[0]
    page = page_refs[0].shape[0]
    lhs = _stack_idx_heads(iqlo_ref, iqhi_ref)
    w_scale = IDX_HEADS ** -0.5 * idx_dim ** -0.5
    for i, pr in enumerate(page_refs):
        score_ref[:, i * page:(i + 1) * page] = _idx_scores(lhs, pr[...], sm_ref, rows, iw_lane, w_scale)


def _dsa_s_scores(page_table, iqlo, iqhi, sm, pool_ki2, layer, t, idx_dim, iw_lane, pages_per_step):
    b, n_pages = page_table.shape
    page = pool_ki2.shape[2]
    wi = iqlo.shape[1]
    assert n_pages % pages_per_step == 0
    row = lambda w: pl.BlockSpec((t, w), lambda bi, gi, pt: (bi, 0))
    pages = [pl.BlockSpec((None, None, page, LANE),
                          lambda bi, gi, pt, i=i: (layer, pt[bi, gi * pages_per_step + i], 0, 0))
             for i in range(pages_per_step)]
    return pl.pallas_call(
        functools.partial(_dsa_s_scores_kernel, idx_dim=idx_dim, iw_lane=iw_lane),
        grid_spec=pltpu.PrefetchScalarGridSpec(
            num_scalar_prefetch=1,
            grid=(b, n_pages // pages_per_step),
            in_specs=[row(wi), row(wi), row(LANE)] + pages,
            out_specs=pl.BlockSpec((None, t, pages_per_step * page), lambda bi, gi, pt: (bi, 0, gi)),
        ),
        out_shape=jax.ShapeDtypeStruct((b, t, n_pages * page), F32),
        compiler_params=pltpu.CompilerParams(dimension_semantics=("arbitrary", "arbitrary")),
        name="dsa_s_scores",
    )(page_table, iqlo, iqhi, sm, *([pool_ki2] * pages_per_step))


def _dsa_s_attn_kernel(pt_ref, sp_ref, sn_ref, q_ref, kn_ref, vn_ref, *rest, k_sel, kv_b):
    n_pg = (len(rest) - 7) // 2
    k_refs, v_refs = rest[:n_pg], rest[n_pg:2 * n_pg]
    o_ref, keysp_ref, keysn_ref, thr_ref, m_ref, l_ref, acc_ref = rest[2 * n_pg:]
    t = q_ref.shape[0]
    page = k_refs[0].shape[0]
    n_heads = q_ref.shape[1] // HEAD_DIM
    group = n_heads // kv_b
    gi = pl.program_id(1)
    scale = HEAD_DIM ** -0.5
    qi = lax.broadcasted_iota(jnp.int32, (t, page), 0)
    kj = lax.broadcasted_iota(jnp.int32, (t, page), 1)
    new_visible = kj <= qi

    @pl.when(gi == 0)
    def _():
        keysp_ref[...] = _sortable_key(sp_ref[...])
        keysn_ref[...] = jnp.where(new_visible, _sortable_key(sn_ref[...]), jnp.int32(INT32_MIN))
        m_ref[...] = jnp.full_like(m_ref, MASKED)
        l_ref[...] = jnp.zeros_like(l_ref)
        acc_ref[...] = jnp.zeros_like(acc_ref)

        kf = jnp.float32(k_sel)

        def count_ge(thr):
            return (jnp.sum(jnp.where(keysp_ref[...] >= thr, 1.0, 0.0), axis=1, keepdims=True)
                    + jnp.sum(jnp.where(keysn_ref[...] >= thr, 1.0, 0.0), axis=1, keepdims=True))

        zero = jnp.zeros((t, 1), jnp.int32)
        thr0 = jnp.where(count_ge(zero) >= kf, zero, jnp.int32(INT32_MIN))

        def search(i, thr):
            cand = thr | (jnp.int32(1) << (30 - i))
            return jnp.where(count_ge(cand) >= kf, cand, thr)

        thr_ref[...] = jnp.broadcast_to(lax.fori_loop(0, 31, search, thr0), thr_ref.shape)

    thr = thr_ref[:, 0:1]

    q = q_ref[...]
    q3 = [jnp.concatenate([q[:, (n * group + g) * HEAD_DIM:(n * group + g + 1) * HEAD_DIM] for g in range(group)],
                          axis=0).astype(BF16) for n in range(kv_b)]

    def update(kmat, vmat, bias):
        n_keys = kmat.shape[0]
        kh, vh = kmat.astype(BF16), vmat.astype(BF16)
        for n in range(kv_b):
            rows = slice(n * group * t, (n + 1) * group * t)
            cols = slice(n * HEAD_DIM, (n + 1) * HEAD_DIM)
            lg = _dot_nt(q3[n], kh[:, cols]) * scale
            lg = (lg.reshape(group, t, n_keys) + bias[None]).reshape(group * t, n_keys)
            m_old = m_ref[rows, 0:1]
            m_new = jnp.maximum(m_old, jnp.max(lg, axis=-1, keepdims=True))
            alpha = jnp.exp(m_old - m_new)
            pr = jnp.exp(lg - m_new)
            l_ref[rows, :] = alpha * l_ref[rows, :] + jnp.sum(pr, axis=-1, keepdims=True)
            acc_ref[rows, :] = alpha * acc_ref[rows, :] + jnp.dot(pr.astype(BF16), vh[:, cols],
                                                                  preferred_element_type=F32)
            m_ref[rows, :] = jnp.broadcast_to(m_new, (group * t, LANE))

    biases = []
    for i in range(n_pg):
        start = pl.multiple_of((gi * n_pg + i) * page, page)
        biases.append(jnp.where(keysp_ref[:, pl.ds(start, page)] >= thr, 0.0, MASKED))
    update(jnp.concatenate([r[...] for r in k_refs], axis=0), jnp.concatenate([r[...] for r in v_refs], axis=0),
           jnp.concatenate(biases, axis=1))

    @pl.when(gi == pl.num_programs(1) - 1)
    def _():
        bias_new = jnp.where(new_visible, jnp.where(keysn_ref[...] >= thr, 0.0, MASKED), MASKED)
        update(kn_ref[...], vn_ref[...], bias_new)
        for h in range(n_heads):
            rows = slice(h * t, (h + 1) * t)
            o_ref[:, h * HEAD_DIM:(h + 1) * HEAD_DIM] = acc_ref[rows, :] / l_ref[rows, 0:1]


def _dsa_s_attn(page_table, scores_past, scores_new, q, k_new, v_new, pool_k, pool_v, layer, t, kv_b, k_sel,
                pages_per_step):
    b, n_pages = page_table.shape
    page = pool_k.shape[2]
    wq = q.shape[1]
    wk = k_new.shape[2]
    n_past = n_pages * page
    pk = pool_k.reshape(pool_k.shape[:3] + (wk,))
    pv = pool_v.reshape(pool_v.shape[:3] + (wk,))
    pages = [pl.BlockSpec((None, None, page, wk),
                          lambda bi, gi, pt, i=i: (layer, pt[bi, gi * pages_per_step + i], 0, 0))
             for i in range(pages_per_step)]
    per_seq = lambda rows, w: pl.BlockSpec((None, rows, w), lambda bi, gi, pt: (bi, 0, 0))
    n_heads = wq // HEAD_DIM
    return pl.pallas_call(
        functools.partial(_dsa_s_attn_kernel, k_sel=k_sel, kv_b=kv_b),
        grid_spec=pltpu.PrefetchScalarGridSpec(
            num_scalar_prefetch=1,
            grid=(b, n_pages // pages_per_step),
            in_specs=[per_seq(t, n_past), per_seq(t, page), pl.BlockSpec((t, wq), lambda bi, gi, pt: (bi, 0)),
                      per_seq(page, wk), per_seq(page, wk)] + pages + pages,
            out_specs=pl.BlockSpec((t, wq), lambda bi, gi, pt: (bi, 0)),
            scratch_shapes=[pltpu.VMEM((t, n_past), jnp.int32), pltpu.VMEM((t, page), jnp.int32),
                            pltpu.VMEM((t, LANE), jnp.int32),
                            pltpu.VMEM((n_heads * t, LANE), F32), pltpu.VMEM((n_heads * t, LANE), F32),
                            pltpu.VMEM((n_heads * t, HEAD_DIM), F32)],
        ),
        out_shape=jax.ShapeDtypeStruct((b * t, wq), F32),
        compiler_params=pltpu.CompilerParams(dimension_semantics=("arbitrary", "arbitrary")),
        name="dsa_s_attn",
    )(page_table, scores_past, scores_new, q, k_new, v_new, *([pk] * pages_per_step), *([pv] * pages_per_step))


def _in_proj_layout(d_model, kv_b, idx_dim):
    d_mix = d_model
    h_a = (3 * d_mix) // (8 * HEAD_DIM)
    h_b = h_a
    h_c = d_mix // HEAD_DIM - h_a - h_b
    sizes = (3 * h_a * HEAD_DIM, h_a * HEAD_DIM, h_a, h_a,
             h_b * HEAD_DIM, kv_b * HEAD_DIM, kv_b * HEAD_DIM, IDX_HEADS * idx_dim, idx_dim, IDX_HEADS,
             h_c * HEAD_DIM, h_c * HEAD_DIM, h_c * HEAD_DIM, h_c * HEAD_DIM)
    names = ('qkv_a', 'za', 'aa', 'ba', 'qb', 'kb', 'vb', 'iq', 'ik', 'iw', 'qc', 'kc', 'vc', 'gc')
    src, size, off = {}, {}, 0
    for nm, s in zip(names, sizes):
        src[nm] = off
        size[nm] = s
        off += s
    wide = ('iq', 'qc', 'kc', 'vc', 'gc', 'qb', 'za', 'qkv_a', 'kb', 'vb')
    narrow = ('ik', 'iw', 'aa', 'ba')
    pos, cur = {}, 0
    for nm in wide:
        cur = -(-cur // size[nm]) * size[nm]
        pos[nm] = cur
        cur += size[nm]
    small = -(-cur // LANE) * LANE
    cur = small
    for nm in narrow:
        pos[nm] = cur
        cur += size[nm]
    assert cur <= small + LANE
    lay = dict(src=src, size=size, pos=pos, small=small, order=wide + narrow, n_src=off)
    return (h_a, h_b, h_c), lay


def _rearrange_w_in(w_in, lay, bn):
    cols, cur = [], 0
    for nm in lay['order']:
        gap = lay['pos'][nm] - cur
        if gap:
            cols.append(jnp.zeros(w_in.shape[:-1] + (gap,), w_in.dtype))
        cols.append(w_in[..., lay['src'][nm]:lay['src'][nm] + lay['size'][nm]])
        cur = lay['pos'][nm] + lay['size'][nm]
    n_pad = -(-(lay['small'] + LANE) // bn) * bn
    cols.append(jnp.zeros(w_in.shape[:-1] + (n_pad - cur,), w_in.dtype))
    return jnp.concatenate(cols, axis=-1).astype(BF16)


def _proj_part(p, lay, nm, b, t):
    return p[:, lay['pos'][nm]:lay['pos'][nm] + lay['size'][nm]].reshape(b, t, lay['size'][nm])


def _prompt_mixer(p, lay, heads, kv_b, idx_dim, b, t, conv_w, a_log, dt_bias, gdn_g):
    h_a, h_b, h_c = heads
    pos_ids = jnp.arange(t, dtype=jnp.int32)
    tab_head = _rope_tables(pos_ids, HEAD_DIM, LANE)
    tab_idx = _rope_tables(pos_ids, idx_dim, LANE)

    q, kf, kh, vh, iqlo, iqhi, kif, ki2 = _dsa_prep(p, lay, t, tab_head + tab_idx)
    ob = _dsa_prompt(q, kh, vh, iqlo, iqhi, ki2, p, lay, b, t, kv_b, idx_dim)

    n_conv = lay['size']['qkv_a']
    qn, kn, vv = _gdn_prep(p, lay, t, jnp.zeros((b, 8, n_conv), F32), conv_w)
    c = min(CHUNK, t)
    aa = _proj_part(p, lay, 'aa', b, t)
    ba = _proj_part(p, lay, 'ba', b, t)
    g = -jnp.exp(a_log) * jax.nn.softplus(aa + dt_bias)
    oa, s_gdn = _gdn(qn, kn, vv, _gdn_gate_rows(g, b, t, h_a, c), _gdn_gate_rows(jax.nn.sigmoid(ba), b, t, h_a, c),
                     p, lay, gdn_g, jnp.zeros((b, h_a, HEAD_DIM, HEAD_DIM), F32), b, t)

    oc, s_ret = _retention(p, lay, tab_head, jnp.zeros((b, h_c, HEAD_DIM, HEAD_DIM), F32), b, t,
                           hp=2, chunk=min(2 * CHUNK, t))

    o = jnp.concatenate([oa, ob, oc], axis=-1)
    vb = _proj_part(p, lay, 'vb', b, t).reshape(b, t, kv_b, HEAD_DIM)
    conv_new = _proj_part(p, lay, 'qkv_a', b, t)[:, t - (CONV_W - 1):]
    state = (kf.reshape(b, t, kv_b, HEAD_DIM), vb, kif.reshape(b, t, idx_dim), s_gdn, conv_new, s_ret)
    return o, state


def _sample_mixer(p, lay, heads, kv_b, idx_dim, b, t, conv_w, a_log, dt_bias, gdn_g, past, layer):
    h_a, h_b, h_c = heads
    cache_k, cache_v, pool_ki2, state_gdn, state_gdn_conv, state_ret, page_table = past
    page = cache_k.shape[2]
    n_past = page_table.shape[1] * page
    m = b * t
    assert t <= CHUNK and t >= CONV_W - 1
    pos_ids = n_past + jnp.arange(t, dtype=jnp.int32)
    per_row = lambda tabs: tuple(jnp.tile(x, (b, 1)) for x in tabs)
    tables = per_row(_rope_tables(pos_ids, HEAD_DIM, LANE)) + per_row(_rope_tables(pos_ids, idx_dim, LANE))
    q, kf, kh, vh, iqlo, iqhi, kif, ki2 = _dsa_prep(p, lay, m, tables)

    f32 = lambda x: x.astype(F32)
    to_page = lambda x: jnp.pad(x.reshape(b, t, -1), ((0, 0), (0, page - t), (0, 0)))
    sm = p[:, lay['small']:lay['small'] + LANE]
    iw_lane = lay['pos']['iw'] - lay['small']
    scores_past = _dsa_s_scores(page_table, f32(iqlo), f32(iqhi), sm, pool_ki2, layer, t, idx_dim, iw_lane,
                                PAGES_PER_STEP)
    own = jnp.arange(b, dtype=jnp.int32).reshape(b, 1)
    scores_new = _dsa_s_scores(own, f32(iqlo), f32(iqhi), sm, to_page(f32(ki2))[None], 0, t, idx_dim, iw_lane, 1)
    k_sel = max(1, min(TOPK_MAX, (n_past + t) // 4))
    vb = _proj_part(p, lay, 'vb', b, t)
    ob = _dsa_s_attn(page_table, scores_past, scores_new, f32(q), to_page(kf), to_page(vb), cache_k, cache_v,
                     layer, t, kv_b, k_sel, PAGES_PER_STEP)

    tp = CHUNK
    lead = tp - t
    padl = lambda x: jnp.pad(x.reshape(b, t, -1), ((0, 0), (lead, 0), (0, 0)))
    rows = lambda x: padl(x).reshape(b * tp, -1)
    p_pad = rows(p)
    buf8 = jnp.pad(state_gdn_conv[layer], ((0, 0), (8 - (CONV_W - 1), 0), (0, 0)))
    qn, kn, vv = _gdn_prep(p, lay, t, buf8, conv_w)
    aa = _proj_part(p, lay, 'aa', b, t)
    ba = _proj_part(p, lay, 'ba', b, t)
    g = -jnp.exp(a_log) * jax.nn.softplus(aa + dt_bias)
    oa, s_gdn = _gdn(rows(qn), rows(kn), rows(vv), _gdn_gate_rows(padl(g), b, tp, h_a, tp),
                     _gdn_gate_rows(padl(jax.nn.sigmoid(ba)), b, tp, h_a, tp), p_pad, lay, gdn_g,
                     state_gdn[layer], b, tp)
    log_gamma = jnp.log1p(-jnp.exp2(RET_GAMMA_BASE - jnp.arange(h_c, dtype=F32)))
    s_ret0 = state_ret[layer] * jnp.exp(-lead * log_gamma)[None, :, None, None]
    tab_pad = _rope_tables(n_past - lead + jnp.arange(tp, dtype=jnp.int32), HEAD_DIM, LANE)
    oc, s_ret = _retention(p_pad, lay, tab_pad, s_ret0, b, tp, hp=2, chunk=tp)
    tail = lambda x: x.reshape(b, tp, -1)[:, lead:].reshape(m, -1)

    o = jnp.concatenate([tail(oa), ob.astype(BF16), tail(oc)], axis=-1)
    conv_new = _proj_part(p, lay, 'qkv_a', b, t)[:, t - (CONV_W - 1):]
    state = (kf.reshape(b, t, kv_b, HEAD_DIM), vb.reshape(b, t, kv_b, HEAD_DIM), kif.reshape(b, t, idx_dim),
             s_gdn, conv_new, s_ret)
    return o, state


def _forward(x, mod, weights, past, cfg):
    (norm_pre, norm_post, w_in_r, gdn_conv_w, gdn_a_log, gdn_dt_bias, gdn_norm,
     w_out, ffn_w_gate, ffn_w_up, ffn_w_down) = weights
    heads, lay, kv_b, idx_dim = cfg
    b, t, d = x.shape
    depth = w_out.shape[0]
    m = b * t
    bm = min(1024, m)
    bm_down = min(256, m)
    h = x.reshape(m, d)
    vec = lambda l, s, j: mod[l, :, s, j].reshape(b, 1, d)
    gpre = lambda l, s: norm_pre[l, s].reshape(1, d)
    gpost = lambda l, s: norm_post[l, s].reshape(1, d)
    states = []
    u = _norm_mod(h, gpre(0, 0), vec(0, 0, 1), vec(0, 0, 0), t)
    for l in range(depth):
        hf = _ffn_in(u, ffn_w_gate, ffn_w_up, (l, 0), bm=bm, bn=MXU_COLS)
        y = _matmul(hf, ffn_w_down, (l, 0), bm=bm_down, bn=2 * MXU_COLS, out_dtype=F32, name="ffn_down",
                    single_buffer_w=True)
        h, u = _resid(h, y, vec(l, 0, 2), gpost(l, 0), (gpre(l, 1), vec(l, 1, 1), vec(l, 1, 0)), t, FFN_RES)

        p = _matmul(u, w_in_r, (l,), bm=bm, bn=2 * MXU_COLS, out_dtype=F32, name="in_proj")
        if past is None:
            o, st = _prompt_mixer(p, lay, heads, kv_b, idx_dim, b, t, gdn_conv_w[l], gdn_a_log[l], gdn_dt_bias[l],
                                  gdn_norm[l])
        else:
            o, st = _sample_mixer(p, lay, heads, kv_b, idx_dim, b, t, gdn_conv_w[l], gdn_a_log[l], gdn_dt_bias[l],
                                  gdn_norm[l], past, l)
        states.append(st)
        y = _matmul(o, w_out, (l,), bm=bm, bn=MXU_COLS, out_dtype=F32, name="out_proj")
        h, u = _resid(h, y, vec(l, 1, 2), gpost(l, 1), (gpre(l, 2), vec(l, 2, 1), vec(l, 2, 0)), t, 1.0)

        hf = _ffn_in(u, ffn_w_gate, ffn_w_up, (l, 1), bm=bm, bn=MXU_COLS)
        y = _matmul(hf, ffn_w_down, (l, 1), bm=bm_down, bn=2 * MXU_COLS, out_dtype=F32, name="ffn_down",
                    single_buffer_w=True)
        nxt = (gpre(l + 1, 0), vec(l + 1, 0, 1), vec(l + 1, 0, 0)) if l + 1 < depth else None
        h, u = _resid(h, y, vec(l, 2, 2), gpost(l, 2), nxt, t, FFN_RES)
    new_state = [jnp.stack([s[i] for s in states]) for i in range(6)]
    return h.reshape(b, t, d), new_state


def kernel(x_prompt, x_sample, cache_k, cache_v, cache_kidx, state_gdn, state_gdn_conv, state_ret, page_table,
           c_prompt, c_sample, w_ada, b_ada, norm_pre, norm_post, w_in, gdn_conv_w, gdn_a_log, gdn_dt_bias,
           gdn_norm, w_out, ffn_w_gate, ffn_w_up, ffn_w_down):
    d = x_prompt.shape[-1]
    depth = w_ada.shape[0]
    kv_b = cache_k.shape[3]
    idx_dim = cache_kidx.shape[-1]
    heads, lay = _in_proj_layout(d, kv_b, idx_dim)
    assert lay['n_src'] == w_in.shape[-1] and 2 * idx_dim == LANE
    bp, bs = c_prompt.shape[0], c_sample.shape[0]

    rows = -(-(bp + bs) // 8) * 8
    c_rows = jnp.concatenate([c_prompt, c_sample, jnp.zeros((rows - bp - bs, d), F32)], axis=0)
    mod = _adaln(c_rows, w_ada, b_ada).reshape(depth, rows, N_SUB, 3, d)

    w_in_r = _rearrange_w_in(w_in, lay, 2 * MXU_COLS)
    weights = (norm_pre, norm_post, w_in_r, gdn_conv_w, gdn_a_log, gdn_dt_bias, gdn_norm,
               w_out, ffn_w_gate, ffn_w_up, ffn_w_down)
    cfg = (heads, lay, kv_b, idx_dim)
    y_p, st_p = _forward(x_prompt, mod[:, :bp], weights, None, cfg)
    pool_ki2 = jnp.concatenate([cache_kidx, cache_kidx], axis=-1)
    past = (cache_k, cache_v, pool_ki2, state_gdn, state_gdn_conv, state_ret, page_table)
    y_s, st_s = _forward(x_sample, mod[:, bp:bp + bs], weights, past, cfg)
    return (y_p, y_s, *st_p, *st_s)
```

```python
import functools

import jax
import jax.numpy as jnp
from jax import lax
from jax.experimental import pallas as pl
from jax.experimental.pallas import tpu as pltpu

F32 = jnp.float32
BF16 = jnp.bfloat16

HEAD_DIM = 128
IDX_HEADS = 32
TOPK_MAX = 256
CONV_W = 4
FFN_RES = 0.5
N_SUB = 3
CHUNK = 64
Q_BLOCK = 128
ROPE_THETA = 10000.0
RET_GAMMA_BASE = -5.0
EPS = 1e-6
L2_EPS = 1e-6

V7X_VMEM_BYTES = 64 * 1024 * 1024
V7X_VMEM_BUDGET = V7X_VMEM_BYTES - 8 * 1024 * 1024
LANE = 128
MXU_COLS = 256


def _vmem_limit(block_bytes, scratch_bytes):
    need = 2 * block_bytes + scratch_bytes + 8 * 1024 * 1024
    assert need <= V7X_VMEM_BUDGET, need
    return need


def _nbytes(shape, dtype):
    n = 1
    for s in shape:
        n *= s
    return n * jnp.dtype(dtype).itemsize


def _adaln_kernel(c_ref, w_ref, b_ref, o_ref):
    c = c_ref[...]
    cs = (c * jax.nn.sigmoid(c)).astype(BF16)
    w = w_ref[...].astype(BF16)
    o_ref[...] = jnp.dot(cs, w, preferred_element_type=F32) + b_ref[...]


def _adaln(c_rows, w_ada, b_ada):
    depth, d, n = w_ada.shape
    rows = c_rows.shape[0]
    bn = 1024
    assert n % bn == 0
    blocks = _nbytes((rows, d), F32) + _nbytes((d, bn), F32) + _nbytes((1, bn), F32) + _nbytes((rows, bn), F32)
    return pl.pallas_call(
        _adaln_kernel,
        grid=(depth, n // bn),
        in_specs=[
            pl.BlockSpec((rows, d), lambda l, j: (0, 0)),
            pl.BlockSpec((None, d, bn), lambda l, j: (l, 0, j)),
            pl.BlockSpec((None, 1, bn), lambda l, j: (l, 0, j)),
        ],
        out_specs=pl.BlockSpec((None, rows, bn), lambda l, j: (l, 0, j)),
        out_shape=jax.ShapeDtypeStruct((depth, rows, n), F32),
        compiler_params=pltpu.CompilerParams(
            dimension_semantics=("arbitrary", "arbitrary"),
            vmem_limit_bytes=_vmem_limit(blocks, _nbytes((d, bn), BF16))),
        name="adaln",
    )(c_rows, w_ada, b_ada.reshape(depth, 1, n))


def _mm_kernel(x_ref, xs_ref, w_ref, o_ref, os_ref, *scratch, cast_w):
    first = pl.program_id(1) == 0
    if cast_w:
        (wb_ref,) = scratch

        @pl.when(first)
        def _():
            wb_ref[...] = w_ref[...].astype(BF16)

        w_src = wb_ref
    else:
        w_src = w_ref

    @pl.when(first)
    def _():
        os_ref[...] = jnp.dot(xs_ref[...], w_src[...], preferred_element_type=F32).astype(os_ref.dtype)

    o_ref[...] = jnp.dot(x_ref[...], w_src[...], preferred_element_type=F32).astype(o_ref.dtype)


def _matmul(x, xs, w, w_index, *, bm, bn, out_dtype, name, single_buffer_w=False):
    m, k = x.shape
    ms = xs.shape[0]
    n = w.shape[-1]
    assert w.shape[-2] == k and m % bm == 0 and n % bn == 0, (x.shape, w.shape, bm, bn)
    cast_w = w.dtype != BF16
    lead = len(w_index)
    scratch = [pltpu.VMEM((k, bn), BF16)] if cast_w else []
    w_bytes = _nbytes((k, bn), w.dtype)
    blocks = (_nbytes((bm, k), x.dtype) + _nbytes((bm, bn), out_dtype) + _nbytes((ms, bn), out_dtype)
              + (0 if single_buffer_w else w_bytes))
    fixed = ((_nbytes((k, bn), BF16) if cast_w else 0) + (w_bytes if single_buffer_w else 0)
             + _nbytes((ms, k), xs.dtype))
    w_mode = dict(pipeline_mode=pl.Buffered(1)) if single_buffer_w else {}
    return pl.pallas_call(
        functools.partial(_mm_kernel, cast_w=cast_w),
        grid=(n // bn, m // bm),
        in_specs=[
            pl.BlockSpec((bm, k), lambda j, i: (i, 0)),
            pl.BlockSpec((ms, k), lambda j, i: (0, 0), pipeline_mode=pl.Buffered(1)),
            pl.BlockSpec((None,) * lead + (k, bn), lambda j, i: tuple(w_index) + (0, j), **w_mode),
        ],
        out_specs=[pl.BlockSpec((bm, bn), lambda j, i: (i, j)), pl.BlockSpec((ms, bn), lambda j, i: (0, j))],
        out_shape=[jax.ShapeDtypeStruct((m, n), out_dtype), jax.ShapeDtypeStruct((ms, n), out_dtype)],
        scratch_shapes=scratch,
        compiler_params=pltpu.CompilerParams(
            dimension_semantics=("arbitrary", "arbitrary"),
            vmem_limit_bytes=_vmem_limit(blocks, fixed)),
        name=name,
    )(x, xs, w)


def _ffn_in_kernel(x_ref, xs_ref, wg_ref, wu_ref, o_ref, os_ref, wgb_ref, wub_ref):
    def swiglu(x):
        g = jnp.dot(x, wgb_ref[...], preferred_element_type=F32)
        u = jnp.dot(x, wub_ref[...], preferred_element_type=F32)
        return g * jax.nn.sigmoid(g) * u

    @pl.when(pl.program_id(1) == 0)
    def _():
        wgb_ref[...] = wg_ref[...].astype(BF16)
        wub_ref[...] = wu_ref[...].astype(BF16)
        os_ref[...] = swiglu(xs_ref[...]).astype(os_ref.dtype)

    o_ref[...] = swiglu(x_ref[...]).astype(o_ref.dtype)


def _ffn_in(x, xs, wg, wu, w_index, *, bm, bn):
    m, k = x.shape
    ms = xs.shape[0]
    n = wg.shape[-1]
    assert m % bm == 0 and n % bn == 0
    lead = len(w_index)
    wspec = pl.BlockSpec((None,) * lead + (k, bn), lambda j, i: tuple(w_index) + (0, j))
    blocks = _nbytes((bm, k), BF16) + 2 * _nbytes((k, bn), F32) + _nbytes((bm, bn), BF16) + _nbytes((ms, bn), BF16)
    return pl.pallas_call(
        _ffn_in_kernel,
        grid=(n // bn, m // bm),
        in_specs=[pl.BlockSpec((bm, k), lambda j, i: (i, 0)),
                  pl.BlockSpec((ms, k), lambda j, i: (0, 0), pipeline_mode=pl.Buffered(1)), wspec, wspec],
        out_specs=[pl.BlockSpec((bm, bn), lambda j, i: (i, j)), pl.BlockSpec((ms, bn), lambda j, i: (0, j))],
        out_shape=[jax.ShapeDtypeStruct((m, n), BF16), jax.ShapeDtypeStruct((ms, n), BF16)],
        scratch_shapes=[pltpu.VMEM((k, bn), BF16), pltpu.VMEM((k, bn), BF16)],
        compiler_params=pltpu.CompilerParams(
            dimension_semantics=("arbitrary", "arbitrary"),
            vmem_limit_bytes=_vmem_limit(blocks, 2 * _nbytes((k, bn), BF16) + _nbytes((ms, k), BF16))),
        name="ffn_in",
    )(x, xs, wg, wu)


def _rms_rows(x):
    return x * lax.rsqrt(jnp.mean(x * x, axis=-1, keepdims=True) + EPS)


def _norm_mod_kernel(h_ref, gpre_ref, scale_ref, shift_ref, u_ref):
    u = _rms_rows(h_ref[...]) * gpre_ref[...] * (1.0 + scale_ref[...]) + shift_ref[...]
    u_ref[...] = u.astype(u_ref.dtype)


def _resid_kernel(h_ref, y_ref, gate_ref, gpost_ref, *rest, res_w, with_next):
    yn = _rms_rows(y_ref[...]) * gpost_ref[...]
    h = h_ref[...] + res_w * gate_ref[...] * yn
    if with_next:
        gpre_ref, scale_ref, shift_ref, hn_ref, u_ref = rest
        u = _rms_rows(h) * gpre_ref[...] * (1.0 + scale_ref[...]) + shift_ref[...]
        u_ref[...] = u.astype(u_ref.dtype)
    else:
        (hn_ref,) = rest
    hn_ref[...] = h


def _row_block(rows_per_seq):
    return min(256, rows_per_seq)


def _seq_vec_spec(d, bm, rows_per_seq):
    return pl.BlockSpec((None, 1, d), lambda i: (i * bm // rows_per_seq, 0, 0))


def _norm_mod(h, gpre, scale, shift, rows_per_seq):
    m, d = h.shape
    bm = _row_block(rows_per_seq)
    row = pl.BlockSpec((bm, d), lambda i: (i, 0))
    vec = pl.BlockSpec((1, d), lambda i: (0, 0))
    sv = _seq_vec_spec(d, bm, rows_per_seq)
    return pl.pallas_call(
        _norm_mod_kernel,
        grid=(m // bm,),
        in_specs=[row, vec, sv, sv],
        out_specs=row,
        out_shape=jax.ShapeDtypeStruct((m, d), BF16),
        compiler_params=pltpu.CompilerParams(dimension_semantics=("arbitrary",)),
        name="norm_mod",
    )(h, gpre, scale, shift)


def _resid(h, y, gate, gpost, nxt, rows_per_seq, res_w):
    m, d = h.shape
    bm = _row_block(rows_per_seq)
    row = pl.BlockSpec((bm, d), lambda i: (i, 0))
    vec = pl.BlockSpec((1, d), lambda i: (0, 0))
    sv = _seq_vec_spec(d, bm, rows_per_seq)
    with_next = nxt is not None
    in_specs = [row, row, sv, vec] + ([vec, sv, sv] if with_next else [])
    out_specs = [row, row] if with_next else [row]
    out_shape = [jax.ShapeDtypeStruct((m, d), F32)] + ([jax.ShapeDtypeStruct((m, d), BF16)] if with_next else [])
    outs = pl.pallas_call(
        functools.partial(_resid_kernel, res_w=res_w, with_next=with_next),
        grid=(m // bm,),
        in_specs=in_specs,
        out_specs=out_specs,
        out_shape=out_shape,
        compiler_params=pltpu.CompilerParams(dimension_semantics=("arbitrary",)),
        name="resid",
    )(h, y, gate, gpost, *(nxt if with_next else ()))
    return (outs[0], outs[1]) if with_next else (outs[0], None)


def _l2_norm(x):
    return x * lax.rsqrt(jnp.sum(x * x, axis=-1, keepdims=True) + L2_EPS)


def _rope(x, pos):
    half = x.shape[-1] // 2
    inv = ROPE_THETA ** (-jnp.arange(half, dtype=F32) / half)
    ang = pos.astype(F32)[:, None] * inv[None, :]
    cos = jnp.cos(ang)[None, :, None, :]
    sin = jnp.sin(ang)[None, :, None, :]
    x1, x2 = x[..., :half], x[..., half:]
    return jnp.concatenate([x1 * cos - x2 * sin, x2 * cos + x1 * sin], axis=-1)


def _causal_conv(x, buf, w):
    t = x.shape[1]
    xp = jnp.concatenate([buf, x], axis=1)
    y = xp[:, 0:t] * w[0]
    for j in range(1, CONV_W):
        y = y + xp[:, j:j + t] * w[j]
    return jax.nn.silu(y), xp[:, t:]


def _to_chunks(x, c, pad):
    x = jnp.pad(x, [(0, 0), (0, pad)] + [(0, 0)] * (x.ndim - 2))
    b, tp = x.shape[:2]
    x = x.reshape((b, tp // c, c) + x.shape[2:])
    return jnp.transpose(x, (1, 0, 3, 2) + tuple(range(4, x.ndim)))


def _from_chunks(o, t):
    n, b, h, c, d = o.shape
    return jnp.transpose(o, (1, 0, 3, 2, 4)).reshape(b, n * c, h, d)[:, :t]


def _chunk_decay(gc):
    c = gc.shape[-1]
    tri = jnp.tril(jnp.ones((c, c), bool))
    diff = gc[..., :, None] - gc[..., None, :]
    return jnp.where(tri, jnp.exp(jnp.where(tri, diff, 0.0)), 0.0)


def _gated_delta_rule(q, k, v, g, beta, s0):
    t, dk = q.shape[1], q.shape[-1]
    dv = v.shape[-1]
    c = min(CHUNK, t)
    pad = (-t) % c
    q = _to_chunks(q * dk ** -0.5, c, pad)
    k = _to_chunks(k, c, pad)
    v = _to_chunks(v, c, pad)
    beta = _to_chunks(beta, c, pad)
    gc = jnp.cumsum(_to_chunks(g, c, pad), axis=-1)
    decay = _chunk_decay(gc)
    kb = k * beta[..., None]
    strict = jnp.tril(jnp.ones((c, c), bool), -1)
    lower = jnp.where(strict, jnp.einsum('nbhik,nbhjk->nbhij', kb, k) * decay, 0.0) + jnp.eye(c, dtype=F32)
    rhs = jnp.concatenate([v * beta[..., None], kb * jnp.exp(gc)[..., None]], axis=-1)
    sol = lax.linalg.triangular_solve(lower, rhs, left_side=True, lower=True, unit_diagonal=True)
    u, w = sol[..., :dv], sol[..., dv:]

    def step(S, xs):
        qc, kc, uc, wc, gcc, dc = xs
        v_new = uc - jnp.einsum('bhik,bhkv->bhiv', wc, S)
        intra = jnp.einsum('bhik,bhjk->bhij', qc, kc) * dc
        o = (jnp.einsum('bhik,bhkv->bhiv', qc * jnp.exp(gcc)[..., None], S)
             + jnp.einsum('bhij,bhjv->bhiv', intra, v_new))
        gl = gcc[..., -1:]
        S = S * jnp.exp(gl)[..., None] + jnp.einsum('bhjk,bhjv->bhkv', kc * jnp.exp(gl - gcc)[..., None], v_new)
        return S, o

    S, o = lax.scan(step, s0, (q, k, u, w, gc, decay))
    return _from_chunks(o, t), S


def _scan_retention(q, k, v, logd, s0):
    t = q.shape[1]
    c = min(CHUNK, t)
    pad = (-t) % c
    q = _to_chunks(q, c, pad)
    k = _to_chunks(k, c, pad)
    v = _to_chunks(v, c, pad)
    gc = jnp.cumsum(_to_chunks(logd, c, pad), axis=-1)
    decay = _chunk_decay(gc)

    def step(S, xs):
        qc, kc, vc, gcc, dc = xs
        intra = jnp.einsum('bhik,bhjk->bhij', qc, kc) * dc
        o = (jnp.einsum('bhik,bhkv->bhiv', qc * jnp.exp(gcc)[..., None], S)
             + jnp.einsum('bhij,bhjv->bhiv', intra, vc))
        gl = gcc[..., -1:]
        S = S * jnp.exp(gl)[..., None] + jnp.einsum('bhjk,bhjv->bhkv', kc * jnp.exp(gl - gcc)[..., None], vc)
        return S, o

    S, o = lax.scan(step, s0, (q, k, v, gc, decay))
    return _from_chunks(o, t), S


def _gather_rows(rows, idx):
    return jax.vmap(lambda r, i: r[i])(rows, idx)


def _paged_gather(pool, page_table, new_rows, idx):
    page = pool.shape[1]
    n_past = page_table.shape[1] * page
    t = new_rows.shape[1]
    pidx = jnp.minimum(idx, n_past - 1)
    phys = jax.vmap(lambda pt, i: pt[i // page])(page_table, pidx)
    past_rows = pool[phys, pidx % page]
    new_r = _gather_rows(new_rows, jnp.clip(idx - n_past, 0, t - 1))
    is_past = (idx < n_past).reshape(idx.shape + (1,) * (past_rows.ndim - idx.ndim))
    return jnp.where(is_past, past_rows, new_r)


def _sparse_attention(q, iq, iw, kidx_all, qpos, gather_kv, k_sel):
    b, t, h_b, _ = q.shape
    qb_sz = min(Q_BLOCK, t)
    pad = (-t) % qb_sz
    nb = (t + pad) // qb_sz

    def blocks(a):
        a = jnp.pad(a, [(0, 0), (0, pad)] + [(0, 0)] * (a.ndim - 2))
        return jnp.moveaxis(a.reshape((b, nb, qb_sz) + a.shape[2:]), 1, 0)

    qpos_b = jnp.pad(qpos, (0, pad), mode='edge').reshape(nb, qb_sz)
    kpos = jnp.arange(kidx_all.shape[1], dtype=jnp.int32)

    def one_block(args):
        qb, iqb, iwb, pb = args
        rel = jax.nn.relu(jnp.einsum('bqhd,bsd->bqhs', iqb, kidx_all))
        score = jnp.einsum('bqhs,bqh->bqs', rel, iwb)
        visible = kpos[None, :] <= pb[:, None]
        score = jnp.where(visible[None], score, -jnp.inf)
        _, idx = lax.top_k(score, k_sel)
        valid = idx <= pb[None, :, None]
        kr, vr = gather_kv(idx)
        kv_b = kr.shape[3]
        qg = qb.reshape(b, qb_sz, kv_b, h_b // kv_b, HEAD_DIM)
        logits = jnp.einsum('bqngd,bqsnd->bqngs', qg, kr) * HEAD_DIM ** -0.5
        logits = jnp.where(valid[:, :, None, None, :], logits, -jnp.inf)
        p = jax.nn.softmax(logits, axis=-1)
        o = jnp.einsum('bqngs,bqsnd->bqngd', p, vr)
        return o.reshape(b, qb_sz, h_b, HEAD_DIM)

    o = lax.map(one_block, (blocks(q), blocks(iq), blocks(iw), qpos_b))
    return jnp.moveaxis(o, 0, 1).reshape(b, nb * qb_sz, h_b, HEAD_DIM)[:, :t]


def _token_mixer(parts, dims, conv_w, a_log, dt_bias, gdn_g, past):
    (qkv_a, za, qb, kb, vb, iq, qc, kc, vc, gc, aa, ba, ik, iw) = parts
    h_a, h_b, h_c, kv_b, idx_dim = dims
    b, t = qkv_a.shape[:2]
    if past is None:
        conv_buf = jnp.zeros((b, CONV_W - 1, qkv_a.shape[-1]), F32)
        s_gdn = jnp.zeros((b, h_a, HEAD_DIM, HEAD_DIM), F32)
        s_ret = jnp.zeros((b, h_c, HEAD_DIM, HEAD_DIM), F32)
        n_past = 0
    else:
        conv_buf, s_gdn, s_ret, pool_k, pool_v, pool_kidx, page_table = past
        n_past = page_table.shape[1] * pool_k.shape[1]
    pos = n_past + jnp.arange(t, dtype=jnp.int32)

    conv_out, conv_new = _causal_conv(qkv_a, conv_buf, conv_w)
    qa, ka, va = [x.reshape(b, t, h_a, HEAD_DIM) for x in jnp.split(conv_out, 3, axis=-1)]
    g = -jnp.exp(a_log) * jax.nn.softplus(aa + dt_bias)
    oa, s_gdn_new = _gated_delta_rule(_l2_norm(qa), _l2_norm(ka), va, g, jax.nn.sigmoid(ba), s_gdn)
    oa = _rms_rows(oa) * gdn_g * jax.nn.silu(za.reshape(b, t, h_a, HEAD_DIM))

    qb = _rope(qb.reshape(b, t, h_b, HEAD_DIM), pos)
    kb = _rope(kb.reshape(b, t, kv_b, HEAD_DIM), pos)
    vb = vb.reshape(b, t, kv_b, HEAD_DIM)
    iq = _rope(iq.reshape(b, t, IDX_HEADS, idx_dim), pos)
    ik = _rope(ik.reshape(b, t, 1, idx_dim), pos)[:, :, 0]
    iw = iw * (IDX_HEADS ** -0.5 * idx_dim ** -0.5)
    if past is None:
        kidx_all = ik
        gather_kv = lambda idx: (_gather_rows(kb, idx), _gather_rows(vb, idx))
    else:
        past_kidx = pool_kidx[page_table].reshape(b, n_past, idx_dim)
        kidx_all = jnp.concatenate([past_kidx, ik], axis=1)
        gather_kv = lambda idx: (_paged_gather(pool_k, page_table, kb, idx),
                                 _paged_gather(pool_v, page_table, vb, idx))
    k_sel = max(1, min(TOPK_MAX, kidx_all.shape[1] // 4))
    ob = _sparse_attention(qb, iq, iw, kidx_all, pos, gather_kv, k_sel)

    qc = _rope(qc.reshape(b, t, h_c, HEAD_DIM), pos)
    kc = _rope(kc.reshape(b, t, h_c, HEAD_DIM), pos) * HEAD_DIM ** -0.5
    vc = vc.reshape(b, t, h_c, HEAD_DIM)
    log_gamma = jnp.log1p(-jnp.exp2(RET_GAMMA_BASE - jnp.arange(h_c, dtype=F32)))
    logd = jnp.broadcast_to(log_gamma, (b, t, h_c))
    oc, s_ret_new = _scan_retention(qc, kc, vc, logd, s_ret)
    oc = _rms_rows(oc) * jax.nn.silu(gc.reshape(b, t, h_c, HEAD_DIM))

    o = jnp.concatenate([oa.reshape(b, t, -1), ob.reshape(b, t, -1), oc.reshape(b, t, -1)], axis=-1)
    return o, (kb, vb, ik, s_gdn_new, conv_new, s_ret_new)


def _rope_tables(pos, head_dim, lanes):
    half = head_dim // 2
    inv = ROPE_THETA ** (-jnp.arange(half, dtype=F32) / half)
    ang = pos.astype(F32)[:, None] * inv[None, :]
    cos, sin = jnp.cos(ang), jnp.sin(ang)
    reps = lanes // head_dim
    return (jnp.tile(jnp.concatenate([cos, cos], axis=-1), (1, reps)),
            jnp.tile(jnp.concatenate([-sin, sin], axis=-1), (1, reps)))


def _rope_head(x, cos_t, sin_t):
    return x * cos_t + pltpu.roll(x, HEAD_DIM // 2, axis=1) * sin_t


def _rope_half_heads(x, cos_t, sin_t):
    lane = lax.broadcasted_iota(jnp.int32, x.shape, 1)
    fwd = pltpu.roll(x, 32, axis=1)
    bwd = pltpu.roll(x, LANE - 32, axis=1)
    return x * cos_t + jnp.where(lane % 64 < 32, bwd, fwd) * sin_t


def _dsa_prep_kernel(qb_ref, kb_ref, vb_ref, iq_ref, sm_ref, cos_ref, sin_ref, cosi_ref, sini_ref,
                     q_ref, kf_ref, kh_ref, vh_ref, iqlo_ref, iqhi_ref, kif_ref, ki2_ref):
    cos_t, sin_t = cos_ref[...], sin_ref[...]
    cosi_t, sini_t = cosi_ref[...], sini_ref[...]
    for h in range(qb_ref.shape[1] // HEAD_DIM):
        sl = slice(h * HEAD_DIM, (h + 1) * HEAD_DIM)
        q_ref[:, sl] = _rope_head(qb_ref[:, sl], cos_t, sin_t).astype(BF16)
    for h in range(kb_ref.shape[1] // HEAD_DIM):
        sl = slice(h * HEAD_DIM, (h + 1) * HEAD_DIM)
        kr = _rope_head(kb_ref[:, sl], cos_t, sin_t)
        kf_ref[:, sl] = kr
        kh_ref[:, sl] = kr.astype(BF16)
    vh_ref[...] = vb_ref[...].astype(BF16)
    lane = lax.broadcasted_iota(jnp.int32, cos_t.shape, 1)
    low = lane < 64
    for c in range(iq_ref.shape[1] // LANE):
        sl = slice(c * LANE, (c + 1) * LANE)
        r = _rope_half_heads(iq_ref[:, sl], cosi_t, sini_t)
        iqlo_ref[:, sl] = jnp.where(low, r, 0.0).astype(BF16)
        iqhi_ref[:, sl] = jnp.where(low, 0.0, r).astype(BF16)
    r = _rope_half_heads(sm_ref[...], cosi_t, sini_t)
    kif_ref[...] = r[:, :64]
    ki2_ref[...] = jnp.where(low, r, pltpu.roll(r, 64, axis=1)).astype(BF16)


def _col_spec(bt, width, off):
    assert off % width == 0, (off, width)
    return pl.BlockSpec((bt, width), lambda i: (i, off // width))


def _dsa_prep(p, lay, t, tables):
    m = p.shape[0]
    bt = min(256, t)
    nt = t // bt
    pos, wid = lay['pos'], lay['size']
    tab = pl.BlockSpec((bt, LANE), lambda i: (i % nt, 0))
    row = lambda w: pl.BlockSpec((bt, w), lambda i: (i, 0))
    wq, wk, wi = wid['qb'], wid['kb'], wid['iq']
    outs = pl.pallas_call(
        _dsa_prep_kernel,
        grid=(m // bt,),
        in_specs=[_col_spec(bt, wq, pos['qb']), _col_spec(bt, wk, pos['kb']), _col_spec(bt, wk, pos['vb']),
                  _col_spec(bt, wi, pos['iq']), _col_spec(bt, LANE, lay['small']), tab, tab, tab, tab],
        out_specs=[row(wq), row(wk), row(wk), row(wk), row(wi), row(wi), row(64), row(LANE)],
        out_shape=[jax.ShapeDtypeStruct((m, wq), BF16), jax.ShapeDtypeStruct((m, wk), F32),
                   jax.ShapeDtypeStruct((m, wk), BF16), jax.ShapeDtypeStruct((m, wk), BF16),
                   jax.ShapeDtypeStruct((m, wi), BF16), jax.ShapeDtypeStruct((m, wi), BF16),
                   jax.ShapeDtypeStruct((m, 64), F32), jax.ShapeDtypeStruct((m, LANE), BF16)],
        compiler_params=pltpu.CompilerParams(dimension_semantics=("arbitrary",)),
        name="dsa_prep",
    )(p, p, p, p, p, *tables)
    return outs


INT32_MIN = -2 ** 31


def _sortable_key(x):
    b = lax.bitcast_convert_type(x, jnp.int32)
    return b ^ ((b >> 31) & jnp.int32(0x7FFFFFFF))


def _kth_largest_key(keys_ref, k_sel):
    kf = jnp.float32(k_sel)

    def count_ge(t):
        return jnp.sum(jnp.where(keys_ref[...] >= t, 1.0, 0.0), axis=1, keepdims=True)

    rows = keys_ref.shape[0]
    zero = jnp.zeros((rows, 1), jnp.int32)
    t0 = jnp.where(count_ge(zero) >= kf, zero, jnp.int32(INT32_MIN))

    def body(i, t):
        cand = t | (jnp.int32(1) << (30 - i))
        return jnp.where(count_ge(cand) >= kf, cand, t)

    return lax.fori_loop(0, 31, body, t0)


def _dsa_kernel(q_ref, k_ref, v_ref, iqlo_ref, iqhi_ref, ki2_ref, sm_ref, o_ref, score_ref, keys_ref,
                *, k_sel, kv_b, idx_dim, iw_lane, q_block0):
    qb = q_ref.shape[0]
    n_keys = k_ref.shape[0]
    group = q_ref.shape[1] // HEAD_DIM // kv_b
    q0 = (q_block0 + pl.program_id(1)) * qb
    ki2 = ki2_ref[...]
    w_scale = IDX_HEADS ** -0.5 * idx_dim ** -0.5
    heads_per_dot = 8
    score_ref[...] = jnp.zeros_like(score_ref)
    for gi in range(IDX_HEADS // heads_per_dot):
        rows = []
        for c in range(heads_per_dot // 2):
            sl = slice((gi * heads_per_dot // 2 + c) * LANE, (gi * heads_per_dot // 2 + c + 1) * LANE)
            rows.append(iqlo_ref[:, sl])
            rows.append(iqhi_ref[:, sl])
        lhs = jnp.concatenate(rows, axis=0)
        d = lax.dot_general(lhs, ki2, (((1,), (1,)), ((), ())), preferred_element_type=F32)
        acc = score_ref[...]
        for hh in range(heads_per_dot):
            lane = iw_lane + gi * heads_per_dot + hh
            w = sm_ref[:, lane:lane + 1] * w_scale
            acc = acc + jnp.maximum(d[hh * qb:(hh + 1) * qb], 0.0) * w
        score_ref[...] = acc

    qpos = q0 + lax.broadcasted_iota(jnp.int32, (qb, n_keys), 0)
    kpos = lax.broadcasted_iota(jnp.int32, (qb, n_keys), 1)
    visible = kpos <= qpos
    keys_ref[...] = jnp.where(visible, _sortable_key(score_ref[...]), jnp.int32(INT32_MIN))
    thr = _kth_largest_key(keys_ref, k_sel)
    neg = jnp.float32(-jnp.inf)
    score_ref[...] = jnp.where(visible, jnp.where(keys_ref[...] >= thr, 0.0, neg), neg)

    scale = HEAD_DIM ** -0.5
    for n in range(kv_b):
        q3 = jnp.concatenate(
            [q_ref[:, (n * group + g) * HEAD_DIM:(n * group + g + 1) * HEAD_DIM] for g in range(group)], axis=0)
        kn = k_ref[:, n * HEAD_DIM:(n + 1) * HEAD_DIM]
        lg = lax.dot_general(q3, kn, (((1,), (1,)), ((), ())), preferred_element_type=F32) * scale
        lg = lg.reshape(group, qb, n_keys) + score_ref[...][None]
        mx = jnp.max(lg, axis=-1, keepdims=True)
        pr = jnp.exp(lg - mx)
        den = jnp.sum(pr, axis=-1, keepdims=True)
        o = jnp.dot(pr.reshape(group * qb, n_keys).astype(BF16), v_ref[:, n * HEAD_DIM:(n + 1) * HEAD_DIM],
                    preferred_element_type=F32)
        o = o / den.reshape(group * qb, 1)
        for g in range(group):
            sl = slice((n * group + g) * HEAD_DIM, (n * group + g + 1) * HEAD_DIM)
            o_ref[:, sl] = o[g * qb:(g + 1) * qb].astype(o_ref.dtype)


def _dsa_prompt(q, kh, vh, iqlo, iqhi, ki2, p, lay, b, t, kv_b, idx_dim):
    m, wq = q.shape
    wk, wi = kh.shape[1], iqlo.shape[1]
    qb = min(Q_BLOCK, t)
    nq = t // qb
    k_sel = max(1, min(TOPK_MAX, t // 4))
    n_bands = 4 if nq % 4 == 0 else 1
    per_band = nq // n_bands
    kh3, vh3, ki3 = (x.reshape(b, t, x.shape[1]) for x in (kh, vh, ki2))
    outs = []
    for band in range(n_bands):
        first = band * per_band
        n_keys = (first + per_band) * qb
        qrow = lambda w, first=first: pl.BlockSpec((qb, w), lambda bi, qi: (bi * nq + first + qi, 0))
        seq = lambda w, n_keys=n_keys: pl.BlockSpec((None, n_keys, w), lambda bi, qi: (bi, 0, 0))
        small = pl.BlockSpec((qb, LANE), lambda bi, qi, first=first: (bi * nq + first + qi, lay['small'] // LANE))
        blocks = (_nbytes((qb, wq), BF16) * 2 + 2 * _nbytes((n_keys, wk), BF16) + 2 * _nbytes((qb, wi), BF16)
                  + _nbytes((n_keys, LANE), BF16) + _nbytes((qb, LANE), F32))
        temps = 2 * _nbytes((qb, n_keys), F32) + 2 * _nbytes((8 * qb, n_keys), F32)
        outs.append(pl.pallas_call(
            functools.partial(_dsa_kernel, k_sel=k_sel, kv_b=kv_b, idx_dim=idx_dim,
                              iw_lane=lay['pos']['iw'] - lay['small'], q_block0=first),
            grid=(b, per_band),
            in_specs=[qrow(wq), seq(wk), seq(wk), qrow(wi), qrow(wi), seq(LANE), small],
            out_specs=pl.BlockSpec((None, qb, wq), lambda bi, qi: (bi, qi, 0)),
            out_shape=jax.ShapeDtypeStruct((b, per_band * qb, wq), BF16),
            scratch_shapes=[pltpu.VMEM((qb, n_keys), F32), pltpu.VMEM((qb, n_keys), jnp.int32)],
            compiler_params=pltpu.CompilerParams(
                dimension_semantics=("arbitrary", "arbitrary"),
                vmem_limit_bytes=_vmem_limit(blocks, temps)),
            name="dsa",
        )(q, kh3, vh3, iqlo, iqhi, ki3, p))
    return jnp.concatenate(outs, axis=1).reshape(m, wq)


def _gdn_prep_kernel(x_ref, prev_ref, buf_ref, w_ref, q_ref, k_ref, v_ref, xp_ref, *, blocks_per_seq):
    bt = x_ref.shape[0]
    first = pl.program_id(0) % blocks_per_seq == 0
    xp_ref[0:8, :] = jnp.where(first, buf_ref[...], prev_ref[...])
    xp_ref[8:, :] = x_ref[...]
    n_heads = q_ref.shape[1] // HEAD_DIM
    for part, out_ref in enumerate((q_ref, k_ref, v_ref)):
        for h in range(n_heads):
            src = slice((part * n_heads + h) * HEAD_DIM, (part * n_heads + h + 1) * HEAD_DIM)
            y = xp_ref[8 - (CONV_W - 1):8 - (CONV_W - 1) + bt, src] * w_ref[0:1, src]
            for j in range(1, CONV_W):
                y = y + xp_ref[8 - (CONV_W - 1) + j:8 - (CONV_W - 1) + j + bt, src] * w_ref[j:j + 1, src]
            y = y * jax.nn.sigmoid(y)
            if part < 2:
                y = y * lax.rsqrt(jnp.sum(y * y, axis=-1, keepdims=True) + L2_EPS)
            if part == 0:
                y = y * HEAD_DIM ** -0.5
            out_ref[:, h * HEAD_DIM:(h + 1) * HEAD_DIM] = y


def _gdn_prep(p, lay, t, buf8, conv_w):
    m = p.shape[0]
    bt = min(256, t)
    nt = t // bt
    c = lay['size']['qkv_a']
    off = lay['pos']['qkv_a']
    assert off % c == 0
    cb = off // c
    wq = c // 3
    row = pl.BlockSpec((bt, wq), lambda i: (i, 0))
    blocks = _nbytes((bt, c), F32) + 2 * _nbytes((8, c), F32) + _nbytes((CONV_W, c), F32) + 3 * _nbytes((bt, wq), F32)
    return pl.pallas_call(
        functools.partial(_gdn_prep_kernel, blocks_per_seq=nt),
        grid=(m // bt,),
        in_specs=[pl.BlockSpec((bt, c), lambda i: (i, cb)),
                  pl.BlockSpec((8, c), lambda i: (jnp.maximum(i * (bt // 8) - 1, 0), cb)),
                  pl.BlockSpec((None, 8, c), lambda i: (i // nt, 0, 0)),
                  pl.BlockSpec((CONV_W, c), lambda i: (0, 0))],
        out_specs=[row, row, row],
        out_shape=[jax.ShapeDtypeStruct((m, wq), F32)] * 3,
        scratch_shapes=[pltpu.VMEM((bt + 8, c), F32)],
        compiler_params=pltpu.CompilerParams(
            dimension_semantics=("arbitrary",),
            vmem_limit_bytes=_vmem_limit(blocks, _nbytes((bt + 8, c), F32))),
        name="gdn_prep",
    )(p, p, buf8, conv_w)


def _dot_hi(a, b):
    ah, bh = a.astype(BF16), b.astype(BF16)
    al = (a - ah.astype(F32)).astype(BF16)
    bl = (b - bh.astype(F32)).astype(BF16)
    dot = functools.partial(jnp.dot, preferred_element_type=F32)
    return dot(ah, bh) + (dot(al, bh) + dot(ah, bl))


def _dot_nt(a, b):
    return lax.dot_general(a, b, (((1,), (1,)), ((), ())), preferred_element_type=F32)


def _dot_tn(a, b):
    return lax.dot_general(a, b, (((0,), (0,)), ((), ())), preferred_element_type=F32)


def _gdn_kernel(q_ref, k_ref, v_ref, g_ref, beta_ref, z_ref, gn_ref, s0_ref, o_ref, s_ref, *, chunk):
    t = q_ref.shape[0]
    hp = q_ref.shape[1] // HEAD_DIM
    c = chunk
    ii = lax.broadcasted_iota(jnp.int32, (c, c), 0)
    jj = lax.broadcasted_iota(jnp.int32, (c, c), 1)
    tril, strict, eye = ii >= jj, ii > jj, ii == jj

    @pl.when(pl.program_id(1) == 0)
    def _():
        s_ref[...] = s0_ref[...]

    gn = gn_ref[...]

    heads = range(hp)
    cols = [slice(j * HEAD_DIM, (j + 1) * HEAD_DIM) for j in heads]

    def body(i, carry):
        r0 = pl.multiple_of(i * c, c)
        g_row = [g_ref[j, pl.ds(i, 1), :] for j in heads]
        gc_col = [jnp.sum(jnp.where(tril, g_row[j], 0.0), axis=1, keepdims=True) for j in heads]
        gc_row = [jnp.sum(jnp.where(eye, gc_col[j], 0.0), axis=0, keepdims=True) for j in heads]
        beta_col = [jnp.sum(jnp.where(eye, beta_ref[j, pl.ds(i, 1), :], 0.0), axis=1, keepdims=True) for j in heads]
        decay = [jnp.where(tril, jnp.exp(jnp.where(tril, gc_col[j] - gc_row[j], 0.0)), 0.0) for j in heads]
        k = [k_ref[pl.ds(r0, c), cols[j]] for j in heads]
        kb = [k[j] * beta_col[j] for j in heads]
        kh = [k[j].astype(BF16) for j in heads]
        a = [jnp.where(strict, _dot_nt(kb[j].astype(BF16), kh[j]) * decay[j], 0.0) for j in heads]
        inv = [jnp.where(eye, 1.0, 0.0) - a[j] for j in heads]
        pw = a
        span = 2
        while span < c:
            pw = [_dot_hi(pw[j], pw[j]) for j in heads]
            inv = [inv[j] + _dot_hi(inv[j], pw[j]) for j in heads]
            span *= 2
        egc = [jnp.exp(gc_col[j]) for j in heads]
        u = [_dot_hi(inv[j], v_ref[pl.ds(r0, c), cols[j]] * beta_col[j]) for j in heads]
        w = [_dot_hi(inv[j], kb[j] * egc[j]) for j in heads]
        s = [s_ref[j] for j in heads]
        sh = [s[j].astype(BF16) for j in heads]
        vnh = [(u[j] - jnp.dot(w[j].astype(BF16), sh[j], preferred_element_type=F32)).astype(BF16) for j in heads]
        gl = [jnp.sum(g_row[j], axis=1, keepdims=True) for j in heads]
        for j in heads:
            s_ref[j] = s[j] * jnp.exp(gl[j]) + _dot_tn((k[j] * jnp.exp(gl[j] - gc_col[j])).astype(BF16), vnh[j])
        q = [q_ref[pl.ds(r0, c), cols[j]] for j in heads]
        intra = [_dot_nt(q[j].astype(BF16), kh[j]) * decay[j] for j in heads]
        o = [jnp.dot((q[j] * egc[j]).astype(BF16), sh[j], preferred_element_type=F32)
             + jnp.dot(intra[j].astype(BF16), vnh[j], preferred_element_type=F32) for j in heads]
        for j in heads:
            z = z_ref[pl.ds(r0, c), cols[j]]
            o_ref[pl.ds(r0, c), cols[j]] = (_rms_rows(o[j]) * gn * (z * jax.nn.sigmoid(z))).astype(o_ref.dtype)
        return carry

    lax.fori_loop(0, t // c, body, 0)


def _gdn_time_block(t):
    return min(4 * CHUNK, t)


def _gdn_gate_rows(x, b, t, h_a, c):
    tb = _gdn_time_block(t)
    x = jnp.transpose(x, (0, 2, 1)).reshape(b, h_a, t // tb, tb // c, c)
    return jnp.transpose(x, (0, 2, 1, 3, 4))


def _gdn(qn, kn, vv, g, beta, p, lay, gdn_g, s0, b, t):
    m, w = qn.shape
    h_a = w // HEAD_DIM
    c = g.shape[-1]
    tb = _gdn_time_block(t)
    nt = t // tb
    zoff = lay['pos']['za']
    assert zoff % w == 0
    seq = pl.BlockSpec((tb, w), lambda bi, ti: (bi * nt + ti, 0))
    gspec = pl.BlockSpec((None, None, h_a, tb // c, c), lambda bi, ti: (bi, ti, 0, 0, 0))
    sspec = pl.BlockSpec((None, h_a, HEAD_DIM, HEAD_DIM), lambda bi, ti: (bi, 0, 0, 0))
    blocks = 4 * _nbytes((tb, w), F32) + _nbytes((tb, w), BF16) + 2 * _nbytes((h_a, HEAD_DIM, HEAD_DIM), F32)
    return pl.pallas_call(
        functools.partial(_gdn_kernel, chunk=c),
        grid=(b, nt),
        in_specs=[seq, seq, seq, gspec, gspec,
                  pl.BlockSpec((tb, w), lambda bi, ti: (bi * nt + ti, zoff // w)),
                  pl.BlockSpec((1, HEAD_DIM), lambda bi, ti: (0, 0)), sspec],
        out_specs=[seq, sspec],
        out_shape=[jax.ShapeDtypeStruct((m, w), BF16), jax.ShapeDtypeStruct(s0.shape, F32)],
        compiler_params=pltpu.CompilerParams(
            dimension_semantics=("arbitrary", "arbitrary"),
            vmem_limit_bytes=_vmem_limit(blocks, 0)),
        name="gdn",
    )(qn, kn, vv, g, beta, p, gdn_g.reshape(1, HEAD_DIM), s0)


def _retention_kernel(q_ref, k_ref, v_ref, gate_ref, cos_ref, sin_ref, lg_ref, s0_ref, o_ref, s_ref, *, chunk):
    t = q_ref.shape[0]
    hp = q_ref.shape[1] // HEAD_DIM
    c = chunk
    ii = lax.broadcasted_iota(jnp.int32, (c, c), 0)
    jj = lax.broadcasted_iota(jnp.int32, (c, c), 1)
    tril = ii >= jj
    dist = (ii - jj).astype(F32)
    step = lax.broadcasted_iota(jnp.int32, (c, 1), 0).astype(F32)
    s_ref[...] = s0_ref[...]

    def body(i, carry):
        r0 = pl.multiple_of(i * c, c)
        cos_t = cos_ref[pl.ds(r0, c), :]
        sin_t = sin_ref[pl.ds(r0, c), :]
        for j in range(hp):
            sl = slice(j * HEAD_DIM, (j + 1) * HEAD_DIM)
            lg = lg_ref[j, 0:1, 0:1]
            decay = jnp.where(tril, jnp.exp(jnp.where(tril, dist * lg, 0.0)), 0.0)
            q = _rope_head(q_ref[pl.ds(r0, c), sl], cos_t, sin_t)
            k = _rope_head(k_ref[pl.ds(r0, c), sl], cos_t, sin_t) * HEAD_DIM ** -0.5
            vh = v_ref[pl.ds(r0, c), sl].astype(BF16)
            s = s_ref[j]
            intra = _dot_nt(q.astype(BF16), k.astype(BF16)) * decay
            o = (jnp.dot((q * jnp.exp((step + 1.0) * lg)).astype(BF16), s.astype(BF16), preferred_element_type=F32)
                 + jnp.dot(intra.astype(BF16), vh, preferred_element_type=F32))
            s_ref[j] = s * jnp.exp(c * lg) + _dot_tn((k * jnp.exp((c - 1.0 - step) * lg)).astype(BF16), vh)
            gt = gate_ref[pl.ds(r0, c), sl]
            o_ref[pl.ds(r0, c), sl] = (_rms_rows(o) * (gt * jax.nn.sigmoid(gt))).astype(o_ref.dtype)
        return carry

    lax.fori_loop(0, t // c, body, 0)


def _retention(p, lay, tables, s0, b, t, hp, chunk):
    m = p.shape[0]
    h_c = s0.shape[1]
    assert h_c % hp == 0 and t % chunk == 0
    wb = hp * HEAD_DIM
    pos = lay['pos']
    log_gamma = jnp.log1p(-jnp.exp2(RET_GAMMA_BASE - jnp.arange(h_c, dtype=F32)))
    lg = jnp.broadcast_to(log_gamma[:, None, None], (h_c, 8, LANE))

    def col(nm):
        assert pos[nm] % wb == 0
        return pl.BlockSpec((t, wb), lambda bi, hi: (bi, pos[nm] // wb + hi))

    tab = pl.BlockSpec((t, LANE), lambda bi, hi: (0, 0))
    sspec = pl.BlockSpec((None, hp, HEAD_DIM, HEAD_DIM), lambda bi, hi: (bi, hi, 0, 0))
    blocks = (4 * _nbytes((t, wb), F32) + 2 * _nbytes((t, LANE), F32) + _nbytes((t, wb), BF16)
              + 2 * _nbytes((hp, HEAD_DIM, HEAD_DIM), F32))
    return pl.pallas_call(
        functools.partial(_retention_kernel, chunk=chunk),
        grid=(b, h_c // hp),
        in_specs=[col('qc'), col('kc'), col('vc'), col('gc'), tab, tab,
                  pl.BlockSpec((hp, 8, LANE), lambda bi, hi: (hi, 0, 0)), sspec],
        out_specs=[pl.BlockSpec((t, wb), lambda bi, hi: (bi, hi)), sspec],
        out_shape=[jax.ShapeDtypeStruct((m, h_c * HEAD_DIM), BF16), jax.ShapeDtypeStruct(s0.shape, F32)],
        compiler_params=pltpu.CompilerParams(
            dimension_semantics=("arbitrary", "arbitrary"),
            vmem_limit_bytes=_vmem_limit(blocks, 0)),
        name="retention",
    )(p, p, p, p, tables[0], tables[1], lg, s0)


PAGES_PER_STEP = 4
MASKED = -1e30


def _stack_idx_heads(iqlo_ref, iqhi_ref):
    rows = []
    for c in range(iqlo_ref.shape[1] // LANE):
        sl = slice(c * LANE, (c + 1) * LANE)
        rows.append(iqlo_ref[:, sl])
        rows.append(iqhi_ref[:, sl])
    return jnp.concatenate(rows, axis=0).astype(BF16)


def _idx_scores(lhs, ki2, sm_ref, rows, iw_lane, w_scale):
    d = _dot_nt(lhs, ki2.astype(BF16))
    acc = jnp.zeros((rows, ki2.shape[0]), F32)
    for h in range(IDX_HEADS):
        w = sm_ref[:, iw_lane + h:iw_lane + h + 1] * w_scale
        acc = acc + jnp.maximum(d[h * rows:(h + 1) * rows], 0.0) * w
    return acc


def _dsa_s_scores_kernel(pt_ref, iqlo_ref, iqhi_ref, sm_ref, *rest, idx_dim, iw_lane):
    page_refs, score_ref = rest[:-1], rest[-1]
    rows = iqlo_ref.shape[0]
    page = page_refs[0].shape[0]
    lhs = _stack_idx_heads(iqlo_ref, iqhi_ref)
    w_scale = IDX_HEADS ** -0.5 * idx_dim ** -0.5
    for i, pr in enumerate(page_refs):
        score_ref[:, i * page:(i + 1) * page] = _idx_scores(lhs, pr[...], sm_ref, rows, iw_lane, w_scale)


def _dsa_s_scores(page_table, iqlo, iqhi, sm, pool_ki2, layer, t, idx_dim, iw_lane, pages_per_step):
    b, n_pages = page_table.shape
    page = pool_ki2.shape[2]
    wi = iqlo.shape[1]
    assert n_pages % pages_per_step == 0
    row = lambda w: pl.BlockSpec((t, w), lambda bi, gi, pt: (bi, 0))
    pages = [pl.BlockSpec((None, None, page, LANE),
                          lambda bi, gi, pt, i=i: (layer, pt[bi, gi * pages_per_step + i], 0, 0))
             for i in range(pages_per_step)]
    return pl.pallas_call(
        functools.partial(_dsa_s_scores_kernel, idx_dim=idx_dim, iw_lane=iw_lane),
        grid_spec=pltpu.PrefetchScalarGridSpec(
            num_scalar_prefetch=1,
            grid=(b, n_pages // pages_per_step),
            in_specs=[row(wi), row(wi), row(LANE)] + pages,
            out_specs=pl.BlockSpec((None, t, pages_per_step * page), lambda bi, gi, pt: (bi, 0, gi)),
        ),
        out_shape=jax.ShapeDtypeStruct((b, t, n_pages * page), F32),
        compiler_params=pltpu.CompilerParams(dimension_semantics=("arbitrary", "arbitrary")),
        name="dsa_s_scores",
    )(page_table, iqlo, iqhi, sm, *([pool_ki2] * pages_per_step))


def _dsa_s_attn_kernel(pt_ref, sp_ref, sn_ref, q_ref, kn_ref, vn_ref, *rest, k_sel, kv_b):
    n_pg = (len(rest) - 7) // 2
    k_refs, v_refs = rest[:n_pg], rest[n_pg:2 * n_pg]
    o_ref, keysp_ref, keysn_ref, thr_ref, m_ref, l_ref, acc_ref = rest[2 * n_pg:]
    t = q_ref.shape[0]
    page = k_refs[0].shape[0]
    n_heads = q_ref.shape[1] // HEAD_DIM
    group = n_heads // kv_b
    gi = pl.program_id(1)
    scale = HEAD_DIM ** -0.5
    qi = lax.broadcasted_iota(jnp.int32, (t, page), 0)
    kj = lax.broadcasted_iota(jnp.int32, (t, page), 1)
    new_visible = kj <= qi

    @pl.when(gi == 0)
    def _():
        keysp_ref[...] = _sortable_key(sp_ref[...])
        keysn_ref[...] = jnp.where(new_visible, _sortable_key(sn_ref[...]), jnp.int32(INT32_MIN))
        m_ref[...] = jnp.full_like(m_ref, MASKED)
        l_ref[...] = jnp.zeros_like(l_ref)
        acc_ref[...] = jnp.zeros_like(acc_ref)

        kf = jnp.float32(k_sel)

        def count_ge(thr):
            return (jnp.sum(jnp.where(keysp_ref[...] >= thr, 1.0, 0.0), axis=1, keepdims=True)
                    + jnp.sum(jnp.where(keysn_ref[...] >= thr, 1.0, 0.0), axis=1, keepdims=True))

        zero = jnp.zeros((t, 1), jnp.int32)
        thr0 = jnp.where(count_ge(zero) >= kf, zero, jnp.int32(INT32_MIN))

        def search(i, thr):
            cand = thr | (jnp.int32(1) << (30 - i))
            return jnp.where(count_ge(cand) >= kf, cand, thr)

        thr_ref[...] = jnp.broadcast_to(lax.fori_loop(0, 31, search, thr0), thr_ref.shape)

    thr = thr_ref[:, 0:1]

    q = q_ref[...]
    q3 = [jnp.concatenate([q[:, (n * group + g) * HEAD_DIM:(n * group + g + 1) * HEAD_DIM] for g in range(group)],
                          axis=0).astype(BF16) for n in range(kv_b)]

    def update(k_heads, v_heads, bias):
        n_keys = bias.shape[1]
        for n in range(kv_b):
            rows = slice(n * group * t, (n + 1) * group * t)
            lg = _dot_nt(q3[n], k_heads[n].astype(BF16)) * scale
            lg = (lg.reshape(group, t, n_keys) + bias[None]).reshape(group * t, n_keys)
            m_old = m_ref[rows, 0:1]
            m_new = jnp.maximum(m_old, jnp.max(lg, axis=-1, keepdims=True))
            alpha = jnp.exp(m_old - m_new)
            pr = jnp.exp(lg - m_new)
            l_ref[rows, :] = alpha * l_ref[rows, :] + jnp.sum(pr, axis=-1, keepdims=True)
            acc_ref[rows, :] = alpha * acc_ref[rows, :] + jnp.dot(pr.astype(BF16), v_heads[n].astype(BF16),
                                                                  preferred_element_type=F32)
            m_ref[rows, :] = jnp.broadcast_to(m_new, (group * t, LANE))

    biases = []
    for i in range(n_pg):
        start = pl.multiple_of((gi * n_pg + i) * page, page)
        biases.append(jnp.where(keysp_ref[:, pl.ds(start, page)] >= thr, 0.0, MASKED))
    update([jnp.concatenate([r[:, n, :] for r in k_refs], axis=0) for n in range(kv_b)],
           [jnp.concatenate([r[:, n, :] for r in v_refs], axis=0) for n in range(kv_b)],
           jnp.concatenate(biases, axis=1))

    @pl.when(gi == pl.num_programs(1) - 1)
    def _():
        bias_new = jnp.where(new_visible, jnp.where(keysn_ref[...] >= thr, 0.0, MASKED), MASKED)
        heads_of = lambda ref: [ref[:, n * HEAD_DIM:(n + 1) * HEAD_DIM] for n in range(kv_b)]
        update(heads_of(kn_ref), heads_of(vn_ref), bias_new)
        for h in range(n_heads):
            rows = slice(h * t, (h + 1) * t)
            o_ref[:, h * HEAD_DIM:(h + 1) * HEAD_DIM] = acc_ref[rows, :] / l_ref[rows, 0:1]


def _dsa_s_attn(page_table, scores_past, scores_new, q, k_new, v_new, pool_k, pool_v, layer, t, kv_b, k_sel,
                pages_per_step):
    b, n_pages = page_table.shape
    page = pool_k.shape[2]
    wq = q.shape[1]
    wk = k_new.shape[2]
    n_past = n_pages * page
    pk, pv = pool_k, pool_v
    pages = [pl.BlockSpec((None, None, page, kv_b, HEAD_DIM),
                          lambda bi, gi, pt, i=i: (layer, pt[bi, gi * pages_per_step + i], 0, 0, 0))
             for i in range(pages_per_step)]
    per_seq = lambda rows, w: pl.BlockSpec((None, rows, w), lambda bi, gi, pt: (bi, 0, 0))
    n_heads = wq // HEAD_DIM
    return pl.pallas_call(
        functools.partial(_dsa_s_attn_kernel, k_sel=k_sel, kv_b=kv_b),
        grid_spec=pltpu.PrefetchScalarGridSpec(
            num_scalar_prefetch=1,
            grid=(b, n_pages // pages_per_step),
            in_specs=[per_seq(t, n_past), per_seq(t, page), pl.BlockSpec((t, wq), lambda bi, gi, pt: (bi, 0)),
                      per_seq(page, wk), per_seq(page, wk)] + pages + pages,
            out_specs=pl.BlockSpec((t, wq), lambda bi, gi, pt: (bi, 0)),
            scratch_shapes=[pltpu.VMEM((t, n_past), jnp.int32), pltpu.VMEM((t, page), jnp.int32),
                            pltpu.VMEM((t, LANE), jnp.int32),
                            pltpu.VMEM((n_heads * t, LANE), F32), pltpu.VMEM((n_heads * t, LANE), F32),
                            pltpu.VMEM((n_heads * t, HEAD_DIM), F32)],
        ),
        out_shape=jax.ShapeDtypeStruct((b * t, wq), F32),
        compiler_params=pltpu.CompilerParams(dimension_semantics=("arbitrary", "arbitrary")),
        name="dsa_s_attn",
    )(page_table, scores_past, scores_new, q, k_new, v_new, *([pk] * pages_per_step), *([pv] * pages_per_step))


def _in_proj_layout(d_model, kv_b, idx_dim):
    d_mix = d_model
    h_a = (3 * d_mix) // (8 * HEAD_DIM)
    h_b = h_a
    h_c = d_mix // HEAD_DIM - h_a - h_b
    sizes = (3 * h_a * HEAD_DIM, h_a * HEAD_DIM, h_a, h_a,
             h_b * HEAD_DIM, kv_b * HEAD_DIM, kv_b * HEAD_DIM, IDX_HEADS * idx_dim, idx_dim, IDX_HEADS,
             h_c * HEAD_DIM, h_c * HEAD_DIM, h_c * HEAD_DIM, h_c * HEAD_DIM)
    names = ('qkv_a', 'za', 'aa', 'ba', 'qb', 'kb', 'vb', 'iq', 'ik', 'iw', 'qc', 'kc', 'vc', 'gc')
    src, size, off = {}, {}, 0
    for nm, s in zip(names, sizes):
        src[nm] = off
        size[nm] = s
        off += s
    wide = ('iq', 'qc', 'kc', 'vc', 'gc', 'qb', 'za', 'qkv_a', 'kb', 'vb')
    narrow = ('ik', 'iw', 'aa', 'ba')
    pos, cur = {}, 0
    for nm in wide:
        cur = -(-cur // size[nm]) * size[nm]
        pos[nm] = cur
        cur += size[nm]
    small = -(-cur // LANE) * LANE
    cur = small
    for nm in narrow:
        pos[nm] = cur
        cur += size[nm]
    assert cur <= small + LANE
    lay = dict(src=src, size=size, pos=pos, small=small, order=wide + narrow, n_src=off)
    return (h_a, h_b, h_c), lay


def _rearrange_w_in(w_in, lay, bn):
    cols, cur = [], 0
    for nm in lay['order']:
        gap = lay['pos'][nm] - cur
        if gap:
            cols.append(jnp.zeros(w_in.shape[:-1] + (gap,), w_in.dtype))
        cols.append(w_in[..., lay['src'][nm]:lay['src'][nm] + lay['size'][nm]])
        cur = lay['pos'][nm] + lay['size'][nm]
    n_pad = -(-(lay['small'] + LANE) // bn) * bn
    cols.append(jnp.zeros(w_in.shape[:-1] + (n_pad - cur,), w_in.dtype))
    return jnp.concatenate(cols, axis=-1).astype(BF16)


def _proj_part(p, lay, nm, b, t):
    return p[:, lay['pos'][nm]:lay['pos'][nm] + lay['size'][nm]].reshape(b, t, lay['size'][nm])


def _prompt_mixer(p, lay, heads, kv_b, idx_dim, b, t, conv_w, a_log, dt_bias, gdn_g):
    h_a, h_b, h_c = heads
    pos_ids = jnp.arange(t, dtype=jnp.int32)
    tab_head = _rope_tables(pos_ids, HEAD_DIM, LANE)
    tab_idx = _rope_tables(pos_ids, idx_dim, LANE)

    q, kf, kh, vh, iqlo, iqhi, kif, ki2 = _dsa_prep(p, lay, t, tab_head + tab_idx)
    ob = _dsa_prompt(q, kh, vh, iqlo, iqhi, ki2, p, lay, b, t, kv_b, idx_dim)

    n_conv = lay['size']['qkv_a']
    qn, kn, vv = _gdn_prep(p, lay, t, jnp.zeros((b, 8, n_conv), F32), conv_w)
    c = min(CHUNK, t)
    aa = _proj_part(p, lay, 'aa', b, t)
    ba = _proj_part(p, lay, 'ba', b, t)
    g = -jnp.exp(a_log) * jax.nn.softplus(aa + dt_bias)
    oa, s_gdn = _gdn(qn, kn, vv, _gdn_gate_rows(g, b, t, h_a, c), _gdn_gate_rows(jax.nn.sigmoid(ba), b, t, h_a, c),
                     p, lay, gdn_g, jnp.zeros((b, h_a, HEAD_DIM, HEAD_DIM), F32), b, t)

    oc, s_ret = _retention(p, lay, tab_head, jnp.zeros((b, h_c, HEAD_DIM, HEAD_DIM), F32), b, t,
                           hp=2, chunk=min(2 * CHUNK, t))

    o = jnp.concatenate([oa, ob, oc], axis=-1)
    vb = _proj_part(p, lay, 'vb', b, t).reshape(b, t, kv_b, HEAD_DIM)
    conv_new = _proj_part(p, lay, 'qkv_a', b, t)[:, t - (CONV_W - 1):]
    state = (kf.reshape(b, t, kv_b, HEAD_DIM), vb, kif.reshape(b, t, idx_dim), s_gdn, conv_new, s_ret)
    return o, state


def _sample_mixer(p, lay, heads, kv_b, idx_dim, b, t, conv_w, a_log, dt_bias, gdn_g, past, layer):
    h_a, h_b, h_c = heads
    cache_k, cache_v, pool_ki2, state_gdn, state_gdn_conv, state_ret, page_table = past
    page = cache_k.shape[2]
    n_past = page_table.shape[1] * page
    m = b * t
    assert t <= CHUNK and t >= CONV_W - 1
    pos_ids = n_past + jnp.arange(t, dtype=jnp.int32)
    per_row = lambda tabs: tuple(jnp.tile(x, (b, 1)) for x in tabs)
    tables = per_row(_rope_tables(pos_ids, HEAD_DIM, LANE)) + per_row(_rope_tables(pos_ids, idx_dim, LANE))
    q, kf, kh, vh, iqlo, iqhi, kif, ki2 = _dsa_prep(p, lay, m, tables)

    f32 = lambda x: x.astype(F32)
    to_page = lambda x: jnp.pad(x.reshape(b, t, -1), ((0, 0), (0, page - t), (0, 0)))
    sm = p[:, lay['small']:lay['small'] + LANE]
    iw_lane = lay['pos']['iw'] - lay['small']
    scores_past = _dsa_s_scores(page_table, f32(iqlo), f32(iqhi), sm, pool_ki2, layer, t, idx_dim, iw_lane,
                                PAGES_PER_STEP)
    own = jnp.arange(b, dtype=jnp.int32).reshape(b, 1)
    scores_new = _dsa_s_scores(own, f32(iqlo), f32(iqhi), sm, to_page(f32(ki2))[None], 0, t, idx_dim, iw_lane, 1)
    k_sel = max(1, min(TOPK_MAX, (n_past + t) // 4))
    vb = _proj_part(p, lay, 'vb', b, t)
    ob = _dsa_s_attn(page_table, scores_past, scores_new, f32(q), to_page(kf), to_page(vb), cache_k, cache_v,
                     layer, t, kv_b, k_sel, PAGES_PER_STEP)

    tp = CHUNK
    lead = tp - t
    padl = lambda x: jnp.pad(x.reshape(b, t, -1), ((0, 0), (lead, 0), (0, 0)))
    rows = lambda x: padl(x).reshape(b * tp, -1)
    p_pad = rows(p)
    buf8 = jnp.pad(state_gdn_conv[layer], ((0, 0), (8 - (CONV_W - 1), 0), (0, 0)))
    qn, kn, vv = _gdn_prep(p, lay, t, buf8, conv_w)
    aa = _proj_part(p, lay, 'aa', b, t)
    ba = _proj_part(p, lay, 'ba', b, t)
    g = -jnp.exp(a_log) * jax.nn.softplus(aa + dt_bias)
    oa, s_gdn = _gdn(rows(qn), rows(kn), rows(vv), _gdn_gate_rows(padl(g), b, tp, h_a, tp),
                     _gdn_gate_rows(padl(jax.nn.sigmoid(ba)), b, tp, h_a, tp), p_pad, lay, gdn_g,
                     state_gdn[layer], b, tp)
    log_gamma = jnp.log1p(-jnp.exp2(RET_GAMMA_BASE - jnp.arange(h_c, dtype=F32)))
    s_ret0 = state_ret[layer] * jnp.exp(-lead * log_gamma)[None, :, None, None]
    tab_pad = _rope_tables(n_past - lead + jnp.arange(tp, dtype=jnp.int32), HEAD_DIM, LANE)
    oc, s_ret = _retention(p_pad, lay, tab_pad, s_ret0, b, tp, hp=2, chunk=tp)
    tail = lambda x: x.reshape(b, tp, -1)[:, lead:].reshape(m, -1)

    o = jnp.concatenate([tail(oa), ob.astype(BF16), tail(oc)], axis=-1)
    conv_new = _proj_part(p, lay, 'qkv_a', b, t)[:, t - (CONV_W - 1):]
    state = (kf.reshape(b, t, kv_b, HEAD_DIM), vb.reshape(b, t, kv_b, HEAD_DIM), kif.reshape(b, t, idx_dim),
             s_gdn, conv_new, s_ret)
    return o, state


def _forward(xs, mods, weights, past, cfg):
    (norm_pre, norm_post, w_in_r, gdn_conv_w, gdn_a_log, gdn_dt_bias, gdn_norm,
     w_out, ffn_w_gate, ffn_w_up, ffn_w_down) = weights
    heads, lay, kv_b, idx_dim = cfg
    depth = w_out.shape[0]
    d = xs[0].shape[-1]
    bt = [x.shape[:2] for x in xs]
    m = bt[0][0] * bt[0][1]
    bm = min(1024, m)
    bm_down = min(256, m)
    hs = [x.reshape(-1, d) for x in xs]
    vec = lambda g, l, s, j: mods[g][l, :, s, j].reshape(bt[g][0], 1, d)
    gpre = lambda l, s: norm_pre[l, s].reshape(1, d)
    gpost = lambda l, s: norm_post[l, s].reshape(1, d)

    def resid(ys, l, s, nxt, res_w):
        out = [_resid(hs[g], ys[g], vec(g, l, s, 2), gpost(l, s),
                      None if nxt is None else (gpre(*nxt), vec(g, *nxt, 1), vec(g, *nxt, 0)), bt[g][1], res_w)
               for g in range(2)]
        return [o[0] for o in out], [o[1] for o in out]

    def ffn(us, l, j):
        hf = _ffn_in(us[0], us[1], ffn_w_gate, ffn_w_up, (l, j), bm=bm, bn=MXU_COLS)
        return _matmul(hf[0], hf[1], ffn_w_down, (l, j), bm=bm_down, bn=2 * MXU_COLS, out_dtype=F32,
                       name="ffn_down", single_buffer_w=True)

    states = ([], [])
    us = [_norm_mod(hs[g], gpre(0, 0), vec(g, 0, 0, 1), vec(g, 0, 0, 0), bt[g][1]) for g in range(2)]
    for l in range(depth):
        hs, us = resid(ffn(us, l, 0), l, 0, (l, 1), FFN_RES)

        ps = _matmul(us[0], us[1], w_in_r, (l,), bm=bm, bn=2 * MXU_COLS, out_dtype=F32, name="in_proj")
        mix = (gdn_conv_w[l], gdn_a_log[l], gdn_dt_bias[l], gdn_norm[l])
        o_p, st_p = _prompt_mixer(ps[0], lay, heads, kv_b, idx_dim, *bt[0], *mix)
        o_s, st_s = _sample_mixer(ps[1], lay, heads, kv_b, idx_dim, *bt[1], *mix, past, l)
        states[0].append(st_p)
        states[1].append(st_s)
        ys = _matmul(o_p, o_s, w_out, (l,), bm=bm, bn=MXU_COLS, out_dtype=F32, name="out_proj")
        hs, us = resid(ys, l, 1, (l, 2), 1.0)

        hs, us = resid(ffn(us, l, 1), l, 2, (l + 1, 0) if l + 1 < depth else None, FFN_RES)
    new_state = [[jnp.stack([s[i] for s in states[g]]) for i in range(6)] for g in range(2)]
    return [hs[g].reshape(bt[g] + (d,)) for g in range(2)], new_state


def kernel(x_prompt, x_sample, cache_k, cache_v, cache_kidx, state_gdn, state_gdn_conv, state_ret, page_table,
           c_prompt, c_sample, w_ada, b_ada, norm_pre, norm_post, w_in, gdn_conv_w, gdn_a_log, gdn_dt_bias,
           gdn_norm, w_out, ffn_w_gate, ffn_w_up, ffn_w_down):
    d = x_prompt.shape[-1]
    depth = w_ada.shape[0]
    kv_b = cache_k.shape[3]
    idx_dim = cache_kidx.shape[-1]
    heads, lay = _in_proj_layout(d, kv_b, idx_dim)
    assert lay['n_src'] == w_in.shape[-1] and 2 * idx_dim == LANE
    bp, bs = c_prompt.shape[0], c_sample.shape[0]

    rows = -(-(bp + bs) // 8) * 8
    c_rows = jnp.concatenate([c_prompt, c_sample, jnp.zeros((rows - bp - bs, d), F32)], axis=0)
    mod = _adaln(c_rows, w_ada, b_ada).reshape(depth, rows, N_SUB, 3, d)

    w_in_r = _rearrange_w_in(w_in, lay, 2 * MXU_COLS)
    weights = (norm_pre, norm_post, w_in_r, gdn_conv_w, gdn_a_log, gdn_dt_bias, gdn_norm,
               w_out, ffn_w_gate, ffn_w_up, ffn_w_down)
    cfg = (heads, lay, kv_b, idx_dim)
    pool_ki2 = jnp.concatenate([cache_kidx, cache_kidx], axis=-1)
    past = (cache_k, cache_v, pool_ki2, state_gdn, state_gdn_conv, state_ret, page_table)
    (y_p, y_s), (st_p, st_s) = _forward((x_prompt, x_sample), (mod[:, :bp], mod[:, bp:bp + bs]), weights, past, cfg)
    return (y_p, y_s, *st_p, *st_s)
```

```python
import functools

import jax
import jax.numpy as jnp
from jax import lax
from jax.experimental import pallas as pl
from jax.experimental.pallas import tpu as pltpu

F32 = jnp.float32
BF16 = jnp.bfloat16

HEAD_DIM = 128
IDX_HEADS = 32
TOPK_MAX = 256
CONV_W = 4
FFN_RES = 0.5
N_SUB = 3
CHUNK = 64
Q_BLOCK = 128
ROPE_THETA = 10000.0
RET_GAMMA_BASE = -5.0
EPS = 1e-6
L2_EPS = 1e-6

V7X_VMEM_BYTES = 64 * 1024 * 1024
V7X_VMEM_BUDGET = V7X_VMEM_BYTES - 8 * 1024 * 1024
LANE = 128
MXU_COLS = 256


def _vmem_limit(block_bytes, scratch_bytes):
    need = 2 * block_bytes + scratch_bytes + 8 * 1024 * 1024
    assert need <= V7X_VMEM_BUDGET, need
    return need


def _nbytes(shape, dtype):
    n = 1
    for s in shape:
        n *= s
    return n * jnp.dtype(dtype).itemsize


def _adaln_kernel(c_ref, w_ref, b_ref, o_ref):
    c = c_ref[...]
    cs = (c * jax.nn.sigmoid(c)).astype(BF16)
    w = w_ref[...].astype(BF16)
    o_ref[...] = jnp.dot(cs, w, preferred_element_type=F32) + b_ref[...]


def _adaln(c_rows, w_ada, b_ada):
    depth, d, n = w_ada.shape
    rows = c_rows.shape[0]
    bn = 1024
    assert n % bn == 0
    blocks = _nbytes((rows, d), F32) + _nbytes((d, bn), F32) + _nbytes((1, bn), F32) + _nbytes((rows, bn), F32)
    return pl.pallas_call(
        _adaln_kernel,
        grid=(depth, n // bn),
        in_specs=[
            pl.BlockSpec((rows, d), lambda l, j: (0, 0)),
            pl.BlockSpec((None, d, bn), lambda l, j: (l, 0, j)),
            pl.BlockSpec((None, 1, bn), lambda l, j: (l, 0, j)),
        ],
        out_specs=pl.BlockSpec((None, rows, bn), lambda l, j: (l, 0, j)),
        out_shape=jax.ShapeDtypeStruct((depth, rows, n), F32),
        compiler_params=pltpu.CompilerParams(
            dimension_semantics=("arbitrary", "arbitrary"),
            vmem_limit_bytes=_vmem_limit(blocks, _nbytes((d, bn), BF16))),
        name="adaln",
    )(c_rows, w_ada, b_ada.reshape(depth, 1, n))


def _mm_kernel(x_ref, xs_ref, w_ref, o_ref, os_ref, *scratch, cast_w):
    first = pl.program_id(1) == 0
    if cast_w:
        (wb_ref,) = scratch

        @pl.when(first)
        def _():
            wb_ref[...] = w_ref[...].astype(BF16)

        w_src = wb_ref
    else:
        w_src = w_ref

    @pl.when(first)
    def _():
        os_ref[...] = jnp.dot(xs_ref[...], w_src[...], preferred_element_type=F32).astype(os_ref.dtype)

    o_ref[...] = jnp.dot(x_ref[...], w_src[...], preferred_element_type=F32).astype(o_ref.dtype)


def _matmul(x, xs, w, w_index, *, bm, bn, out_dtype, name, single_buffer_w=False):
    m, k = x.shape
    ms = xs.shape[0]
    n = w.shape[-1]
    assert w.shape[-2] == k and m % bm == 0 and n % bn == 0, (x.shape, w.shape, bm, bn)
    cast_w = w.dtype != BF16
    lead = len(w_index)
    scratch = [pltpu.VMEM((k, bn), BF16)] if cast_w else []
    w_bytes = _nbytes((k, bn), w.dtype)
    blocks = (_nbytes((bm, k), x.dtype) + _nbytes((bm, bn), out_dtype) + _nbytes((ms, bn), out_dtype)
              + (0 if single_buffer_w else w_bytes))
    fixed = ((_nbytes((k, bn), BF16) if cast_w else 0) + (w_bytes if single_buffer_w else 0)
             + _nbytes((ms, k), xs.dtype))
    w_mode = dict(pipeline_mode=pl.Buffered(1)) if single_buffer_w else {}
    return pl.pallas_call(
        functools.partial(_mm_kernel, cast_w=cast_w),
        grid=(n // bn, m // bm),
        in_specs=[
            pl.BlockSpec((bm, k), lambda j, i: (i, 0)),
            pl.BlockSpec((ms, k), lambda j, i: (0, 0), pipeline_mode=pl.Buffered(1)),
            pl.BlockSpec((None,) * lead + (k, bn), lambda j, i: tuple(w_index) + (0, j), **w_mode),
        ],
        out_specs=[pl.BlockSpec((bm, bn), lambda j, i: (i, j)), pl.BlockSpec((ms, bn), lambda j, i: (0, j))],
        out_shape=[jax.ShapeDtypeStruct((m, n), out_dtype), jax.ShapeDtypeStruct((ms, n), out_dtype)],
        scratch_shapes=scratch,
        compiler_params=pltpu.CompilerParams(
            dimension_semantics=("arbitrary", "arbitrary"),
            vmem_limit_bytes=_vmem_limit(blocks, fixed)),
        name=name,
    )(x, xs, w)


def _ffn_in_kernel(x_ref, xs_ref, wg_ref, wu_ref, o_ref, os_ref, wgb_ref, wub_ref):
    def swiglu(x):
        g = jnp.dot(x, wgb_ref[...], preferred_element_type=F32)
        u = jnp.dot(x, wub_ref[...], preferred_element_type=F32)
        return g * jax.nn.sigmoid(g) * u

    @pl.when(pl.program_id(1) == 0)
    def _():
        wgb_ref[...] = wg_ref[...].astype(BF16)
        wub_ref[...] = wu_ref[...].astype(BF16)
        os_ref[...] = swiglu(xs_ref[...]).astype(os_ref.dtype)

    o_ref[...] = swiglu(x_ref[...]).astype(o_ref.dtype)


def _ffn_in(x, xs, wg, wu, w_index, *, bm, bn):
    m, k = x.shape
    ms = xs.shape[0]
    n = wg.shape[-1]
    assert m % bm == 0 and n % bn == 0
    lead = len(w_index)
    wspec = pl.BlockSpec((None,) * lead + (k, bn), lambda j, i: tuple(w_index) + (0, j))
    blocks = _nbytes((bm, k), BF16) + 2 * _nbytes((k, bn), F32) + _nbytes((bm, bn), BF16) + _nbytes((ms, bn), BF16)
    return pl.pallas_call(
        _ffn_in_kernel,
        grid=(n // bn, m // bm),
        in_specs=[pl.BlockSpec((bm, k), lambda j, i: (i, 0)),
                  pl.BlockSpec((ms, k), lambda j, i: (0, 0), pipeline_mode=pl.Buffered(1)), wspec, wspec],
        out_specs=[pl.BlockSpec((bm, bn), lambda j, i: (i, j)), pl.BlockSpec((ms, bn), lambda j, i: (0, j))],
        out_shape=[jax.ShapeDtypeStruct((m, n), BF16), jax.ShapeDtypeStruct((ms, n), BF16)],
        scratch_shapes=[pltpu.VMEM((k, bn), BF16), pltpu.VMEM((k, bn), BF16)],
        compiler_params=pltpu.CompilerParams(
            dimension_semantics=("arbitrary", "arbitrary"),
            vmem_limit_bytes=_vmem_limit(blocks, 2 * _nbytes((k, bn), BF16) + _nbytes((ms, k), BF16))),
        name="ffn_in",
    )(x, xs, wg, wu)


def _rms_rows(x):
    return x * lax.rsqrt(jnp.mean(x * x, axis=-1, keepdims=True) + EPS)


def _norm_mod_kernel(h_ref, gpre_ref, scale_ref, shift_ref, u_ref):
    u = _rms_rows(h_ref[...]) * gpre_ref[...] * (1.0 + scale_ref[...]) + shift_ref[...]
    u_ref[...] = u.astype(u_ref.dtype)


def _resid_kernel(h_ref, y_ref, gate_ref, gpost_ref, *rest, res_w, with_next):
    yn = _rms_rows(y_ref[...]) * gpost_ref[...]
    h = h_ref[...] + res_w * gate_ref[...] * yn
    if with_next:
        gpre_ref, scale_ref, shift_ref, hn_ref, u_ref = rest
        u = _rms_rows(h) * gpre_ref[...] * (1.0 + scale_ref[...]) + shift_ref[...]
        u_ref[...] = u.astype(u_ref.dtype)
    else:
        (hn_ref,) = rest
    hn_ref[...] = h


def _row_block(rows_per_seq):
    return min(256, rows_per_seq)


def _seq_vec_spec(d, bm, rows_per_seq):
    return pl.BlockSpec((None, 1, d), lambda i: (i * bm // rows_per_seq, 0, 0))


def _norm_mod(h, gpre, scale, shift, rows_per_seq):
    m, d = h.shape
    bm = _row_block(rows_per_seq)
    row = pl.BlockSpec((bm, d), lambda i: (i, 0))
    vec = pl.BlockSpec((1, d), lambda i: (0, 0))
    sv = _seq_vec_spec(d, bm, rows_per_seq)
    return pl.pallas_call(
        _norm_mod_kernel,
        grid=(m // bm,),
        in_specs=[row, vec, sv, sv],
        out_specs=row,
        out_shape=jax.ShapeDtypeStruct((m, d), BF16),
        compiler_params=pltpu.CompilerParams(dimension_semantics=("arbitrary",)),
        name="norm_mod",
    )(h, gpre, scale, shift)


def _resid(h, y, gate, gpost, nxt, rows_per_seq, res_w):
    m, d = h.shape
    bm = _row_block(rows_per_seq)
    row = pl.BlockSpec((bm, d), lambda i: (i, 0))
    vec = pl.BlockSpec((1, d), lambda i: (0, 0))
    sv = _seq_vec_spec(d, bm, rows_per_seq)
    with_next = nxt is not None
    in_specs = [row, row, sv, vec] + ([vec, sv, sv] if with_next else [])
    out_specs = [row, row] if with_next else [row]
    out_shape = [jax.ShapeDtypeStruct((m, d), F32)] + ([jax.ShapeDtypeStruct((m, d), BF16)] if with_next else [])
    outs = pl.pallas_call(
        functools.partial(_resid_kernel, res_w=res_w, with_next=with_next),
        grid=(m // bm,),
        in_specs=in_specs,
        out_specs=out_specs,
        out_shape=out_shape,
        compiler_params=pltpu.CompilerParams(dimension_semantics=("arbitrary",)),
        name="resid",
    )(h, y, gate, gpost, *(nxt if with_next else ()))
    return (outs[0], outs[1]) if with_next else (outs[0], None)


def _l2_norm(x):
    return x * lax.rsqrt(jnp.sum(x * x, axis=-1, keepdims=True) + L2_EPS)


def _rope(x, pos):
    half = x.shape[-1] // 2
    inv = ROPE_THETA ** (-jnp.arange(half, dtype=F32) / half)
    ang = pos.astype(F32)[:, None] * inv[None, :]
    cos = jnp.cos(ang)[None, :, None, :]
    sin = jnp.sin(ang)[None, :, None, :]
    x1, x2 = x[..., :half], x[..., half:]
    return jnp.concatenate([x1 * cos - x2 * sin, x2 * cos + x1 * sin], axis=-1)


def _causal_conv(x, buf, w):
    t = x.shape[1]
    xp = jnp.concatenate([buf, x], axis=1)
    y = xp[:, 0:t] * w[0]
    for j in range(1, CONV_W):
        y = y + xp[:, j:j + t] * w[j]
    return jax.nn.silu(y), xp[:, t:]


def _to_chunks(x, c, pad):
    x = jnp.pad(x, [(0, 0), (0, pad)] + [(0, 0)] * (x.ndim - 2))
    b, tp = x.shape[:2]
    x = x.reshape((b, tp // c, c) + x.shape[2:])
    return jnp.transpose(x, (1, 0, 3, 2) + tuple(range(4, x.ndim)))


def _from_chunks(o, t):
    n, b, h, c, d = o.shape
    return jnp.transpose(o, (1, 0, 3, 2, 4)).reshape(b, n * c, h, d)[:, :t]


def _chunk_decay(gc):
    c = gc.shape[-1]
    tri = jnp.tril(jnp.ones((c, c), bool))
    diff = gc[..., :, None] - gc[..., None, :]
    return jnp.where(tri, jnp.exp(jnp.where(tri, diff, 0.0)), 0.0)


def _gated_delta_rule(q, k, v, g, beta, s0):
    t, dk = q.shape[1], q.shape[-1]
    dv = v.shape[-1]
    c = min(CHUNK, t)
    pad = (-t) % c
    q = _to_chunks(q * dk ** -0.5, c, pad)
    k = _to_chunks(k, c, pad)
    v = _to_chunks(v, c, pad)
    beta = _to_chunks(beta, c, pad)
    gc = jnp.cumsum(_to_chunks(g, c, pad), axis=-1)
    decay = _chunk_decay(gc)
    kb = k * beta[..., None]
    strict = jnp.tril(jnp.ones((c, c), bool), -1)
    lower = jnp.where(strict, jnp.einsum('nbhik,nbhjk->nbhij', kb, k) * decay, 0.0) + jnp.eye(c, dtype=F32)
    rhs = jnp.concatenate([v * beta[..., None], kb * jnp.exp(gc)[..., None]], axis=-1)
    sol = lax.linalg.triangular_solve(lower, rhs, left_side=True, lower=True, unit_diagonal=True)
    u, w = sol[..., :dv], sol[..., dv:]

    def step(S, xs):
        qc, kc, uc, wc, gcc, dc = xs
        v_new = uc - jnp.einsum('bhik,bhkv->bhiv', wc, S)
        intra = jnp.einsum('bhik,bhjk->bhij', qc, kc) * dc
        o = (jnp.einsum('bhik,bhkv->bhiv', qc * jnp.exp(gcc)[..., None], S)
             + jnp.einsum('bhij,bhjv->bhiv', intra, v_new))
        gl = gcc[..., -1:]
        S = S * jnp.exp(gl)[..., None] + jnp.einsum('bhjk,bhjv->bhkv', kc * jnp.exp(gl - gcc)[..., None], v_new)
        return S, o

    S, o = lax.scan(step, s0, (q, k, u, w, gc, decay))
    return _from_chunks(o, t), S


def _scan_retention(q, k, v, logd, s0):
    t = q.shape[1]
    c = min(CHUNK, t)
    pad = (-t) % c
    q = _to_chunks(q, c, pad)
    k = _to_chunks(k, c, pad)
    v = _to_chunks(v, c, pad)
    gc = jnp.cumsum(_to_chunks(logd, c, pad), axis=-1)
    decay = _chunk_decay(gc)

    def step(S, xs):
        qc, kc, vc, gcc, dc = xs
        intra = jnp.einsum('bhik,bhjk->bhij', qc, kc) * dc
        o = (jnp.einsum('bhik,bhkv->bhiv', qc * jnp.exp(gcc)[..., None], S)
             + jnp.einsum('bhij,bhjv->bhiv', intra, vc))
        gl = gcc[..., -1:]
        S = S * jnp.exp(gl)[..., None] + jnp.einsum('bhjk,bhjv->bhkv', kc * jnp.exp(gl - gcc)[..., None], vc)
        return S, o

    S, o = lax.scan(step, s0, (q, k, v, gc, decay))
    return _from_chunks(o, t), S


def _gather_rows(rows, idx):
    return jax.vmap(lambda r, i: r[i])(rows, idx)


def _paged_gather(pool, page_table, new_rows, idx):
    page = pool.shape[1]
    n_past = page_table.shape[1] * page
    t = new_rows.shape[1]
    pidx = jnp.minimum(idx, n_past - 1)
    phys = jax.vmap(lambda pt, i: pt[i // page])(page_table, pidx)
    past_rows = pool[phys, pidx % page]
    new_r = _gather_rows(new_rows, jnp.clip(idx - n_past, 0, t - 1))
    is_past = (idx < n_past).reshape(idx.shape + (1,) * (past_rows.ndim - idx.ndim))
    return jnp.where(is_past, past_rows, new_r)


def _sparse_attention(q, iq, iw, kidx_all, qpos, gather_kv, k_sel):
    b, t, h_b, _ = q.shape
    qb_sz = min(Q_BLOCK, t)
    pad = (-t) % qb_sz
    nb = (t + pad) // qb_sz

    def blocks(a):
        a = jnp.pad(a, [(0, 0), (0, pad)] + [(0, 0)] * (a.ndim - 2))
        return jnp.moveaxis(a.reshape((b, nb, qb_sz) + a.shape[2:]), 1, 0)

    qpos_b = jnp.pad(qpos, (0, pad), mode='edge').reshape(nb, qb_sz)
    kpos = jnp.arange(kidx_all.shape[1], dtype=jnp.int32)

    def one_block(args):
        qb, iqb, iwb, pb = args
        rel = jax.nn.relu(jnp.einsum('bqhd,bsd->bqhs', iqb, kidx_all))
        score = jnp.einsum('bqhs,bqh->bqs', rel, iwb)
        visible = kpos[None, :] <= pb[:, None]
        score = jnp.where(visible[None], score, -jnp.inf)
        _, idx = lax.top_k(score, k_sel)
        valid = idx <= pb[None, :, None]
        kr, vr = gather_kv(idx)
        kv_b = kr.shape[3]
        qg = qb.reshape(b, qb_sz, kv_b, h_b // kv_b, HEAD_DIM)
        logits = jnp.einsum('bqngd,bqsnd->bqngs', qg, kr) * HEAD_DIM ** -0.5
        logits = jnp.where(valid[:, :, None, None, :], logits, -jnp.inf)
        p = jax.nn.softmax(logits, axis=-1)
        o = jnp.einsum('bqngs,bqsnd->bqngd', p, vr)
        return o.reshape(b, qb_sz, h_b, HEAD_DIM)

    o = lax.map(one_block, (blocks(q), blocks(iq), blocks(iw), qpos_b))
    return jnp.moveaxis(o, 0, 1).reshape(b, nb * qb_sz, h_b, HEAD_DIM)[:, :t]


def _token_mixer(parts, dims, conv_w, a_log, dt_bias, gdn_g, past):
    (qkv_a, za, qb, kb, vb, iq, qc, kc, vc, gc, aa, ba, ik, iw) = parts
    h_a, h_b, h_c, kv_b, idx_dim = dims
    b, t = qkv_a.shape[:2]
    if past is None:
        conv_buf = jnp.zeros((b, CONV_W - 1, qkv_a.shape[-1]), F32)
        s_gdn = jnp.zeros((b, h_a, HEAD_DIM, HEAD_DIM), F32)
        s_ret = jnp.zeros((b, h_c, HEAD_DIM, HEAD_DIM), F32)
        n_past = 0
    else:
        conv_buf, s_gdn, s_ret, pool_k, pool_v, pool_kidx, page_table = past
        n_past = page_table.shape[1] * pool_k.shape[1]
    pos = n_past + jnp.arange(t, dtype=jnp.int32)

    conv_out, conv_new = _causal_conv(qkv_a, conv_buf, conv_w)
    qa, ka, va = [x.reshape(b, t, h_a, HEAD_DIM) for x in jnp.split(conv_out, 3, axis=-1)]
    g = -jnp.exp(a_log) * jax.nn.softplus(aa + dt_bias)
    oa, s_gdn_new = _gated_delta_rule(_l2_norm(qa), _l2_norm(ka), va, g, jax.nn.sigmoid(ba), s_gdn)
    oa = _rms_rows(oa) * gdn_g * jax.nn.silu(za.reshape(b, t, h_a, HEAD_DIM))

    qb = _rope(qb.reshape(b, t, h_b, HEAD_DIM), pos)
    kb = _rope(kb.reshape(b, t, kv_b, HEAD_DIM), pos)
    vb = vb.reshape(b, t, kv_b, HEAD_DIM)
    iq = _rope(iq.reshape(b, t, IDX_HEADS, idx_dim), pos)
    ik = _rope(ik.reshape(b, t, 1, idx_dim), pos)[:, :, 0]
    iw = iw * (IDX_HEADS ** -0.5 * idx_dim ** -0.5)
    if past is None:
        kidx_all = ik
        gather_kv = lambda idx: (_gather_rows(kb, idx), _gather_rows(vb, idx))
    else:
        past_kidx = pool_kidx[page_table].reshape(b, n_past, idx_dim)
        kidx_all = jnp.concatenate([past_kidx, ik], axis=1)
        gather_kv = lambda idx: (_paged_gather(pool_k, page_table, kb, idx),
                                 _paged_gather(pool_v, page_table, vb, idx))
    k_sel = max(1, min(TOPK_MAX, kidx_all.shape[1] // 4))
    ob = _sparse_attention(qb, iq, iw, kidx_all, pos, gather_kv, k_sel)

    qc = _rope(qc.reshape(b, t, h_c, HEAD_DIM), pos)
    kc = _rope(kc.reshape(b, t, h_c, HEAD_DIM), pos) * HEAD_DIM ** -0.5
    vc = vc.reshape(b, t, h_c, HEAD_DIM)
    log_gamma = jnp.log1p(-jnp.exp2(RET_GAMMA_BASE - jnp.arange(h_c, dtype=F32)))
    logd = jnp.broadcast_to(log_gamma, (b, t, h_c))
    oc, s_ret_new = _scan_retention(qc, kc, vc, logd, s_ret)
    oc = _rms_rows(oc) * jax.nn.silu(gc.reshape(b, t, h_c, HEAD_DIM))

    o = jnp.concatenate([oa.reshape(b, t, -1), ob.reshape(b, t, -1), oc.reshape(b, t, -1)], axis=-1)
    return o, (kb, vb, ik, s_gdn_new, conv_new, s_ret_new)


def _rope_tables(pos, head_dim, lanes):
    half = head_dim // 2
    inv = ROPE_THETA ** (-jnp.arange(half, dtype=F32) / half)
    ang = pos.astype(F32)[:, None] * inv[None, :]
    cos, sin = jnp.cos(ang), jnp.sin(ang)
    reps = lanes // head_dim
    return (jnp.tile(jnp.concatenate([cos, cos], axis=-1), (1, reps)),
            jnp.tile(jnp.concatenate([-sin, sin], axis=-1), (1, reps)))


def _rope_head(x, cos_t, sin_t):
    return x * cos_t + pltpu.roll(x, HEAD_DIM // 2, axis=1) * sin_t


def _rope_half_heads(x, cos_t, sin_t):
    lane = lax.broadcasted_iota(jnp.int32, x.shape, 1)
    fwd = pltpu.roll(x, 32, axis=1)
    bwd = pltpu.roll(x, LANE - 32, axis=1)
    return x * cos_t + jnp.where(lane % 64 < 32, bwd, fwd) * sin_t


def _dsa_prep_kernel(qb_ref, kb_ref, vb_ref, iq_ref, sm_ref, cos_ref, sin_ref, cosi_ref, sini_ref,
                     q_ref, kf_ref, kh_ref, vh_ref, iqlo_ref, iqhi_ref, kif_ref, ki2_ref):
    cos_t, sin_t = cos_ref[...], sin_ref[...]
    cosi_t, sini_t = cosi_ref[...], sini_ref[...]
    for h in range(qb_ref.shape[1] // HEAD_DIM):
        sl = slice(h * HEAD_DIM, (h + 1) * HEAD_DIM)
        q_ref[:, sl] = _rope_head(qb_ref[:, sl], cos_t, sin_t).astype(BF16)
    for h in range(kb_ref.shape[1] // HEAD_DIM):
        sl = slice(h * HEAD_DIM, (h + 1) * HEAD_DIM)
        kr = _rope_head(kb_ref[:, sl], cos_t, sin_t)
        kf_ref[:, sl] = kr
        kh_ref[:, sl] = kr.astype(BF16)
    vh_ref[...] = vb_ref[...].astype(BF16)
    lane = lax.broadcasted_iota(jnp.int32, cos_t.shape, 1)
    low = lane < 64
    for c in range(iq_ref.shape[1] // LANE):
        sl = slice(c * LANE, (c + 1) * LANE)
        r = _rope_half_heads(iq_ref[:, sl], cosi_t, sini_t)
        iqlo_ref[:, sl] = jnp.where(low, r, 0.0).astype(BF16)
        iqhi_ref[:, sl] = jnp.where(low, 0.0, r).astype(BF16)
    r = _rope_half_heads(sm_ref[...], cosi_t, sini_t)
    kif_ref[...] = r[:, :64]
    ki2_ref[...] = jnp.where(low, r, pltpu.roll(r, 64, axis=1)).astype(BF16)


def _col_spec(bt, width, off):
    assert off % width == 0, (off, width)
    return pl.BlockSpec((bt, width), lambda i: (i, off // width))


def _dsa_prep(p, lay, t, tables):
    m = p.shape[0]
    bt = min(256, t)
    nt = t // bt
    pos, wid = lay['pos'], lay['size']
    tab = pl.BlockSpec((bt, LANE), lambda i: (i % nt, 0))
    row = lambda w: pl.BlockSpec((bt, w), lambda i: (i, 0))
    wq, wk, wi = wid['qb'], wid['kb'], wid['iq']
    outs = pl.pallas_call(
        _dsa_prep_kernel,
        grid=(m // bt,),
        in_specs=[_col_spec(bt, wq, pos['qb']), _col_spec(bt, wk, pos['kb']), _col_spec(bt, wk, pos['vb']),
                  _col_spec(bt, wi, pos['iq']), _col_spec(bt, LANE, lay['small']), tab, tab, tab, tab],
        out_specs=[row(wq), row(wk), row(wk), row(wk), row(wi), row(wi), row(64), row(LANE)],
        out_shape=[jax.ShapeDtypeStruct((m, wq), BF16), jax.ShapeDtypeStruct((m, wk), F32),
                   jax.ShapeDtypeStruct((m, wk), BF16), jax.ShapeDtypeStruct((m, wk), BF16),
                   jax.ShapeDtypeStruct((m, wi), BF16), jax.ShapeDtypeStruct((m, wi), BF16),
                   jax.ShapeDtypeStruct((m, 64), F32), jax.ShapeDtypeStruct((m, LANE), BF16)],
        compiler_params=pltpu.CompilerParams(dimension_semantics=("arbitrary",)),
        name="dsa_prep",
    )(p, p, p, p, p, *tables)
    return outs


INT32_MIN = -2 ** 31


def _sortable_key(x):
    b = lax.bitcast_convert_type(x, jnp.int32)
    return b ^ ((b >> 31) & jnp.int32(0x7FFFFFFF))


def _kth_largest_key(keys_ref, k_sel):
    kf = jnp.float32(k_sel)

    def count_ge(t):
        return jnp.sum(jnp.where(keys_ref[...] >= t, 1.0, 0.0), axis=1, keepdims=True)

    rows = keys_ref.shape[0]
    zero = jnp.zeros((rows, 1), jnp.int32)
    t0 = jnp.where(count_ge(zero) >= kf, zero, jnp.int32(INT32_MIN))

    def body(i, t):
        cand = t | (jnp.int32(1) << (30 - i))
        return jnp.where(count_ge(cand) >= kf, cand, t)

    return lax.fori_loop(0, 31, body, t0)


def _tie_cut(count_tied_below, need, n_index):
    bits = max(1, (n_index - 1).bit_length())

    def body(i, base):
        cand = base + (jnp.int32(1) << (bits - 1 - i))
        return jnp.where(count_tied_below(cand) < need, cand, base)

    return lax.fori_loop(0, bits, body, jnp.zeros_like(need, dtype=jnp.int32)) + 1


def _chosen(keys, pos, thr, cut):
    bound = jnp.where(keys > thr, jnp.int32(2 ** 31 - 1), jnp.where(keys == thr, cut, jnp.int32(0)))
    return pos < bound


def _dsa_kernel(q_ref, k_ref, v_ref, iqlo_ref, iqhi_ref, ki2_ref, sm_ref, o_ref, score_ref, keys_ref, cut_ref,
                *, k_sel, kv_b, idx_dim, iw_lane, q_block0):
    qb = q_ref.shape[0]
    n_keys = k_ref.shape[0]
    group = q_ref.shape[1] // HEAD_DIM // kv_b
    q0 = (q_block0 + pl.program_id(1)) * qb
    ki2 = ki2_ref[...]
    w_scale = IDX_HEADS ** -0.5 * idx_dim ** -0.5
    heads_per_dot = 8
    score_ref[...] = jnp.zeros_like(score_ref)
    for gi in range(IDX_HEADS // heads_per_dot):
        rows = []
        for c in range(heads_per_dot // 2):
            sl = slice((gi * heads_per_dot // 2 + c) * LANE, (gi * heads_per_dot // 2 + c + 1) * LANE)
            rows.append(iqlo_ref[:, sl])
            rows.append(iqhi_ref[:, sl])
        lhs = jnp.concatenate(rows, axis=0)
        d = lax.dot_general(lhs, ki2, (((1,), (1,)), ((), ())), preferred_element_type=F32)
        acc = score_ref[...]
        for hh in range(heads_per_dot):
            lane = iw_lane + gi * heads_per_dot + hh
            w = sm_ref[:, lane:lane + 1] * w_scale
            acc = acc + jnp.maximum(d[hh * qb:(hh + 1) * qb], 0.0) * w
        score_ref[...] = acc

    qpos = q0 + lax.broadcasted_iota(jnp.int32, (qb, n_keys), 0)
    kpos = lax.broadcasted_iota(jnp.int32, (qb, n_keys), 1)
    visible = kpos <= qpos
    keys_ref[...] = jnp.where(visible, _sortable_key(score_ref[...] + 0.0), jnp.int32(INT32_MIN))
    thr = _kth_largest_key(keys_ref, k_sel)
    count = lambda mask: jnp.sum(jnp.where(mask, 1.0, 0.0), axis=1, keepdims=True)
    need = jnp.float32(k_sel) - count(keys_ref[...] > thr)
    surplus = jnp.max(count(keys_ref[...] == thr) - need)
    cut_ref[...] = jnp.full(cut_ref.shape, n_keys, jnp.int32)

    @pl.when(surplus > 0.5)
    def _():
        tied_pos = lambda: jnp.where(keys_ref[...] == thr, kpos, jnp.int32(n_keys))
        cut = _tie_cut(lambda c: count(tied_pos() < c), need, n_keys)
        cut_ref[...] = jnp.broadcast_to(cut, cut_ref.shape)

    neg = jnp.float32(-jnp.inf)
    keys = keys_ref[...]
    score_ref[...] = jnp.where(visible, jnp.where(_chosen(keys, kpos, thr, cut_ref[:, 0:1]), 0.0, neg), neg)

    scale = HEAD_DIM ** -0.5
    for n in range(kv_b):
        q3 = jnp.concatenate(
            [q_ref[:, (n * group + g) * HEAD_DIM:(n * group + g + 1) * HEAD_DIM] for g in range(group)], axis=0)
        kn = k_ref[:, n * HEAD_DIM:(n + 1) * HEAD_DIM]
        lg = lax.dot_general(q3, kn, (((1,), (1,)), ((), ())), preferred_element_type=F32) * scale
        lg = lg.reshape(group, qb, n_keys) + score_ref[...][None]
        mx = jnp.max(lg, axis=-1, keepdims=True)
        pr = jnp.exp(lg - mx)
        den = jnp.sum(pr, axis=-1, keepdims=True)
        o = jnp.dot(pr.reshape(group * qb, n_keys).astype(BF16), v_ref[:, n * HEAD_DIM:(n + 1) * HEAD_DIM],
                    preferred_element_type=F32)
        o = o / den.reshape(group * qb, 1)
        for g in range(group):
            sl = slice((n * group + g) * HEAD_DIM, (n * group + g + 1) * HEAD_DIM)
            o_ref[:, sl] = o[g * qb:(g + 1) * qb].astype(o_ref.dtype)


def _dsa_prompt(q, kh, vh, iqlo, iqhi, ki2, p, lay, b, t, kv_b, idx_dim):
    m, wq = q.shape
    wk, wi = kh.shape[1], iqlo.shape[1]
    qb = min(Q_BLOCK, t)
    nq = t // qb
    k_sel = max(1, min(TOPK_MAX, t // 4))
    n_bands = 4 if nq % 4 == 0 else 1
    per_band = nq // n_bands
    kh3, vh3, ki3 = (x.reshape(b, t, x.shape[1]) for x in (kh, vh, ki2))
    outs = []
    for band in range(n_bands):
        first = band * per_band
        n_keys = (first + per_band) * qb
        qrow = lambda w, first=first: pl.BlockSpec((qb, w), lambda bi, qi: (bi * nq + first + qi, 0))
        seq = lambda w, n_keys=n_keys: pl.BlockSpec((None, n_keys, w), lambda bi, qi: (bi, 0, 0))
        small = pl.BlockSpec((qb, LANE), lambda bi, qi, first=first: (bi * nq + first + qi, lay['small'] // LANE))
        blocks = (_nbytes((qb, wq), BF16) * 2 + 2 * _nbytes((n_keys, wk), BF16) + 2 * _nbytes((qb, wi), BF16)
                  + _nbytes((n_keys, LANE), BF16) + _nbytes((qb, LANE), F32))
        temps = 2 * _nbytes((qb, n_keys), F32) + 2 * _nbytes((8 * qb, n_keys), F32)
        outs.append(pl.pallas_call(
            functools.partial(_dsa_kernel, k_sel=k_sel, kv_b=kv_b, idx_dim=idx_dim,
                              iw_lane=lay['pos']['iw'] - lay['small'], q_block0=first),
            grid=(b, per_band),
            in_specs=[qrow(wq), seq(wk), seq(wk), qrow(wi), qrow(wi), seq(LANE), small],
            out_specs=pl.BlockSpec((None, qb, wq), lambda bi, qi: (bi, qi, 0)),
            out_shape=jax.ShapeDtypeStruct((b, per_band * qb, wq), BF16),
            scratch_shapes=[pltpu.VMEM((qb, n_keys), F32), pltpu.VMEM((qb, n_keys), jnp.int32),
                            pltpu.VMEM((qb, LANE), jnp.int32)],
            compiler_params=pltpu.CompilerParams(
                dimension_semantics=("arbitrary", "arbitrary"),
                vmem_limit_bytes=_vmem_limit(blocks, temps)),
            name="dsa",
        )(q, kh3, vh3, iqlo, iqhi, ki3, p))
    return jnp.concatenate(outs, axis=1).reshape(m, wq)


def _gdn_prep_kernel(x_ref, prev_ref, buf_ref, w_ref, q_ref, k_ref, v_ref, xp_ref, *, blocks_per_seq):
    bt = x_ref.shape[0]
    first = pl.program_id(0) % blocks_per_seq == 0
    xp_ref[0:8, :] = jnp.where(first, buf_ref[...], prev_ref[...])
    xp_ref[8:, :] = x_ref[...]
    n_heads = q_ref.shape[1] // HEAD_DIM
    for part, out_ref in enumerate((q_ref, k_ref, v_ref)):
        for h in range(n_heads):
            src = slice((part * n_heads + h) * HEAD_DIM, (part * n_heads + h + 1) * HEAD_DIM)
            y = xp_ref[8 - (CONV_W - 1):8 - (CONV_W - 1) + bt, src] * w_ref[0:1, src]
            for j in range(1, CONV_W):
                y = y + xp_ref[8 - (CONV_W - 1) + j:8 - (CONV_W - 1) + j + bt, src] * w_ref[j:j + 1, src]
            y = y * jax.nn.sigmoid(y)
            if part < 2:
                y = y * lax.rsqrt(jnp.sum(y * y, axis=-1, keepdims=True) + L2_EPS)
            if part == 0:
                y = y * HEAD_DIM ** -0.5
            out_ref[:, h * HEAD_DIM:(h + 1) * HEAD_DIM] = y


def _gdn_prep(p, lay, t, buf8, conv_w):
    m = p.shape[0]
    bt = min(256, t)
    nt = t // bt
    c = lay['size']['qkv_a']
    off = lay['pos']['qkv_a']
    assert off % c == 0
    cb = off // c
    wq = c // 3
    row = pl.BlockSpec((bt, wq), lambda i: (i, 0))
    blocks = _nbytes((bt, c), F32) + 2 * _nbytes((8, c), F32) + _nbytes((CONV_W, c), F32) + 3 * _nbytes((bt, wq), F32)
    return pl.pallas_call(
        functools.partial(_gdn_prep_kernel, blocks_per_seq=nt),
        grid=(m // bt,),
        in_specs=[pl.BlockSpec((bt, c), lambda i: (i, cb)),
                  pl.BlockSpec((8, c), lambda i: (jnp.maximum(i * (bt // 8) - 1, 0), cb)),
                  pl.BlockSpec((None, 8, c), lambda i: (i // nt, 0, 0)),
                  pl.BlockSpec((CONV_W, c), lambda i: (0, 0))],
        out_specs=[row, row, row],
        out_shape=[jax.ShapeDtypeStruct((m, wq), F32)] * 3,
        scratch_shapes=[pltpu.VMEM((bt + 8, c), F32)],
        compiler_params=pltpu.CompilerParams(
            dimension_semantics=("arbitrary",),
            vmem_limit_bytes=_vmem_limit(blocks, _nbytes((bt + 8, c), F32))),
        name="gdn_prep",
    )(p, p, buf8, conv_w)


def _dot_hi(a, b):
    ah, bh = a.astype(BF16), b.astype(BF16)
    al = (a - ah.astype(F32)).astype(BF16)
    bl = (b - bh.astype(F32)).astype(BF16)
    dot = functools.partial(jnp.dot, preferred_element_type=F32)
    return dot(ah, bh) + (dot(al, bh) + dot(ah, bl))


def _dot_nt(a, b):
    return lax.dot_general(a, b, (((1,), (1,)), ((), ())), preferred_element_type=F32)


def _dot_tn(a, b):
    return lax.dot_general(a, b, (((0,), (0,)), ((), ())), preferred_element_type=F32)


def _gdn_kernel(q_ref, k_ref, v_ref, g_ref, beta_ref, z_ref, gn_ref, s0_ref, o_ref, s_ref, *, chunk):
    t = q_ref.shape[0]
    hp = q_ref.shape[1] // HEAD_DIM
    c = chunk
    ii = lax.broadcasted_iota(jnp.int32, (c, c), 0)
    jj = lax.broadcasted_iota(jnp.int32, (c, c), 1)
    tril, strict, eye = ii >= jj, ii > jj, ii == jj

    @pl.when(pl.program_id(1) == 0)
    def _():
        s_ref[...] = s0_ref[...]

    gn = gn_ref[...]

    heads = range(hp)
    cols = [slice(j * HEAD_DIM, (j + 1) * HEAD_DIM) for j in heads]

    def body(i, carry):
        r0 = pl.multiple_of(i * c, c)
        g_row = [g_ref[j, pl.ds(i, 1), :] for j in heads]
        gc_col = [jnp.sum(jnp.where(tril, g_row[j], 0.0), axis=1, keepdims=True) for j in heads]
        gc_row = [jnp.sum(jnp.where(eye, gc_col[j], 0.0), axis=0, keepdims=True) for j in heads]
        beta_col = [jnp.sum(jnp.where(eye, beta_ref[j, pl.ds(i, 1), :], 0.0), axis=1, keepdims=True) for j in heads]
        decay = [jnp.where(tril, jnp.exp(jnp.where(tril, gc_col[j] - gc_row[j], 0.0)), 0.0) for j in heads]
        k = [k_ref[pl.ds(r0, c), cols[j]] for j in heads]
        kb = [k[j] * beta_col[j] for j in heads]
        kh = [k[j].astype(BF16) for j in heads]
        a = [jnp.where(strict, _dot_nt(kb[j].astype(BF16), kh[j]) * decay[j], 0.0) for j in heads]
        inv = [jnp.where(eye, 1.0, 0.0) - a[j] for j in heads]
        pw = a
        span = 2
        while span < c:
            pw = [_dot_hi(pw[j], pw[j]) for j in heads]
            inv = [inv[j] + _dot_hi(inv[j], pw[j]) for j in heads]
            span *= 2
        egc = [jnp.exp(gc_col[j]) for j in heads]
        u = [_dot_hi(inv[j], v_ref[pl.ds(r0, c), cols[j]] * beta_col[j]) for j in heads]
        w = [_dot_hi(inv[j], kb[j] * egc[j]) for j in heads]
        s = [s_ref[j] for j in heads]
        sh = [s[j].astype(BF16) for j in heads]
        vnh = [(u[j] - jnp.dot(w[j].astype(BF16), sh[j], preferred_element_type=F32)).astype(BF16) for j in heads]
        gl = [jnp.sum(g_row[j], axis=1, keepdims=True) for j in heads]
        for j in heads:
            s_ref[j] = s[j] * jnp.exp(gl[j]) + _dot_tn((k[j] * jnp.exp(gl[j] - gc_col[j])).astype(BF16), vnh[j])
        q = [q_ref[pl.ds(r0, c), cols[j]] for j in heads]
        intra = [_dot_nt(q[j].astype(BF16), kh[j]) * decay[j] for j in heads]
        o = [jnp.dot((q[j] * egc[j]).astype(BF16), sh[j], preferred_element_type=F32)
             + jnp.dot(intra[j].astype(BF16), vnh[j], preferred_element_type=F32) for j in heads]
        for j in heads:
            z = z_ref[pl.ds(r0, c), cols[j]]
            o_ref[pl.ds(r0, c), cols[j]] = (_rms_rows(o[j]) * gn * (z * jax.nn.sigmoid(z))).astype(o_ref.dtype)
        return carry

    lax.fori_loop(0, t // c, body, 0)


def _gdn_time_block(t):
    return min(4 * CHUNK, t)


def _gdn_gate_rows(x, b, t, h_a, c):
    tb = _gdn_time_block(t)
    x = jnp.transpose(x, (0, 2, 1)).reshape(b, h_a, t // tb, tb // c, c)
    return jnp.transpose(x, (0, 2, 1, 3, 4))


def _gdn(qn, kn, vv, g, beta, p, lay, gdn_g, s0, b, t):
    m, w = qn.shape
    h_a = w // HEAD_DIM
    c = g.shape[-1]
    tb = _gdn_time_block(t)
    nt = t // tb
    zoff = lay['pos']['za']
    assert zoff % w == 0
    seq = pl.BlockSpec((tb, w), lambda bi, ti: (bi * nt + ti, 0))
    gspec = pl.BlockSpec((None, None, h_a, tb // c, c), lambda bi, ti: (bi, ti, 0, 0, 0))
    sspec = pl.BlockSpec((None, h_a, HEAD_DIM, HEAD_DIM), lambda bi, ti: (bi, 0, 0, 0))
    blocks = 4 * _nbytes((tb, w), F32) + _nbytes((tb, w), BF16) + 2 * _nbytes((h_a, HEAD_DIM, HEAD_DIM), F32)
    return pl.pallas_call(
        functools.partial(_gdn_kernel, chunk=c),
        grid=(b, nt),
        in_specs=[seq, seq, seq, gspec, gspec,
                  pl.BlockSpec((tb, w), lambda bi, ti: (bi * nt + ti, zoff // w)),
                  pl.BlockSpec((1, HEAD_DIM), lambda bi, ti: (0, 0)), sspec],
        out_specs=[seq, sspec],
        out_shape=[jax.ShapeDtypeStruct((m, w), BF16), jax.ShapeDtypeStruct(s0.shape, F32)],
        compiler_params=pltpu.CompilerParams(
            dimension_semantics=("arbitrary", "arbitrary"),
            vmem_limit_bytes=_vmem_limit(blocks, 0)),
        name="gdn",
    )(qn, kn, vv, g, beta, p, gdn_g.reshape(1, HEAD_DIM), s0)


def _retention_kernel(q_ref, k_ref, v_ref, gate_ref, cos_ref, sin_ref, lg_ref, s0_ref, o_ref, s_ref, *, chunk):
    t = q_ref.shape[0]
    hp = q_ref.shape[1] // HEAD_DIM
    c = chunk
    ii = lax.broadcasted_iota(jnp.int32, (c, c), 0)
    jj = lax.broadcasted_iota(jnp.int32, (c, c), 1)
    tril = ii >= jj
    dist = (ii - jj).astype(F32)
    step = lax.broadcasted_iota(jnp.int32, (c, 1), 0).astype(F32)
    s_ref[...] = s0_ref[...]

    def body(i, carry):
        r0 = pl.multiple_of(i * c, c)
        cos_t = cos_ref[pl.ds(r0, c), :]
        sin_t = sin_ref[pl.ds(r0, c), :]
        for j in range(hp):
            sl = slice(j * HEAD_DIM, (j + 1) * HEAD_DIM)
            lg = lg_ref[j, 0:1, 0:1]
            decay = jnp.where(tril, jnp.exp(jnp.where(tril, dist * lg, 0.0)), 0.0)
            q = _rope_head(q_ref[pl.ds(r0, c), sl], cos_t, sin_t)
            k = _rope_head(k_ref[pl.ds(r0, c), sl], cos_t, sin_t) * HEAD_DIM ** -0.5
            vh = v_ref[pl.ds(r0, c), sl].astype(BF16)
            s = s_ref[j]
            intra = _dot_nt(q.astype(BF16), k.astype(BF16)) * decay
            o = (jnp.dot((q * jnp.exp((step + 1.0) * lg)).astype(BF16), s.astype(BF16), preferred_element_type=F32)
                 + jnp.dot(intra.astype(BF16), vh, preferred_element_type=F32))
            s_ref[j] = s * jnp.exp(c * lg) + _dot_tn((k * jnp.exp((c - 1.0 - step) * lg)).astype(BF16), vh)
            gt = gate_ref[pl.ds(r0, c), sl]
            o_ref[pl.ds(r0, c), sl] = (_rms_rows(o) * (gt * jax.nn.sigmoid(gt))).astype(o_ref.dtype)
        return carry

    lax.fori_loop(0, t // c, body, 0)


def _retention(p, lay, tables, s0, b, t, hp, chunk):
    m = p.shape[0]
    h_c = s0.shape[1]
    assert h_c % hp == 0 and t % chunk == 0
    wb = hp * HEAD_DIM
    pos = lay['pos']
    log_gamma = jnp.log1p(-jnp.exp2(RET_GAMMA_BASE - jnp.arange(h_c, dtype=F32)))
    lg = jnp.broadcast_to(log_gamma[:, None, None], (h_c, 8, LANE))

    def col(nm):
        assert pos[nm] % wb == 0
        return pl.BlockSpec((t, wb), lambda bi, hi: (bi, pos[nm] // wb + hi))

    tab = pl.BlockSpec((t, LANE), lambda bi, hi: (0, 0))
    sspec = pl.BlockSpec((None, hp, HEAD_DIM, HEAD_DIM), lambda bi, hi: (bi, hi, 0, 0))
    blocks = (4 * _nbytes((t, wb), F32) + 2 * _nbytes((t, LANE), F32) + _nbytes((t, wb), BF16)
              + 2 * _nbytes((hp, HEAD_DIM, HEAD_DIM), F32))
    return pl.pallas_call(
        functools.partial(_retention_kernel, chunk=chunk),
        grid=(b, h_c // hp),
        in_specs=[col('qc'), col('kc'), col('vc'), col('gc'), tab, tab,
                  pl.BlockSpec((hp, 8, LANE), lambda bi, hi: (hi, 0, 0)), sspec],
        out_specs=[pl.BlockSpec((t, wb), lambda bi, hi: (bi, hi)), sspec],
        out_shape=[jax.ShapeDtypeStruct((m, h_c * HEAD_DIM), BF16), jax.ShapeDtypeStruct(s0.shape, F32)],
        compiler_params=pltpu.CompilerParams(
            dimension_semantics=("arbitrary", "arbitrary"),
            vmem_limit_bytes=_vmem_limit(blocks, 0)),
        name="retention",
    )(p, p, p, p, tables[0], tables[1], lg, s0)


PAGES_PER_STEP = 4
MASKED = -1e30


def _stack_idx_heads(iqlo_ref, iqhi_ref):
    rows = []
    for c in range(iqlo_ref.shape[1] // LANE):
        sl = slice(c * LANE, (c + 1) * LANE)
        rows.append(iqlo_ref[:, sl])
        rows.append(iqhi_ref[:, sl])
    return jnp.concatenate(rows, axis=0).astype(BF16)


def _idx_scores(lhs, ki2, sm_ref, rows, iw_lane, w_scale):
    d = _dot_nt(lhs, ki2.astype(BF16))
    acc = jnp.zeros((rows, ki2.shape[0]), F32)
    for h in range(IDX_HEADS):
        w = sm_ref[:, iw_lane + h:iw_lane + h + 1] * w_scale
        acc = acc + jnp.maximum(d[h * rows:(h + 1) * rows], 0.0) * w
    return acc


def _dsa_s_scores_kernel(pt_ref, iqlo_ref, iqhi_ref, sm_ref, *rest, idx_dim, iw_lane):
    page_refs, score_ref = rest[:-1], rest[-1]
    rows = iqlo_ref.shape[0]
    page = page_refs[0].shape[0]
    lhs = _stack_idx_heads(iqlo_ref, iqhi_ref)
    w_scale = IDX_HEADS ** -0.5 * idx_dim ** -0.5
    for i, pr in enumerate(page_refs):
        score_ref[:, i * page:(i + 1) * page] = _idx_scores(lhs, pr[...], sm_ref, rows, iw_lane, w_scale)


def _dsa_s_scores(page_table, iqlo, iqhi, sm, pool_ki2, layer, t, idx_dim, iw_lane, pages_per_step):
    b, n_pages = page_table.shape
    page = pool_ki2.shape[2]
    wi = iqlo.shape[1]
    assert n_pages % pages_per_step == 0
    row = lambda w: pl.BlockSpec((t, w), lambda bi, gi, pt: (bi, 0))
    pages = [pl.BlockSpec((None, None, page, LANE),
                          lambda bi, gi, pt, i=i: (layer, pt[bi, gi * pages_per_step + i], 0, 0))
             for i in range(pages_per_step)]
    return pl.pallas_call(
        functools.partial(_dsa_s_scores_kernel, idx_dim=idx_dim, iw_lane=iw_lane),
        grid_spec=pltpu.PrefetchScalarGridSpec(
            num_scalar_prefetch=1,
            grid=(b, n_pages // pages_per_step),
            in_specs=[row(wi), row(wi), row(LANE)] + pages,
            out_specs=pl.BlockSpec((None, t, pages_per_step * page), lambda bi, gi, pt: (bi, 0, gi)),
        ),
        out_shape=jax.ShapeDtypeStruct((b, t, n_pages * page), F32),
        compiler_params=pltpu.CompilerParams(dimension_semantics=("arbitrary", "arbitrary")),
        name="dsa_s_scores",
    )(page_table, iqlo, iqhi, sm, *([pool_ki2] * pages_per_step))


def _dsa_s_attn_kernel(pt_ref, sp_ref, sn_ref, q_ref, kn_ref, vn_ref, *rest, k_sel, kv_b):
    n_pg = (len(rest) - 8) // 2
    k_refs, v_refs = rest[:n_pg], rest[n_pg:2 * n_pg]
    o_ref, keysp_ref, keysn_ref, thr_ref, cut_ref, m_ref, l_ref, acc_ref = rest[2 * n_pg:]
    t = q_ref.shape[0]
    page = k_refs[0].shape[0]
    n_past = sp_ref.shape[1]
    n_heads = q_ref.shape[1] // HEAD_DIM
    group = n_heads // kv_b
    gi = pl.program_id(1)
    scale = HEAD_DIM ** -0.5
    qi = lax.broadcasted_iota(jnp.int32, (t, page), 0)
    kj = lax.broadcasted_iota(jnp.int32, (t, page), 1)
    new_visible = kj <= qi

    @pl.when(gi == 0)
    def _():
        keysp_ref[...] = _sortable_key(sp_ref[...] + 0.0)
        keysn_ref[...] = jnp.where(new_visible, _sortable_key(sn_ref[...] + 0.0), jnp.int32(INT32_MIN))
        m_ref[...] = jnp.full_like(m_ref, MASKED)
        l_ref[...] = jnp.zeros_like(l_ref)
        acc_ref[...] = jnp.zeros_like(acc_ref)

        kf = jnp.float32(k_sel)

        def count_ge(thr):
            return (jnp.sum(jnp.where(keysp_ref[...] >= thr, 1.0, 0.0), axis=1, keepdims=True)
                    + jnp.sum(jnp.where(keysn_ref[...] >= thr, 1.0, 0.0), axis=1, keepdims=True))

        zero = jnp.zeros((t, 1), jnp.int32)
        thr0 = jnp.where(count_ge(zero) >= kf, zero, jnp.int32(INT32_MIN))

        def search(i, thr):
            cand = thr | (jnp.int32(1) << (30 - i))
            return jnp.where(count_ge(cand) >= kf, cand, thr)

        thr = lax.fori_loop(0, 31, search, thr0)
        thr_ref[...] = jnp.broadcast_to(thr, thr_ref.shape)

        count = lambda mask: jnp.sum(jnp.where(mask, 1.0, 0.0), axis=1, keepdims=True)
        need = kf - count(keysp_ref[...] > thr) - count(keysn_ref[...] > thr)
        n_index = n_past + page
        pos_p = lax.broadcasted_iota(jnp.int32, keysp_ref.shape, 1)
        tied_p = lambda: jnp.where(keysp_ref[...] == thr, pos_p, jnp.int32(n_index))
        tied_n = lambda: jnp.where(keysn_ref[...] == thr, n_past + kj, jnp.int32(n_index))
        cut = _tie_cut(lambda c: count(tied_p() < c) + count(tied_n() < c), need, n_index)
        cut_ref[...] = jnp.broadcast_to(cut, cut_ref.shape)

    thr = thr_ref[:, 0:1]
    cut = cut_ref[:, 0:1]

    chosen = lambda keys, pos: _chosen(keys, pos, thr, cut)

    q = q_ref[...]
    q3 = [jnp.concatenate([q[:, (n * group + g) * HEAD_DIM:(n * group + g + 1) * HEAD_DIM] for g in range(group)],
                          axis=0).astype(BF16) for n in range(kv_b)]

    def update(k_heads, v_heads, bias):
        n_keys = bias.shape[1]
        for n in range(kv_b):
            rows = slice(n * group * t, (n + 1) * group * t)
            lg = _dot_nt(q3[n], k_heads[n].astype(BF16)) * scale
            lg = (lg.reshape(group, t, n_keys) + bias[None]).reshape(group * t, n_keys)
            m_old = m_ref[rows, 0:1]
            m_new = jnp.maximum(m_old, jnp.max(lg, axis=-1, keepdims=True))
            alpha = jnp.exp(m_old - m_new)
            pr = jnp.exp(lg - m_new)
            l_ref[rows, :] = alpha * l_ref[rows, :] + jnp.sum(pr, axis=-1, keepdims=True)
            acc_ref[rows, :] = alpha * acc_ref[rows, :] + jnp.dot(pr.astype(BF16), v_heads[n].astype(BF16),
                                                                  preferred_element_type=F32)
            m_ref[rows, :] = jnp.broadcast_to(m_new, (group * t, LANE))

    biases = []
    for i in range(n_pg):
        start = pl.multiple_of((gi * n_pg + i) * page, page)
        biases.append(jnp.where(chosen(keysp_ref[:, pl.ds(start, page)], start + kj), 0.0, MASKED))
    update([jnp.concatenate([r[:, n, :] for r in k_refs], axis=0) for n in range(kv_b)],
           [jnp.concatenate([r[:, n, :] for r in v_refs], axis=0) for n in range(kv_b)],
           jnp.concatenate(biases, axis=1))

    @pl.when(gi == pl.num_programs(1) - 1)
    def _():
        bias_new = jnp.where(new_visible, jnp.where(chosen(keysn_ref[...], n_past + kj), 0.0, MASKED), MASKED)
        heads_of = lambda ref: [ref[:, n * HEAD_DIM:(n + 1) * HEAD_DIM] for n in range(kv_b)]
        update(heads_of(kn_ref), heads_of(vn_ref), bias_new)
        for h in range(n_heads):
            rows = slice(h * t, (h + 1) * t)
            o_ref[:, h * HEAD_DIM:(h + 1) * HEAD_DIM] = acc_ref[rows, :] / l_ref[rows, 0:1]


def _dsa_s_attn(page_table, scores_past, scores_new, q, k_new, v_new, pool_k, pool_v, layer, t, kv_b, k_sel,
                pages_per_step):
    b, n_pages = page_table.shape
    page = pool_k.shape[2]
    wq = q.shape[1]
    wk = k_new.shape[2]
    n_past = n_pages * page
    pk, pv = pool_k, pool_v
    pages = [pl.BlockSpec((None, None, page, kv_b, HEAD_DIM),
                          lambda bi, gi, pt, i=i: (layer, pt[bi, gi * pages_per_step + i], 0, 0, 0))
             for i in range(pages_per_step)]
    per_seq = lambda rows, w: pl.BlockSpec((None, rows, w), lambda bi, gi, pt: (bi, 0, 0))
    n_heads = wq // HEAD_DIM
    return pl.pallas_call(
        functools.partial(_dsa_s_attn_kernel, k_sel=k_sel, kv_b=kv_b),
        grid_spec=pltpu.PrefetchScalarGridSpec(
            num_scalar_prefetch=1,
            grid=(b, n_pages // pages_per_step),
            in_specs=[per_seq(t, n_past), per_seq(t, page), pl.BlockSpec((t, wq), lambda bi, gi, pt: (bi, 0)),
                      per_seq(page, wk), per_seq(page, wk)] + pages + pages,
            out_specs=pl.BlockSpec((t, wq), lambda bi, gi, pt: (bi, 0)),
            scratch_shapes=[pltpu.VMEM((t, n_past), jnp.int32), pltpu.VMEM((t, page), jnp.int32),
                            pltpu.VMEM((t, LANE), jnp.int32), pltpu.VMEM((t, LANE), jnp.int32),
                            pltpu.VMEM((n_heads * t, LANE), F32), pltpu.VMEM((n_heads * t, LANE), F32),
                            pltpu.VMEM((n_heads * t, HEAD_DIM), F32)],
        ),
        out_shape=jax.ShapeDtypeStruct((b * t, wq), F32),
        compiler_params=pltpu.CompilerParams(dimension_semantics=("arbitrary", "arbitrary")),
        name="dsa_s_attn",
    )(page_table, scores_past, scores_new, q, k_new, v_new, *([pk] * pages_per_step), *([pv] * pages_per_step))


def _in_proj_layout(d_model, kv_b, idx_dim):
    d_mix = d_model
    h_a = (3 * d_mix) // (8 * HEAD_DIM)
    h_b = h_a
    h_c = d_mix // HEAD_DIM - h_a - h_b
    sizes = (3 * h_a * HEAD_DIM, h_a * HEAD_DIM, h_a, h_a,
             h_b * HEAD_DIM, kv_b * HEAD_DIM, kv_b * HEAD_DIM, IDX_HEADS * idx_dim, idx_dim, IDX_HEADS,
             h_c * HEAD_DIM, h_c * HEAD_DIM, h_c * HEAD_DIM, h_c * HEAD_DIM)
    names = ('qkv_a', 'za', 'aa', 'ba', 'qb', 'kb', 'vb', 'iq', 'ik', 'iw', 'qc', 'kc', 'vc', 'gc')
    src, size, off = {}, {}, 0
    for nm, s in zip(names, sizes):
        src[nm] = off
        size[nm] = s
        off += s
    wide = ('iq', 'qc', 'kc', 'vc', 'gc', 'qb', 'za', 'qkv_a', 'kb', 'vb')
    narrow = ('ik', 'iw', 'aa', 'ba')
    pos, cur = {}, 0
    for nm in wide:
        cur = -(-cur // size[nm]) * size[nm]
        pos[nm] = cur
        cur += size[nm]
    small = -(-cur // LANE) * LANE
    cur = small
    for nm in narrow:
        pos[nm] = cur
        cur += size[nm]
    assert cur <= small + LANE
    lay = dict(src=src, size=size, pos=pos, small=small, order=wide + narrow, n_src=off)
    return (h_a, h_b, h_c), lay


def _rearrange_w_in(w_in, lay, bn):
    cols, cur = [], 0
    for nm in lay['order']:
        gap = lay['pos'][nm] - cur
        if gap:
            cols.append(jnp.zeros(w_in.shape[:-1] + (gap,), w_in.dtype))
        cols.append(w_in[..., lay['src'][nm]:lay['src'][nm] + lay['size'][nm]])
        cur = lay['pos'][nm] + lay['size'][nm]
    n_pad = -(-(lay['small'] + LANE) // bn) * bn
    cols.append(jnp.zeros(w_in.shape[:-1] + (n_pad - cur,), w_in.dtype))
    return jnp.concatenate(cols, axis=-1).astype(BF16)


def _proj_part(p, lay, nm, b, t):
    return p[:, lay['pos'][nm]:lay['pos'][nm] + lay['size'][nm]].reshape(b, t, lay['size'][nm])


def _prompt_mixer(p, lay, heads, kv_b, idx_dim, b, t, conv_w, a_log, dt_bias, gdn_g):
    h_a, h_b, h_c = heads
    pos_ids = jnp.arange(t, dtype=jnp.int32)
    tab_head = _rope_tables(pos_ids, HEAD_DIM, LANE)
    tab_idx = _rope_tables(pos_ids, idx_dim, LANE)

    q, kf, kh, vh, iqlo, iqhi, kif, ki2 = _dsa_prep(p, lay, t, tab_head + tab_idx)
    ob = _dsa_prompt(q, kh, vh, iqlo, iqhi, ki2, p, lay, b, t, kv_b, idx_dim)

    n_conv = lay['size']['qkv_a']
    qn, kn, vv = _gdn_prep(p, lay, t, jnp.zeros((b, 8, n_conv), F32), conv_w)
    c = min(CHUNK, t)
    aa = _proj_part(p, lay, 'aa', b, t)
    ba = _proj_part(p, lay, 'ba', b, t)
    g = -jnp.exp(a_log) * jax.nn.softplus(aa + dt_bias)
    oa, s_gdn = _gdn(qn, kn, vv, _gdn_gate_rows(g, b, t, h_a, c), _gdn_gate_rows(jax.nn.sigmoid(ba), b, t, h_a, c),
                     p, lay, gdn_g, jnp.zeros((b, h_a, HEAD_DIM, HEAD_DIM), F32), b, t)

    oc, s_ret = _retention(p, lay, tab_head, jnp.zeros((b, h_c, HEAD_DIM, HEAD_DIM), F32), b, t,
                           hp=2, chunk=min(2 * CHUNK, t))

    o = jnp.concatenate([oa, ob, oc], axis=-1)
    vb = _proj_part(p, lay, 'vb', b, t).reshape(b, t, kv_b, HEAD_DIM)
    conv_new = _proj_part(p, lay, 'qkv_a', b, t)[:, t - (CONV_W - 1):]
    state = (kf.reshape(b, t, kv_b, HEAD_DIM), vb, kif.reshape(b, t, idx_dim), s_gdn, conv_new, s_ret)
    return o, state


def _sample_mixer(p, lay, heads, kv_b, idx_dim, b, t, conv_w, a_log, dt_bias, gdn_g, past, layer):
    h_a, h_b, h_c = heads
    cache_k, cache_v, pool_ki2, state_gdn, state_gdn_conv, state_ret, page_table = past
    page = cache_k.shape[2]
    n_past = page_table.shape[1] * page
    m = b * t
    assert t <= CHUNK and t >= CONV_W - 1
    pos_ids = n_past + jnp.arange(t, dtype=jnp.int32)
    per_row = lambda tabs: tuple(jnp.tile(x, (b, 1)) for x in tabs)
    tables = per_row(_rope_tables(pos_ids, HEAD_DIM, LANE)) + per_row(_rope_tables(pos_ids, idx_dim, LANE))
    q, kf, kh, vh, iqlo, iqhi, kif, ki2 = _dsa_prep(p, lay, m, tables)

    f32 = lambda x: x.astype(F32)
    to_page = lambda x: jnp.pad(x.reshape(b, t, -1), ((0, 0), (0, page - t), (0, 0)))
    sm = p[:, lay['small']:lay['small'] + LANE]
    iw_lane = lay['pos']['iw'] - lay['small']
    scores_past = _dsa_s_scores(page_table, f32(iqlo), f32(iqhi), sm, pool_ki2, layer, t, idx_dim, iw_lane,
                                PAGES_PER_STEP)
    own = jnp.arange(b, dtype=jnp.int32).reshape(b, 1)
    scores_new = _dsa_s_scores(own, f32(iqlo), f32(iqhi), sm, to_page(f32(ki2))[None], 0, t, idx_dim, iw_lane, 1)
    k_sel = max(1, min(TOPK_MAX, (n_past + t) // 4))
    vb = _proj_part(p, lay, 'vb', b, t)
    ob = _dsa_s_attn(page_table, scores_past, scores_new, f32(q), to_page(kf), to_page(vb), cache_k, cache_v,
                     layer, t, kv_b, k_sel, PAGES_PER_STEP)

    tp = CHUNK
    lead = tp - t
    padl = lambda x: jnp.pad(x.reshape(b, t, -1), ((0, 0), (lead, 0), (0, 0)))
    rows = lambda x: padl(x).reshape(b * tp, -1)
    p_pad = rows(p)
    buf8 = jnp.pad(state_gdn_conv[layer], ((0, 0), (8 - (CONV_W - 1), 0), (0, 0)))
    qn, kn, vv = _gdn_prep(p, lay, t, buf8, conv_w)
    aa = _proj_part(p, lay, 'aa', b, t)
    ba = _proj_part(p, lay, 'ba', b, t)
    g = -jnp.exp(a_log) * jax.nn.softplus(aa + dt_bias)
    oa, s_gdn = _gdn(rows(qn), rows(kn), rows(vv), _gdn_gate_rows(padl(g), b, tp, h_a, tp),
                     _gdn_gate_rows(padl(jax.nn.sigmoid(ba)), b, tp, h_a, tp), p_pad, lay, gdn_g,
                     state_gdn[layer], b, tp)
    log_gamma = jnp.log1p(-jnp.exp2(RET_GAMMA_BASE - jnp.arange(h_c, dtype=F32)))
    s_ret0 = state_ret[layer] * jnp.exp(-lead * log_gamma)[None, :, None, None]
    tab_pad = _rope_tables(n_past - lead + jnp.arange(tp, dtype=jnp.int32), HEAD_DIM, LANE)
    oc, s_ret = _retention(p_pad, lay, tab_pad, s_ret0, b, tp, hp=2, chunk=tp)
    tail = lambda x: x.reshape(b, tp, -1)[:, lead:].reshape(m, -1)

    o = jnp.concatenate([tail(oa), ob.astype(BF16), tail(oc)], axis=-1)
    conv_new = _proj_part(p, lay, 'qkv_a', b, t)[:, t - (CONV_W - 1):]
    state = (kf.reshape(b, t, kv_b, HEAD_DIM), vb.reshape(b, t, kv_b, HEAD_DIM), kif.reshape(b, t, idx_dim),
             s_gdn, conv_new, s_ret)
    return o, state


def _forward(xs, mods, weights, past, cfg):
    (norm_pre, norm_post, w_in_r, gdn_conv_w, gdn_a_log, gdn_dt_bias, gdn_norm,
     w_out, ffn_w_gate, ffn_w_up, ffn_w_down) = weights
    heads, lay, kv_b, idx_dim = cfg
    depth = w_out.shape[0]
    d = xs[0].shape[-1]
    bt = [x.shape[:2] for x in xs]
    m = bt[0][0] * bt[0][1]
    bm = min(1024, m)
    bm_down = min(256, m)
    hs = [x.reshape(-1, d) for x in xs]
    vec = lambda g, l, s, j: mods[g][l, :, s, j].reshape(bt[g][0], 1, d)
    gpre = lambda l, s: norm_pre[l, s].reshape(1, d)
    gpost = lambda l, s: norm_post[l, s].reshape(1, d)

    def resid(ys, l, s, nxt, res_w):
        out = [_resid(hs[g], ys[g], vec(g, l, s, 2), gpost(l, s),
                      None if nxt is None else (gpre(*nxt), vec(g, *nxt, 1), vec(g, *nxt, 0)), bt[g][1], res_w)
               for g in range(2)]
        return [o[0] for o in out], [o[1] for o in out]

    def ffn(us, l, j):
        hf = _ffn_in(us[0], us[1], ffn_w_gate, ffn_w_up, (l, j), bm=bm, bn=MXU_COLS)
        return _matmul(hf[0], hf[1], ffn_w_down, (l, j), bm=bm_down, bn=2 * MXU_COLS, out_dtype=F32,
                       name="ffn_down", single_buffer_w=True)

    states = ([], [])
    us = [_norm_mod(hs[g], gpre(0, 0), vec(g, 0, 0, 1), vec(g, 0, 0, 0), bt[g][1]) for g in range(2)]
    for l in range(depth):
        hs, us = resid(ffn(us, l, 0), l, 0, (l, 1), FFN_RES)

        ps = _matmul(us[0], us[1], w_in_r, (l,), bm=bm, bn=2 * MXU_COLS, out_dtype=F32, name="in_proj")
        mix = (gdn_conv_w[l], gdn_a_log[l], gdn_dt_bias[l], gdn_norm[l])
        o_p, st_p = _prompt_mixer(ps[0], lay, heads, kv_b, idx_dim, *bt[0], *mix)
        o_s, st_s = _sample_mixer(ps[1], lay, heads, kv_b, idx_dim, *bt[1], *mix, past, l)
        states[0].append(st_p)
        states[1].append(st_s)
        ys = _matmul(o_p, o_s, w_out, (l,), bm=bm, bn=MXU_COLS, out_dtype=F32, name="out_proj")
        hs, us = resid(ys, l, 1, (l, 2), 1.0)

        hs, us = resid(ffn(us, l, 1), l, 2, (l + 1, 0) if l + 1 < depth else None, FFN_RES)
    new_state = [[jnp.stack([s[i] for s in states[g]]) for i in range(6)] for g in range(2)]
    return [hs[g].reshape(bt[g] + (d,)) for g in range(2)], new_state


def kernel(x_prompt, x_sample, cache_k, cache_v, cache_kidx, state_gdn, state_gdn_conv, state_ret, page_table,
           c_prompt, c_sample, w_ada, b_ada, norm_pre, norm_post, w_in, gdn_conv_w, gdn_a_log, gdn_dt_bias,
           gdn_norm, w_out, ffn_w_gate, ffn_w_up, ffn_w_down):
    d = x_prompt.shape[-1]
    depth = w_ada.shape[0]
    kv_b = cache_k.shape[3]
    idx_dim = cache_kidx.shape[-1]
    heads, lay = _in_proj_layout(d, kv_b, idx_dim)
    assert lay['n_src'] == w_in.shape[-1] and 2 * idx_dim == LANE
    bp, bs = c_prompt.shape[0], c_sample.shape[0]

    rows = -(-(bp + bs) // 8) * 8
    c_rows = jnp.concatenate([c_prompt, c_sample, jnp.zeros((rows - bp - bs, d), F32)], axis=0)
    mod = _adaln(c_rows, w_ada, b_ada).reshape(depth, rows, N_SUB, 3, d)

    w_in_r = _rearrange_w_in(w_in, lay, 2 * MXU_COLS)
    weights = (norm_pre, norm_post, w_in_r, gdn_conv_w, gdn_a_log, gdn_dt_bias, gdn_norm,
               w_out, ffn_w_gate, ffn_w_up, ffn_w_down)
    cfg = (heads, lay, kv_b, idx_dim)
    pool_ki2 = jnp.concatenate([cache_kidx, cache_kidx], axis=-1)
    past = (cache_k, cache_v, pool_ki2, state_gdn, state_gdn_conv, state_ret, page_table)
    (y_p, y_s), (st_p, st_s) = _forward((x_prompt, x_sample), (mod[:, :bp], mod[:, bp:bp + bs]), weights, past, cfg)
    return (y_p, y_s, *st_p, *st_s)
```

```python
import functools

import jax
import jax.numpy as jnp
from jax import lax
from jax.experimental import pallas as pl
from jax.experimental.pallas import tpu as pltpu

F32 = jnp.float32
BF16 = jnp.bfloat16

HEAD_DIM = 128
IDX_HEADS = 32
TOPK_MAX = 256
CONV_W = 4
FFN_RES = 0.5
N_SUB = 3
CHUNK = 64
Q_BLOCK = 128
ROPE_THETA = 10000.0
RET_GAMMA_BASE = -5.0
EPS = 1e-6
L2_EPS = 1e-6

V7X_VMEM_BYTES = 64 * 1024 * 1024
V7X_VMEM_BUDGET = V7X_VMEM_BYTES - 8 * 1024 * 1024
LANE = 128
MXU_COLS = 256


def _vmem_limit(block_bytes, scratch_bytes):
    need = 2 * block_bytes + scratch_bytes + 8 * 1024 * 1024
    assert need <= V7X_VMEM_BUDGET, need
    return need


def _nbytes(shape, dtype):
    n = 1
    for s in shape:
        n *= s
    return n * jnp.dtype(dtype).itemsize


def _adaln_kernel(c_ref, w_ref, b_ref, o_ref):
    c = c_ref[...]
    cs = (c * jax.nn.sigmoid(c)).astype(BF16)
    w = w_ref[...].astype(BF16)
    o_ref[...] = jnp.dot(cs, w, preferred_element_type=F32) + b_ref[...]


def _adaln(c_rows, w_ada, b_ada):
    depth, d, n = w_ada.shape
    rows = c_rows.shape[0]
    bn = 1024
    assert n % bn == 0
    blocks = _nbytes((rows, d), F32) + _nbytes((d, bn), F32) + _nbytes((1, bn), F32) + _nbytes((rows, bn), F32)
    return pl.pallas_call(
        _adaln_kernel,
        grid=(depth, n // bn),
        in_specs=[
            pl.BlockSpec((rows, d), lambda l, j: (0, 0)),
            pl.BlockSpec((None, d, bn), lambda l, j: (l, 0, j)),
            pl.BlockSpec((None, 1, bn), lambda l, j: (l, 0, j)),
        ],
        out_specs=pl.BlockSpec((None, rows, bn), lambda l, j: (l, 0, j)),
        out_shape=jax.ShapeDtypeStruct((depth, rows, n), F32),
        compiler_params=pltpu.CompilerParams(
            dimension_semantics=("arbitrary", "arbitrary"),
            vmem_limit_bytes=_vmem_limit(blocks, _nbytes((d, bn), BF16))),
        name="adaln",
    )(c_rows, w_ada, b_ada.reshape(depth, 1, n))


def _mm_kernel(x_ref, xs_ref, w_ref, o_ref, os_ref, *scratch, cast_w):
    first = pl.program_id(1) == 0
    if cast_w:
        (wb_ref,) = scratch

        @pl.when(first)
        def _():
            wb_ref[...] = w_ref[...].astype(BF16)

        w_src = wb_ref
    else:
        w_src = w_ref

    @pl.when(first)
    def _():
        os_ref[...] = jnp.dot(xs_ref[...], w_src[...], preferred_element_type=F32).astype(os_ref.dtype)

    o_ref[...] = jnp.dot(x_ref[...], w_src[...], preferred_element_type=F32).astype(o_ref.dtype)


def _matmul(x, xs, w, w_index, *, bm, bn, out_dtype, name, single_buffer_w=False):
    m, k = x.shape
    ms = xs.shape[0]
    n = w.shape[-1]
    assert w.shape[-2] == k and m % bm == 0 and n % bn == 0, (x.shape, w.shape, bm, bn)
    cast_w = w.dtype != BF16
    lead = len(w_index)
    scratch = [pltpu.VMEM((k, bn), BF16)] if cast_w else []
    w_bytes = _nbytes((k, bn), w.dtype)
    blocks = (_nbytes((bm, k), x.dtype) + _nbytes((bm, bn), out_dtype) + _nbytes((ms, bn), out_dtype)
              + (0 if single_buffer_w else w_bytes))
    fixed = ((_nbytes((k, bn), BF16) if cast_w else 0) + (w_bytes if single_buffer_w else 0)
             + _nbytes((ms, k), xs.dtype))
    w_mode = dict(pipeline_mode=pl.Buffered(1)) if single_buffer_w else {}
    return pl.pallas_call(
        functools.partial(_mm_kernel, cast_w=cast_w),
        grid=(n // bn, m // bm),
        in_specs=[
            pl.BlockSpec((bm, k), lambda j, i: (i, 0)),
            pl.BlockSpec((ms, k), lambda j, i: (0, 0), pipeline_mode=pl.Buffered(1)),
            pl.BlockSpec((None,) * lead + (k, bn), lambda j, i: tuple(w_index) + (0, j), **w_mode),
        ],
        out_specs=[pl.BlockSpec((bm, bn), lambda j, i: (i, j)), pl.BlockSpec((ms, bn), lambda j, i: (0, j))],
        out_shape=[jax.ShapeDtypeStruct((m, n), out_dtype), jax.ShapeDtypeStruct((ms, n), out_dtype)],
        scratch_shapes=scratch,
        compiler_params=pltpu.CompilerParams(
            dimension_semantics=("arbitrary", "arbitrary"),
            vmem_limit_bytes=_vmem_limit(blocks, fixed)),
        name=name,
    )(x, xs, w)


def _ffn_in_kernel(x_ref, xs_ref, wg_ref, wu_ref, o_ref, os_ref, wgb_ref, wub_ref):
    def swiglu(x):
        g = jnp.dot(x, wgb_ref[...], preferred_element_type=F32)
        u = jnp.dot(x, wub_ref[...], preferred_element_type=F32)
        return g * jax.nn.sigmoid(g) * u

    @pl.when(pl.program_id(1) == 0)
    def _():
        wgb_ref[...] = wg_ref[...].astype(BF16)
        wub_ref[...] = wu_ref[...].astype(BF16)
        os_ref[...] = swiglu(xs_ref[...]).astype(os_ref.dtype)

    o_ref[...] = swiglu(x_ref[...]).astype(o_ref.dtype)


def _ffn_in(x, xs, wg, wu, w_index, *, bm, bn):
    m, k = x.shape
    ms = xs.shape[0]
    n = wg.shape[-1]
    assert m % bm == 0 and n % bn == 0
    lead = len(w_index)
    wspec = pl.BlockSpec((None,) * lead + (k, bn), lambda j, i: tuple(w_index) + (0, j))
    blocks = _nbytes((bm, k), BF16) + 2 * _nbytes((k, bn), F32) + _nbytes((bm, bn), BF16) + _nbytes((ms, bn), BF16)
    return pl.pallas_call(
        _ffn_in_kernel,
        grid=(n // bn, m // bm),
        in_specs=[pl.BlockSpec((bm, k), lambda j, i: (i, 0)),
                  pl.BlockSpec((ms, k), lambda j, i: (0, 0), pipeline_mode=pl.Buffered(1)), wspec, wspec],
        out_specs=[pl.BlockSpec((bm, bn), lambda j, i: (i, j)), pl.BlockSpec((ms, bn), lambda j, i: (0, j))],
        out_shape=[jax.ShapeDtypeStruct((m, n), BF16), jax.ShapeDtypeStruct((ms, n), BF16)],
        scratch_shapes=[pltpu.VMEM((k, bn), BF16), pltpu.VMEM((k, bn), BF16)],
        compiler_params=pltpu.CompilerParams(
            dimension_semantics=("arbitrary", "arbitrary"),
            vmem_limit_bytes=_vmem_limit(blocks, 2 * _nbytes((k, bn), BF16) + _nbytes((ms, k), BF16))),
        name="ffn_in",
    )(x, xs, wg, wu)


def _rms_rows(x):
    return x * lax.rsqrt(jnp.mean(x * x, axis=-1, keepdims=True) + EPS)


def _norm_mod_kernel(h_ref, gpre_ref, scale_ref, shift_ref, u_ref):
    u = _rms_rows(h_ref[...]) * gpre_ref[...] * (1.0 + scale_ref[...]) + shift_ref[...]
    u_ref[...] = u.astype(u_ref.dtype)


def _resid_kernel(h_ref, y_ref, gate_ref, gpost_ref, *rest, res_w, with_next):
    yn = _rms_rows(y_ref[...]) * gpost_ref[...]
    h = h_ref[...] + res_w * gate_ref[...] * yn
    if with_next:
        gpre_ref, scale_ref, shift_ref, hn_ref, u_ref = rest
        u = _rms_rows(h) * gpre_ref[...] * (1.0 + scale_ref[...]) + shift_ref[...]
        u_ref[...] = u.astype(u_ref.dtype)
    else:
        (hn_ref,) = rest
    hn_ref[...] = h


def _row_block(rows_per_seq):
    return min(256, rows_per_seq)


def _seq_vec_spec(d, bm, rows_per_seq):
    return pl.BlockSpec((None, 1, d), lambda i: (i * bm // rows_per_seq, 0, 0))


def _norm_mod(h, gpre, scale, shift, rows_per_seq):
    m, d = h.shape
    bm = _row_block(rows_per_seq)
    row = pl.BlockSpec((bm, d), lambda i: (i, 0))
    vec = pl.BlockSpec((1, d), lambda i: (0, 0))
    sv = _seq_vec_spec(d, bm, rows_per_seq)
    return pl.pallas_call(
        _norm_mod_kernel,
        grid=(m // bm,),
        in_specs=[row, vec, sv, sv],
        out_specs=row,
        out_shape=jax.ShapeDtypeStruct((m, d), BF16),
        compiler_params=pltpu.CompilerParams(dimension_semantics=("arbitrary",)),
        name="norm_mod",
    )(h, gpre, scale, shift)


def _resid(h, y, gate, gpost, nxt, rows_per_seq, res_w):
    m, d = h.shape
    bm = _row_block(rows_per_seq)
    row = pl.BlockSpec((bm, d), lambda i: (i, 0))
    vec = pl.BlockSpec((1, d), lambda i: (0, 0))
    sv = _seq_vec_spec(d, bm, rows_per_seq)
    with_next = nxt is not None
    in_specs = [row, row, sv, vec] + ([vec, sv, sv] if with_next else [])
    out_specs = [row, row] if with_next else [row]
    out_shape = [jax.ShapeDtypeStruct((m, d), F32)] + ([jax.ShapeDtypeStruct((m, d), BF16)] if with_next else [])
    outs = pl.pallas_call(
        functools.partial(_resid_kernel, res_w=res_w, with_next=with_next),
        grid=(m // bm,),
        in_specs=in_specs,
        out_specs=out_specs,
        out_shape=out_shape,
        compiler_params=pltpu.CompilerParams(dimension_semantics=("arbitrary",)),
        name="resid",
    )(h, y, gate, gpost, *(nxt if with_next else ()))
    return (outs[0], outs[1]) if with_next else (outs[0], None)


def _l2_norm(x):
    return x * lax.rsqrt(jnp.sum(x * x, axis=-1, keepdims=True) + L2_EPS)


def _rope(x, pos):
    half = x.shape[-1] // 2
    inv = ROPE_THETA ** (-jnp.arange(half, dtype=F32) / half)
    ang = pos.astype(F32)[:, None] * inv[None, :]
    cos = jnp.cos(ang)[None, :, None, :]
    sin = jnp.sin(ang)[None, :, None, :]
    x1, x2 = x[..., :half], x[..., half:]
    return jnp.concatenate([x1 * cos - x2 * sin, x2 * cos + x1 * sin], axis=-1)


def _causal_conv(x, buf, w):
    t = x.shape[1]
    xp = jnp.concatenate([buf, x], axis=1)
    y = xp[:, 0:t] * w[0]
    for j in range(1, CONV_W):
        y = y + xp[:, j:j + t] * w[j]
    return jax.nn.silu(y), xp[:, t:]


def _to_chunks(x, c, pad):
    x = jnp.pad(x, [(0, 0), (0, pad)] + [(0, 0)] * (x.ndim - 2))
    b, tp = x.shape[:2]
    x = x.reshape((b, tp // c, c) + x.shape[2:])
    return jnp.transpose(x, (1, 0, 3, 2) + tuple(range(4, x.ndim)))


def _from_chunks(o, t):
    n, b, h, c, d = o.shape
    return jnp.transpose(o, (1, 0, 3, 2, 4)).reshape(b, n * c, h, d)[:, :t]


def _chunk_decay(gc):
    c = gc.shape[-1]
    tri = jnp.tril(jnp.ones((c, c), bool))
    diff = gc[..., :, None] - gc[..., None, :]
    return jnp.where(tri, jnp.exp(jnp.where(tri, diff, 0.0)), 0.0)


def _gated_delta_rule(q, k, v, g, beta, s0):
    t, dk = q.shape[1], q.shape[-1]
    dv = v.shape[-1]
    c = min(CHUNK, t)
    pad = (-t) % c
    q = _to_chunks(q * dk ** -0.5, c, pad)
    k = _to_chunks(k, c, pad)
    v = _to_chunks(v, c, pad)
    beta = _to_chunks(beta, c, pad)
    gc = jnp.cumsum(_to_chunks(g, c, pad), axis=-1)
    decay = _chunk_decay(gc)
    kb = k * beta[..., None]
    strict = jnp.tril(jnp.ones((c, c), bool), -1)
    lower = jnp.where(strict, jnp.einsum('nbhik,nbhjk->nbhij', kb, k) * decay, 0.0) + jnp.eye(c, dtype=F32)
    rhs = jnp.concatenate([v * beta[..., None], kb * jnp.exp(gc)[..., None]], axis=-1)
    sol = lax.linalg.triangular_solve(lower, rhs, left_side=True, lower=True, unit_diagonal=True)
    u, w = sol[..., :dv], sol[..., dv:]

    def step(S, xs):
        qc, kc, uc, wc, gcc, dc = xs
        v_new = uc - jnp.einsum('bhik,bhkv->bhiv', wc, S)
        intra = jnp.einsum('bhik,bhjk->bhij', qc, kc) * dc
        o = (jnp.einsum('bhik,bhkv->bhiv', qc * jnp.exp(gcc)[..., None], S)
             + jnp.einsum('bhij,bhjv->bhiv', intra, v_new))
        gl = gcc[..., -1:]
        S = S * jnp.exp(gl)[..., None] + jnp.einsum('bhjk,bhjv->bhkv', kc * jnp.exp(gl - gcc)[..., None], v_new)
        return S, o

    S, o = lax.scan(step, s0, (q, k, u, w, gc, decay))
    return _from_chunks(o, t), S


def _scan_retention(q, k, v, logd, s0):
    t = q.shape[1]
    c = min(CHUNK, t)
    pad = (-t) % c
    q = _to_chunks(q, c, pad)
    k = _to_chunks(k, c, pad)
    v = _to_chunks(v, c, pad)
    gc = jnp.cumsum(_to_chunks(logd, c, pad), axis=-1)
    decay = _chunk_decay(gc)

    def step(S, xs):
        qc, kc, vc, gcc, dc = xs
        intra = jnp.einsum('bhik,bhjk->bhij', qc, kc) * dc
        o = (jnp.einsum('bhik,bhkv->bhiv', qc * jnp.exp(gcc)[..., None], S)
             + jnp.einsum('bhij,bhjv->bhiv', intra, vc))
        gl = gcc[..., -1:]
        S = S * jnp.exp(gl)[..., None] + jnp.einsum('bhjk,bhjv->bhkv', kc * jnp.exp(gl - gcc)[..., None], vc)
        return S, o

    S, o = lax.scan(step, s0, (q, k, v, gc, decay))
    return _from_chunks(o, t), S


def _gather_rows(rows, idx):
    return jax.vmap(lambda r, i: r[i])(rows, idx)


def _paged_gather(pool, page_table, new_rows, idx):
    page = pool.shape[1]
    n_past = page_table.shape[1] * page
    t = new_rows.shape[1]
    pidx = jnp.minimum(idx, n_past - 1)
    phys = jax.vmap(lambda pt, i: pt[i // page])(page_table, pidx)
    past_rows = pool[phys, pidx % page]
    new_r = _gather_rows(new_rows, jnp.clip(idx - n_past, 0, t - 1))
    is_past = (idx < n_past).reshape(idx.shape + (1,) * (past_rows.ndim - idx.ndim))
    return jnp.where(is_past, past_rows, new_r)


def _sparse_attention(q, iq, iw, kidx_all, qpos, gather_kv, k_sel):
    b, t, h_b, _ = q.shape
    qb_sz = min(Q_BLOCK, t)
    pad = (-t) % qb_sz
    nb = (t + pad) // qb_sz

    def blocks(a):
        a = jnp.pad(a, [(0, 0), (0, pad)] + [(0, 0)] * (a.ndim - 2))
        return jnp.moveaxis(a.reshape((b, nb, qb_sz) + a.shape[2:]), 1, 0)

    qpos_b = jnp.pad(qpos, (0, pad), mode='edge').reshape(nb, qb_sz)
    kpos = jnp.arange(kidx_all.shape[1], dtype=jnp.int32)

    def one_block(args):
        qb, iqb, iwb, pb = args
        rel = jax.nn.relu(jnp.einsum('bqhd,bsd->bqhs', iqb, kidx_all))
        score = jnp.einsum('bqhs,bqh->bqs', rel, iwb)
        visible = kpos[None, :] <= pb[:, None]
        score = jnp.where(visible[None], score, -jnp.inf)
        _, idx = lax.top_k(score, k_sel)
        valid = idx <= pb[None, :, None]
        kr, vr = gather_kv(idx)
        kv_b = kr.shape[3]
        qg = qb.reshape(b, qb_sz, kv_b, h_b // kv_b, HEAD_DIM)
        logits = jnp.einsum('bqngd,bqsnd->bqngs', qg, kr) * HEAD_DIM ** -0.5
        logits = jnp.where(valid[:, :, None, None, :], logits, -jnp.inf)
        p = jax.nn.softmax(logits, axis=-1)
        o = jnp.einsum('bqngs,bqsnd->bqngd', p, vr)
        return o.reshape(b, qb_sz, h_b, HEAD_DIM)

    o = lax.map(one_block, (blocks(q), blocks(iq), blocks(iw), qpos_b))
    return jnp.moveaxis(o, 0, 1).reshape(b, nb * qb_sz, h_b, HEAD_DIM)[:, :t]


def _token_mixer(parts, dims, conv_w, a_log, dt_bias, gdn_g, past):
    (qkv_a, za, qb, kb, vb, iq, qc, kc, vc, gc, aa, ba, ik, iw) = parts
    h_a, h_b, h_c, kv_b, idx_dim = dims
    b, t = qkv_a.shape[:2]
    if past is None:
        conv_buf = jnp.zeros((b, CONV_W - 1, qkv_a.shape[-1]), F32)
        s_gdn = jnp.zeros((b, h_a, HEAD_DIM, HEAD_DIM), F32)
        s_ret = jnp.zeros((b, h_c, HEAD_DIM, HEAD_DIM), F32)
        n_past = 0
    else:
        conv_buf, s_gdn, s_ret, pool_k, pool_v, pool_kidx, page_table = past
        n_past = page_table.shape[1] * pool_k.shape[1]
    pos = n_past + jnp.arange(t, dtype=jnp.int32)

    conv_out, conv_new = _causal_conv(qkv_a, conv_buf, conv_w)
    qa, ka, va = [x.reshape(b, t, h_a, HEAD_DIM) for x in jnp.split(conv_out, 3, axis=-1)]
    g = -jnp.exp(a_log) * jax.nn.softplus(aa + dt_bias)
    oa, s_gdn_new = _gated_delta_rule(_l2_norm(qa), _l2_norm(ka), va, g, jax.nn.sigmoid(ba), s_gdn)
    oa = _rms_rows(oa) * gdn_g * jax.nn.silu(za.reshape(b, t, h_a, HEAD_DIM))

    qb = _rope(qb.reshape(b, t, h_b, HEAD_DIM), pos)
    kb = _rope(kb.reshape(b, t, kv_b, HEAD_DIM), pos)
    vb = vb.reshape(b, t, kv_b, HEAD_DIM)
    iq = _rope(iq.reshape(b, t, IDX_HEADS, idx_dim), pos)
    ik = _rope(ik.reshape(b, t, 1, idx_dim), pos)[:, :, 0]
    iw = iw * (IDX_HEADS ** -0.5 * idx_dim ** -0.5)
    if past is None:
        kidx_all = ik
        gather_kv = lambda idx: (_gather_rows(kb, idx), _gather_rows(vb, idx))
    else:
        past_kidx = pool_kidx[page_table].reshape(b, n_past, idx_dim)
        kidx_all = jnp.concatenate([past_kidx, ik], axis=1)
        gather_kv = lambda idx: (_paged_gather(pool_k, page_table, kb, idx),
                                 _paged_gather(pool_v, page_table, vb, idx))
    k_sel = max(1, min(TOPK_MAX, kidx_all.shape[1] // 4))
    ob = _sparse_attention(qb, iq, iw, kidx_all, pos, gather_kv, k_sel)

    qc = _rope(qc.reshape(b, t, h_c, HEAD_DIM), pos)
    kc = _rope(kc.reshape(b, t, h_c, HEAD_DIM), pos) * HEAD_DIM ** -0.5
    vc = vc.reshape(b, t, h_c, HEAD_DIM)
    log_gamma = jnp.log1p(-jnp.exp2(RET_GAMMA_BASE - jnp.arange(h_c, dtype=F32)))
    logd = jnp.broadcast_to(log_gamma, (b, t, h_c))
    oc, s_ret_new = _scan_retention(qc, kc, vc, logd, s_ret)
    oc = _rms_rows(oc) * jax.nn.silu(gc.reshape(b, t, h_c, HEAD_DIM))

    o = jnp.concatenate([oa.reshape(b, t, -1), ob.reshape(b, t, -1), oc.reshape(b, t, -1)], axis=-1)
    return o, (kb, vb, ik, s_gdn_new, conv_new, s_ret_new)


def _rope_tables(pos, head_dim, lanes):
    half = head_dim // 2
    inv = ROPE_THETA ** (-jnp.arange(half, dtype=F32) / half)
    ang = pos.astype(F32)[:, None] * inv[None, :]
    cos, sin = jnp.cos(ang), jnp.sin(ang)
    reps = lanes // head_dim
    return (jnp.tile(jnp.concatenate([cos, cos], axis=-1), (1, reps)),
            jnp.tile(jnp.concatenate([-sin, sin], axis=-1), (1, reps)))


def _rope_head(x, cos_t, sin_t):
    return x * cos_t + pltpu.roll(x, HEAD_DIM // 2, axis=1) * sin_t


def _rope_half_heads(x, cos_t, sin_t):
    lane = lax.broadcasted_iota(jnp.int32, x.shape, 1)
    fwd = pltpu.roll(x, 32, axis=1)
    bwd = pltpu.roll(x, LANE - 32, axis=1)
    return x * cos_t + jnp.where(lane % 64 < 32, bwd, fwd) * sin_t


def _dsa_prep_kernel(qb_ref, kb_ref, vb_ref, iq_ref, sm_ref, cos_ref, sin_ref, cosi_ref, sini_ref,
                     q_ref, kf_ref, kh_ref, vh_ref, iqlo_ref, iqhi_ref, kif_ref, ki2_ref):
    cos_t, sin_t = cos_ref[...], sin_ref[...]
    cosi_t, sini_t = cosi_ref[...], sini_ref[...]
    for h in range(qb_ref.shape[1] // HEAD_DIM):
        sl = slice(h * HEAD_DIM, (h + 1) * HEAD_DIM)
        q_ref[:, sl] = _rope_head(qb_ref[:, sl], cos_t, sin_t).astype(BF16)
    for h in range(kb_ref.shape[1] // HEAD_DIM):
        sl = slice(h * HEAD_DIM, (h + 1) * HEAD_DIM)
        kr = _rope_head(kb_ref[:, sl], cos_t, sin_t)
        kf_ref[:, sl] = kr
        kh_ref[:, sl] = kr.astype(BF16)
    vh_ref[...] = vb_ref[...].astype(BF16)
    lane = lax.broadcasted_iota(jnp.int32, cos_t.shape, 1)
    low = lane < 64
    for c in range(iq_ref.shape[1] // LANE):
        sl = slice(c * LANE, (c + 1) * LANE)
        r = _rope_half_heads(iq_ref[:, sl], cosi_t, sini_t)
        iqlo_ref[:, sl] = jnp.where(low, r, 0.0).astype(BF16)
        iqhi_ref[:, sl] = jnp.where(low, 0.0, r).astype(BF16)
    r = _rope_half_heads(sm_ref[...], cosi_t, sini_t)
    kif_ref[...] = r[:, :64]
    ki2_ref[...] = jnp.where(low, r, pltpu.roll(r, 64, axis=1)).astype(BF16)


def _col_spec(bt, width, off):
    assert off % width == 0, (off, width)
    return pl.BlockSpec((bt, width), lambda i: (i, off // width))


def _dsa_prep(p, lay, t, tables):
    m = p.shape[0]
    bt = min(256, t)
    nt = t // bt
    pos, wid = lay['pos'], lay['size']
    tab = pl.BlockSpec((bt, LANE), lambda i: (i % nt, 0))
    row = lambda w: pl.BlockSpec((bt, w), lambda i: (i, 0))
    wq, wk, wi = wid['qb'], wid['kb'], wid['iq']
    outs = pl.pallas_call(
        _dsa_prep_kernel,
        grid=(m // bt,),
        in_specs=[_col_spec(bt, wq, pos['qb']), _col_spec(bt, wk, pos['kb']), _col_spec(bt, wk, pos['vb']),
                  _col_spec(bt, wi, pos['iq']), _col_spec(bt, LANE, lay['small']), tab, tab, tab, tab],
        out_specs=[row(wq), row(wk), row(wk), row(wk), row(wi), row(wi), row(64), row(LANE)],
        out_shape=[jax.ShapeDtypeStruct((m, wq), BF16), jax.ShapeDtypeStruct((m, wk), F32),
                   jax.ShapeDtypeStruct((m, wk), BF16), jax.ShapeDtypeStruct((m, wk), BF16),
                   jax.ShapeDtypeStruct((m, wi), BF16), jax.ShapeDtypeStruct((m, wi), BF16),
                   jax.ShapeDtypeStruct((m, 64), F32), jax.ShapeDtypeStruct((m, LANE), BF16)],
        compiler_params=pltpu.CompilerParams(dimension_semantics=("arbitrary",)),
        name="dsa_prep",
    )(p, p, p, p, p, *tables)
    return outs


INT32_MIN = -2 ** 31


def _sortable_key(x):
    b = lax.bitcast_convert_type(x, jnp.int32)
    return b ^ ((b >> 31) & jnp.int32(0x7FFFFFFF))


def _kth_largest_key(keys_ref, k_sel):
    kf = jnp.float32(k_sel)

    def count_ge(t):
        return jnp.sum(jnp.where(keys_ref[...] >= t, 1.0, 0.0), axis=1, keepdims=True)

    rows = keys_ref.shape[0]
    zero = jnp.zeros((rows, 1), jnp.int32)
    t0 = jnp.where(count_ge(zero) >= kf, zero, jnp.int32(INT32_MIN))

    def body(i, t):
        cand = t | (jnp.int32(1) << (30 - i))
        return jnp.where(count_ge(cand) >= kf, cand, t)

    return lax.fori_loop(0, 31, body, t0)


def _tie_cut(count_tied_below, need, n_index):
    bits = max(1, (n_index - 1).bit_length())

    def body(i, base):
        cand = base + (jnp.int32(1) << (bits - 1 - i))
        return jnp.where(count_tied_below(cand) < need, cand, base)

    return lax.fori_loop(0, bits, body, jnp.zeros_like(need, dtype=jnp.int32)) + 1


def _chosen(keys, pos, thr, cut):
    bound = jnp.where(keys > thr, jnp.int32(2 ** 31 - 1), jnp.where(keys == thr, cut, jnp.int32(0)))
    return pos < bound


def _dsa_kernel(q_ref, k_ref, v_ref, iqlo_ref, iqhi_ref, ki2_ref, sm_ref, o_ref, score_ref, keys_ref, cut_ref,
                *, k_sel, kv_b, idx_dim, iw_lane, q_block0):
    qb = q_ref.shape[0]
    n_keys = k_ref.shape[0]
    group = q_ref.shape[1] // HEAD_DIM // kv_b
    q0 = (q_block0 + pl.program_id(1)) * qb
    ki2 = ki2_ref[...]
    w_scale = IDX_HEADS ** -0.5 * idx_dim ** -0.5
    heads_per_dot = 8
    score_ref[...] = jnp.zeros_like(score_ref)
    for gi in range(IDX_HEADS // heads_per_dot):
        rows = []
        for c in range(heads_per_dot // 2):
            sl = slice((gi * heads_per_dot // 2 + c) * LANE, (gi * heads_per_dot // 2 + c + 1) * LANE)
            rows.append(iqlo_ref[:, sl])
            rows.append(iqhi_ref[:, sl])
        lhs = jnp.concatenate(rows, axis=0)
        d = lax.dot_general(lhs, ki2, (((1,), (1,)), ((), ())), preferred_element_type=F32)
        acc = score_ref[...]
        for hh in range(heads_per_dot):
            lane = iw_lane + gi * heads_per_dot + hh
            w = sm_ref[:, lane:lane + 1] * w_scale
            acc = acc + jnp.maximum(d[hh * qb:(hh + 1) * qb], 0.0) * w
        score_ref[...] = acc

    qpos = q0 + lax.broadcasted_iota(jnp.int32, (qb, n_keys), 0)
    kpos = lax.broadcasted_iota(jnp.int32, (qb, n_keys), 1)
    visible = kpos <= qpos
    keys_ref[...] = jnp.where(visible, _sortable_key(score_ref[...] + 0.0), jnp.int32(INT32_MIN))
    thr = _kth_largest_key(keys_ref, k_sel)
    count = lambda mask: jnp.sum(jnp.where(mask, 1.0, 0.0), axis=1, keepdims=True)
    need = jnp.float32(k_sel) - count(keys_ref[...] > thr)
    surplus = jnp.max(count(keys_ref[...] == thr) - need)
    cut_ref[...] = jnp.full(cut_ref.shape, n_keys, jnp.int32)

    @pl.when(surplus > 0.5)
    def _():
        tied_pos = lambda: jnp.where(keys_ref[...] == thr, kpos, jnp.int32(n_keys))
        cut = _tie_cut(lambda c: count(tied_pos() < c), need, n_keys)
        cut_ref[...] = jnp.broadcast_to(cut, cut_ref.shape)

    neg = jnp.float32(-jnp.inf)
    keys = keys_ref[...]
    score_ref[...] = jnp.where(visible, jnp.where(_chosen(keys, kpos, thr, cut_ref[:, 0:1]), 0.0, neg), neg)

    scale = HEAD_DIM ** -0.5
    for n in range(kv_b):
        q3 = jnp.concatenate(
            [q_ref[:, (n * group + g) * HEAD_DIM:(n * group + g + 1) * HEAD_DIM] for g in range(group)], axis=0)
        kn = k_ref[:, n * HEAD_DIM:(n + 1) * HEAD_DIM]
        lg = lax.dot_general(q3, kn, (((1,), (1,)), ((), ())), preferred_element_type=F32) * scale
        lg = lg.reshape(group, qb, n_keys) + score_ref[...][None]
        mx = jnp.max(lg, axis=-1, keepdims=True)
        pr = jnp.exp(lg - mx)
        den = jnp.sum(pr, axis=-1, keepdims=True)
        o = jnp.dot(pr.reshape(group * qb, n_keys).astype(BF16), v_ref[:, n * HEAD_DIM:(n + 1) * HEAD_DIM],
                    preferred_element_type=F32)
        o = o / den.reshape(group * qb, 1)
        for g in range(group):
            sl = slice((n * group + g) * HEAD_DIM, (n * group + g + 1) * HEAD_DIM)
            o_ref[:, sl] = o[g * qb:(g + 1) * qb].astype(o_ref.dtype)


def _dsa_prompt(q, kh, vh, iqlo, iqhi, ki2, p, lay, b, t, kv_b, idx_dim):
    m, wq = q.shape
    wk, wi = kh.shape[1], iqlo.shape[1]
    qb = min(Q_BLOCK, t)
    nq = t // qb
    k_sel = max(1, min(TOPK_MAX, t // 4))
    n_bands = next(n for n in (8, 4, 2, 1) if nq % n == 0)
    per_band = nq // n_bands
    kh3, vh3, ki3 = (x.reshape(b, t, x.shape[1]) for x in (kh, vh, ki2))
    outs = []
    for band in range(n_bands):
        first = band * per_band
        n_keys = (first + per_band) * qb
        qrow = lambda w, first=first: pl.BlockSpec((qb, w), lambda bi, qi: (bi * nq + first + qi, 0))
        seq = lambda w, n_keys=n_keys: pl.BlockSpec((None, n_keys, w), lambda bi, qi: (bi, 0, 0))
        small = pl.BlockSpec((qb, LANE), lambda bi, qi, first=first: (bi * nq + first + qi, lay['small'] // LANE))
        blocks = (_nbytes((qb, wq), BF16) * 2 + 2 * _nbytes((n_keys, wk), BF16) + 2 * _nbytes((qb, wi), BF16)
                  + _nbytes((n_keys, LANE), BF16) + _nbytes((qb, LANE), F32))
        temps = 2 * _nbytes((qb, n_keys), F32) + 2 * _nbytes((8 * qb, n_keys), F32)
        outs.append(pl.pallas_call(
            functools.partial(_dsa_kernel, k_sel=k_sel, kv_b=kv_b, idx_dim=idx_dim,
                              iw_lane=lay['pos']['iw'] - lay['small'], q_block0=first),
            grid=(b, per_band),
            in_specs=[qrow(wq), seq(wk), seq(wk), qrow(wi), qrow(wi), seq(LANE), small],
            out_specs=pl.BlockSpec((None, qb, wq), lambda bi, qi: (bi, qi, 0)),
            out_shape=jax.ShapeDtypeStruct((b, per_band * qb, wq), BF16),
            scratch_shapes=[pltpu.VMEM((qb, n_keys), F32), pltpu.VMEM((qb, n_keys), jnp.int32),
                            pltpu.VMEM((qb, LANE), jnp.int32)],
            compiler_params=pltpu.CompilerParams(
                dimension_semantics=("arbitrary", "arbitrary"),
                vmem_limit_bytes=_vmem_limit(blocks, temps)),
            name="dsa",
        )(q, kh3, vh3, iqlo, iqhi, ki3, p))
    return jnp.concatenate(outs, axis=1).reshape(m, wq)


def _gdn_prep_kernel(x_ref, prev_ref, buf_ref, w_ref, q_ref, k_ref, v_ref, xp_ref, *, blocks_per_seq):
    bt = x_ref.shape[0]
    first = pl.program_id(0) % blocks_per_seq == 0
    xp_ref[0:8, :] = jnp.where(first, buf_ref[...], prev_ref[...])
    xp_ref[8:, :] = x_ref[...]
    n_heads = q_ref.shape[1] // HEAD_DIM
    for part, out_ref in enumerate((q_ref, k_ref, v_ref)):
        for h in range(n_heads):
            src = slice((part * n_heads + h) * HEAD_DIM, (part * n_heads + h + 1) * HEAD_DIM)
            y = xp_ref[8 - (CONV_W - 1):8 - (CONV_W - 1) + bt, src] * w_ref[0:1, src]
            for j in range(1, CONV_W):
                y = y + xp_ref[8 - (CONV_W - 1) + j:8 - (CONV_W - 1) + j + bt, src] * w_ref[j:j + 1, src]
            y = y * jax.nn.sigmoid(y)
            if part < 2:
                y = y * lax.rsqrt(jnp.sum(y * y, axis=-1, keepdims=True) + L2_EPS)
            if part == 0:
                y = y * HEAD_DIM ** -0.5
            out_ref[:, h * HEAD_DIM:(h + 1) * HEAD_DIM] = y


def _gdn_prep(p, lay, t, buf8, conv_w):
    m = p.shape[0]
    bt = min(256, t)
    nt = t // bt
    c = lay['size']['qkv_a']
    off = lay['pos']['qkv_a']
    assert off % c == 0
    cb = off // c
    wq = c // 3
    row = pl.BlockSpec((bt, wq), lambda i: (i, 0))
    blocks = _nbytes((bt, c), F32) + 2 * _nbytes((8, c), F32) + _nbytes((CONV_W, c), F32) + 3 * _nbytes((bt, wq), F32)
    return pl.pallas_call(
        functools.partial(_gdn_prep_kernel, blocks_per_seq=nt),
        grid=(m // bt,),
        in_specs=[pl.BlockSpec((bt, c), lambda i: (i, cb)),
                  pl.BlockSpec((8, c), lambda i: (jnp.maximum(i * (bt // 8) - 1, 0), cb)),
                  pl.BlockSpec((None, 8, c), lambda i: (i // nt, 0, 0)),
                  pl.BlockSpec((CONV_W, c), lambda i: (0, 0))],
        out_specs=[row, row, row],
        out_shape=[jax.ShapeDtypeStruct((m, wq), F32)] * 3,
        scratch_shapes=[pltpu.VMEM((bt + 8, c), F32)],
        compiler_params=pltpu.CompilerParams(
            dimension_semantics=("arbitrary",),
            vmem_limit_bytes=_vmem_limit(blocks, _nbytes((bt + 8, c), F32))),
        name="gdn_prep",
    )(p, p, buf8, conv_w)


def _dot_hi(a, b):
    ah, bh = a.astype(BF16), b.astype(BF16)
    al = (a - ah.astype(F32)).astype(BF16)
    bl = (b - bh.astype(F32)).astype(BF16)
    dot = functools.partial(jnp.dot, preferred_element_type=F32)
    return dot(ah, bh) + (dot(al, bh) + dot(ah, bl))


def _dot_nt(a, b):
    return lax.dot_general(a, b, (((1,), (1,)), ((), ())), preferred_element_type=F32)


def _dot_tn(a, b):
    return lax.dot_general(a, b, (((0,), (0,)), ((), ())), preferred_element_type=F32)


def _gdn_kernel(q_ref, k_ref, v_ref, g_ref, beta_ref, z_ref, gn_ref, s0_ref, o_ref, s_ref, *, chunk):
    t = q_ref.shape[0]
    hp = q_ref.shape[1] // HEAD_DIM
    c = chunk
    ii = lax.broadcasted_iota(jnp.int32, (c, c), 0)
    jj = lax.broadcasted_iota(jnp.int32, (c, c), 1)
    tril, strict, eye = ii >= jj, ii > jj, ii == jj

    @pl.when(pl.program_id(1) == 0)
    def _():
        s_ref[...] = s0_ref[...]

    gn = gn_ref[...]

    heads = range(hp)
    cols = [slice(j * HEAD_DIM, (j + 1) * HEAD_DIM) for j in heads]

    def body(i, carry):
        r0 = pl.multiple_of(i * c, c)
        g_row = [g_ref[j, pl.ds(i, 1), :] for j in heads]
        gc_col = [jnp.sum(jnp.where(tril, g_row[j], 0.0), axis=1, keepdims=True) for j in heads]
        gc_row = [jnp.sum(jnp.where(eye, gc_col[j], 0.0), axis=0, keepdims=True) for j in heads]
        beta_col = [jnp.sum(jnp.where(eye, beta_ref[j, pl.ds(i, 1), :], 0.0), axis=1, keepdims=True) for j in heads]
        decay = [jnp.where(tril, jnp.exp(jnp.where(tril, gc_col[j] - gc_row[j], 0.0)), 0.0) for j in heads]
        k = [k_ref[pl.ds(r0, c), cols[j]] for j in heads]
        kb = [k[j] * beta_col[j] for j in heads]
        kh = [k[j].astype(BF16) for j in heads]
        a = [jnp.where(strict, _dot_nt(kb[j].astype(BF16), kh[j]) * decay[j], 0.0) for j in heads]
        inv = [jnp.where(eye, 1.0, 0.0) - a[j] for j in heads]
        pw = a
        span = 2
        while span < c:
            pw = [_dot_hi(pw[j], pw[j]) for j in heads]
            inv = [inv[j] + _dot_hi(inv[j], pw[j]) for j in heads]
            span *= 2
        egc = [jnp.exp(gc_col[j]) for j in heads]
        u = [_dot_hi(inv[j], v_ref[pl.ds(r0, c), cols[j]] * beta_col[j]) for j in heads]
        w = [_dot_hi(inv[j], kb[j] * egc[j]) for j in heads]
        s = [s_ref[j] for j in heads]
        sh = [s[j].astype(BF16) for j in heads]
        vnh = [(u[j] - jnp.dot(w[j].astype(BF16), sh[j], preferred_element_type=F32)).astype(BF16) for j in heads]
        gl = [jnp.sum(g_row[j], axis=1, keepdims=True) for j in heads]
        for j in heads:
            s_ref[j] = s[j] * jnp.exp(gl[j]) + _dot_tn((k[j] * jnp.exp(gl[j] - gc_col[j])).astype(BF16), vnh[j])
        q = [q_ref[pl.ds(r0, c), cols[j]] for j in heads]
        intra = [_dot_nt(q[j].astype(BF16), kh[j]) * decay[j] for j in heads]
        o = [jnp.dot((q[j] * egc[j]).astype(BF16), sh[j], preferred_element_type=F32)
             + jnp.dot(intra[j].astype(BF16), vnh[j], preferred_element_type=F32) for j in heads]
        for j in heads:
            z = z_ref[pl.ds(r0, c), cols[j]]
            o_ref[pl.ds(r0, c), cols[j]] = (_rms_rows(o[j]) * gn * (z * jax.nn.sigmoid(z))).astype(o_ref.dtype)
        return carry

    lax.fori_loop(0, t // c, body, 0)


def _gdn_time_block(t):
    return min(4 * CHUNK, t)


def _gdn_gate_rows(x, b, t, h_a, c):
    tb = _gdn_time_block(t)
    x = jnp.transpose(x, (0, 2, 1)).reshape(b, h_a, t // tb, tb // c, c)
    return jnp.transpose(x, (0, 2, 1, 3, 4))


def _gdn(qn, kn, vv, g, beta, p, lay, gdn_g, s0, b, t):
    m, w = qn.shape
    h_a = w // HEAD_DIM
    c = g.shape[-1]
    tb = _gdn_time_block(t)
    nt = t // tb
    zoff = lay['pos']['za']
    assert zoff % w == 0
    seq = pl.BlockSpec((tb, w), lambda bi, ti: (bi * nt + ti, 0))
    gspec = pl.BlockSpec((None, None, h_a, tb // c, c), lambda bi, ti: (bi, ti, 0, 0, 0))
    sspec = pl.BlockSpec((None, h_a, HEAD_DIM, HEAD_DIM), lambda bi, ti: (bi, 0, 0, 0))
    blocks = 4 * _nbytes((tb, w), F32) + _nbytes((tb, w), BF16) + 2 * _nbytes((h_a, HEAD_DIM, HEAD_DIM), F32)
    return pl.pallas_call(
        functools.partial(_gdn_kernel, chunk=c),
        grid=(b, nt),
        in_specs=[seq, seq, seq, gspec, gspec,
                  pl.BlockSpec((tb, w), lambda bi, ti: (bi * nt + ti, zoff // w)),
                  pl.BlockSpec((1, HEAD_DIM), lambda bi, ti: (0, 0)), sspec],
        out_specs=[seq, sspec],
        out_shape=[jax.ShapeDtypeStruct((m, w), BF16), jax.ShapeDtypeStruct(s0.shape, F32)],
        compiler_params=pltpu.CompilerParams(
            dimension_semantics=("arbitrary", "arbitrary"),
            vmem_limit_bytes=_vmem_limit(blocks, 0)),
        name="gdn",
    )(qn, kn, vv, g, beta, p, gdn_g.reshape(1, HEAD_DIM), s0)


def _retention_kernel(q_ref, k_ref, v_ref, gate_ref, cos_ref, sin_ref, lg_ref, s0_ref, o_ref, s_ref, *, chunk):
    t = q_ref.shape[0]
    hp = q_ref.shape[1] // HEAD_DIM
    c = chunk
    ii = lax.broadcasted_iota(jnp.int32, (c, c), 0)
    jj = lax.broadcasted_iota(jnp.int32, (c, c), 1)
    tril = ii >= jj
    dist = (ii - jj).astype(F32)
    step = lax.broadcasted_iota(jnp.int32, (c, 1), 0).astype(F32)
    s_ref[...] = s0_ref[...]

    def body(i, carry):
        r0 = pl.multiple_of(i * c, c)
        cos_t = cos_ref[pl.ds(r0, c), :]
        sin_t = sin_ref[pl.ds(r0, c), :]
        for j in range(hp):
            sl = slice(j * HEAD_DIM, (j + 1) * HEAD_DIM)
            lg = lg_ref[j, 0:1, 0:1]
            decay = jnp.where(tril, jnp.exp(jnp.where(tril, dist * lg, 0.0)), 0.0)
            q = _rope_head(q_ref[pl.ds(r0, c), sl], cos_t, sin_t)
            k = _rope_head(k_ref[pl.ds(r0, c), sl], cos_t, sin_t) * HEAD_DIM ** -0.5
            vh = v_ref[pl.ds(r0, c), sl].astype(BF16)
            s = s_ref[j]
            intra = _dot_nt(q.astype(BF16), k.astype(BF16)) * decay
            o = (jnp.dot((q * jnp.exp((step + 1.0) * lg)).astype(BF16), s.astype(BF16), preferred_element_type=F32)
                 + jnp.dot(intra.astype(BF16), vh, preferred_element_type=F32))
            s_ref[j] = s * jnp.exp(c * lg) + _dot_tn((k * jnp.exp((c - 1.0 - step) * lg)).astype(BF16), vh)
            gt = gate_ref[pl.ds(r0, c), sl]
            o_ref[pl.ds(r0, c), sl] = (_rms_rows(o) * (gt * jax.nn.sigmoid(gt))).astype(o_ref.dtype)
        return carry

    lax.fori_loop(0, t // c, body, 0)


def _retention(p, lay, tables, s0, b, t, hp, chunk):
    m = p.shape[0]
    h_c = s0.shape[1]
    assert h_c % hp == 0 and t % chunk == 0
    wb = hp * HEAD_DIM
    pos = lay['pos']
    log_gamma = jnp.log1p(-jnp.exp2(RET_GAMMA_BASE - jnp.arange(h_c, dtype=F32)))
    lg = jnp.broadcast_to(log_gamma[:, None, None], (h_c, 8, LANE))

    def col(nm):
        assert pos[nm] % wb == 0
        return pl.BlockSpec((t, wb), lambda bi, hi: (bi, pos[nm] // wb + hi))

    tab = pl.BlockSpec((t, LANE), lambda bi, hi: (0, 0))
    sspec = pl.BlockSpec((None, hp, HEAD_DIM, HEAD_DIM), lambda bi, hi: (bi, hi, 0, 0))
    blocks = (4 * _nbytes((t, wb), F32) + 2 * _nbytes((t, LANE), F32) + _nbytes((t, wb), BF16)
              + 2 * _nbytes((hp, HEAD_DIM, HEAD_DIM), F32))
    return pl.pallas_call(
        functools.partial(_retention_kernel, chunk=chunk),
        grid=(b, h_c // hp),
        in_specs=[col('qc'), col('kc'), col('vc'), col('gc'), tab, tab,
                  pl.BlockSpec((hp, 8, LANE), lambda bi, hi: (hi, 0, 0)), sspec],
        out_specs=[pl.BlockSpec((t, wb), lambda bi, hi: (bi, hi)), sspec],
        out_shape=[jax.ShapeDtypeStruct((m, h_c * HEAD_DIM), BF16), jax.ShapeDtypeStruct(s0.shape, F32)],
        compiler_params=pltpu.CompilerParams(
            dimension_semantics=("arbitrary", "arbitrary"),
            vmem_limit_bytes=_vmem_limit(blocks, 0)),
        name="retention",
    )(p, p, p, p, tables[0], tables[1], lg, s0)


PAGES_PER_STEP = 8
MASKED = -1e30


def _stack_idx_heads(iqlo_ref, iqhi_ref):
    rows = []
    for c in range(iqlo_ref.shape[1] // LANE):
        sl = slice(c * LANE, (c + 1) * LANE)
        rows.append(iqlo_ref[:, sl])
        rows.append(iqhi_ref[:, sl])
    return jnp.concatenate(rows, axis=0).astype(BF16)


def _idx_scores(lhs, ki2, sm_ref, rows, iw_lane, w_scale):
    d = _dot_nt(lhs, ki2.astype(BF16))
    acc = jnp.zeros((rows, ki2.shape[0]), F32)
    for h in range(IDX_HEADS):
        w = sm_ref[:, iw_lane + h:iw_lane + h + 1] * w_scale
        acc = acc + jnp.maximum(d[h * rows:(h + 1) * rows], 0.0) * w
    return acc


def _dsa_s_scores_kernel(pt_ref, iqlo_ref, iqhi_ref, sm_ref, *rest, idx_dim, iw_lane):
    page_refs, score_ref = rest[:-1], rest[-1]
    rows = iqlo_ref.shape[0]
    page = page_refs[0].shape[0]
    lhs = _stack_idx_heads(iqlo_ref, iqhi_ref)
    w_scale = IDX_HEADS ** -0.5 * idx_dim ** -0.5
    for i, pr in enumerate(page_refs):
        score_ref[:, i * page:(i + 1) * page] = _idx_scores(lhs, pr[...], sm_ref, rows, iw_lane, w_scale)


def _dsa_s_scores(page_table, iqlo, iqhi, sm, pool_ki2, layer, t, idx_dim, iw_lane, pages_per_step):
    b, n_pages = page_table.shape
    page = pool_ki2.shape[2]
    wi = iqlo.shape[1]
    assert n_pages % pages_per_step == 0
    row = lambda w: pl.BlockSpec((t, w), lambda bi, gi, pt: (bi, 0))
    pages = [pl.BlockSpec((None, None, page, LANE),
                          lambda bi, gi, pt, i=i: (layer, pt[bi, gi * pages_per_step + i], 0, 0))
             for i in range(pages_per_step)]
    return pl.pallas_call(
        functools.partial(_dsa_s_scores_kernel, idx_dim=idx_dim, iw_lane=iw_lane),
        grid_spec=pltpu.PrefetchScalarGridSpec(
            num_scalar_prefetch=1,
            grid=(b, n_pages // pages_per_step),
            in_specs=[row(wi), row(wi), row(LANE)] + pages,
            out_specs=pl.BlockSpec((None, t, pages_per_step * page), lambda bi, gi, pt: (bi, 0, gi)),
        ),
        out_shape=jax.ShapeDtypeStruct((b, t, n_pages * page), F32),
        compiler_params=pltpu.CompilerParams(dimension_semantics=("arbitrary", "arbitrary")),
        name="dsa_s_scores",
    )(page_table, iqlo, iqhi, sm, *([pool_ki2] * pages_per_step))


def _dsa_s_attn_kernel(pt_ref, sp_ref, sn_ref, q_ref, kn_ref, vn_ref, *rest, k_sel, kv_b):
    n_pg = (len(rest) - 8) // 2
    k_refs, v_refs = rest[:n_pg], rest[n_pg:2 * n_pg]
    o_ref, keysp_ref, keysn_ref, thr_ref, cut_ref, m_ref, l_ref, acc_ref = rest[2 * n_pg:]
    t = q_ref.shape[0]
    page = k_refs[0].shape[0]
    n_past = sp_ref.shape[1]
    n_heads = q_ref.shape[1] // HEAD_DIM
    group = n_heads // kv_b
    gi = pl.program_id(1)
    scale = HEAD_DIM ** -0.5
    qi = lax.broadcasted_iota(jnp.int32, (t, page), 0)
    kj = lax.broadcasted_iota(jnp.int32, (t, page), 1)
    new_visible = kj <= qi

    @pl.when(gi == 0)
    def _():
        keysp_ref[...] = _sortable_key(sp_ref[...] + 0.0)
        keysn_ref[...] = jnp.where(new_visible, _sortable_key(sn_ref[...] + 0.0), jnp.int32(INT32_MIN))
        m_ref[...] = jnp.full_like(m_ref, MASKED)
        l_ref[...] = jnp.zeros_like(l_ref)
        acc_ref[...] = jnp.zeros_like(acc_ref)

        kf = jnp.float32(k_sel)

        def count_ge(thr):
            return (jnp.sum(jnp.where(keysp_ref[...] >= thr, 1.0, 0.0), axis=1, keepdims=True)
                    + jnp.sum(jnp.where(keysn_ref[...] >= thr, 1.0, 0.0), axis=1, keepdims=True))

        zero = jnp.zeros((t, 1), jnp.int32)
        thr0 = jnp.where(count_ge(zero) >= kf, zero, jnp.int32(INT32_MIN))

        def search(i, thr):
            cand = thr | (jnp.int32(1) << (30 - i))
            return jnp.where(count_ge(cand) >= kf, cand, thr)

        thr = lax.fori_loop(0, 31, search, thr0)
        thr_ref[...] = jnp.broadcast_to(thr, thr_ref.shape)

        count = lambda mask: jnp.sum(jnp.where(mask, 1.0, 0.0), axis=1, keepdims=True)
        need = kf - count(keysp_ref[...] > thr) - count(keysn_ref[...] > thr)
        n_index = n_past + page
        pos_p = lax.broadcasted_iota(jnp.int32, keysp_ref.shape, 1)
        tied_p = lambda: jnp.where(keysp_ref[...] == thr, pos_p, jnp.int32(n_index))
        tied_n = lambda: jnp.where(keysn_ref[...] == thr, n_past + kj, jnp.int32(n_index))
        cut = _tie_cut(lambda c: count(tied_p() < c) + count(tied_n() < c), need, n_index)
        cut_ref[...] = jnp.broadcast_to(cut, cut_ref.shape)

    thr = thr_ref[:, 0:1]
    cut = cut_ref[:, 0:1]

    chosen = lambda keys, pos: _chosen(keys, pos, thr, cut)

    q = q_ref[...]
    q3 = [jnp.concatenate([q[:, (n * group + g) * HEAD_DIM:(n * group + g + 1) * HEAD_DIM] for g in range(group)],
                          axis=0).astype(BF16) for n in range(kv_b)]

    def update(k_heads, v_heads, bias):
        n_keys = bias.shape[1]
        for n in range(kv_b):
            rows = slice(n * group * t, (n + 1) * group * t)
            lg = _dot_nt(q3[n], k_heads[n].astype(BF16)) * scale
            lg = (lg.reshape(group, t, n_keys) + bias[None]).reshape(group * t, n_keys)
            m_old = m_ref[rows, 0:1]
            m_new = jnp.maximum(m_old, jnp.max(lg, axis=-1, keepdims=True))
            alpha = jnp.exp(m_old - m_new)
            pr = jnp.exp(lg - m_new)
            l_ref[rows, :] = alpha * l_ref[rows, :] + jnp.sum(pr, axis=-1, keepdims=True)
            acc_ref[rows, :] = alpha * acc_ref[rows, :] + jnp.dot(pr.astype(BF16), v_heads[n].astype(BF16),
                                                                  preferred_element_type=F32)
            m_ref[rows, :] = jnp.broadcast_to(m_new, (group * t, LANE))

    biases = []
    for i in range(n_pg):
        start = pl.multiple_of((gi * n_pg + i) * page, page)
        biases.append(jnp.where(chosen(keysp_ref[:, pl.ds(start, page)], start + kj), 0.0, MASKED))
    update([jnp.concatenate([r[:, n, :] for r in k_refs], axis=0) for n in range(kv_b)],
           [jnp.concatenate([r[:, n, :] for r in v_refs], axis=0) for n in range(kv_b)],
           jnp.concatenate(biases, axis=1))

    @pl.when(gi == pl.num_programs(1) - 1)
    def _():
        bias_new = jnp.where(new_visible, jnp.where(chosen(keysn_ref[...], n_past + kj), 0.0, MASKED), MASKED)
        heads_of = lambda ref: [ref[:, n * HEAD_DIM:(n + 1) * HEAD_DIM] for n in range(kv_b)]
        update(heads_of(kn_ref), heads_of(vn_ref), bias_new)
        for h in range(n_heads):
            rows = slice(h * t, (h + 1) * t)
            o_ref[:, h * HEAD_DIM:(h + 1) * HEAD_DIM] = acc_ref[rows, :] / l_ref[rows, 0:1]


def _dsa_s_attn(page_table, scores_past, scores_new, q, k_new, v_new, pool_k, pool_v, layer, t, kv_b, k_sel,
                pages_per_step):
    b, n_pages = page_table.shape
    page = pool_k.shape[2]
    wq = q.shape[1]
    wk = k_new.shape[2]
    n_past = n_pages * page
    pk, pv = pool_k, pool_v
    pages = [pl.BlockSpec((None, None, page, kv_b, HEAD_DIM),
                          lambda bi, gi, pt, i=i: (layer, pt[bi, gi * pages_per_step + i], 0, 0, 0))
             for i in range(pages_per_step)]
    per_seq = lambda rows, w: pl.BlockSpec((None, rows, w), lambda bi, gi, pt: (bi, 0, 0))
    n_heads = wq // HEAD_DIM
    return pl.pallas_call(
        functools.partial(_dsa_s_attn_kernel, k_sel=k_sel, kv_b=kv_b),
        grid_spec=pltpu.PrefetchScalarGridSpec(
            num_scalar_prefetch=1,
            grid=(b, n_pages // pages_per_step),
            in_specs=[per_seq(t, n_past), per_seq(t, page), pl.BlockSpec((t, wq), lambda bi, gi, pt: (bi, 0)),
                      per_seq(page, wk), per_seq(page, wk)] + pages + pages,
            out_specs=pl.BlockSpec((t, wq), lambda bi, gi, pt: (bi, 0)),
            scratch_shapes=[pltpu.VMEM((t, n_past), jnp.int32), pltpu.VMEM((t, page), jnp.int32),
                            pltpu.VMEM((t, LANE), jnp.int32), pltpu.VMEM((t, LANE), jnp.int32),
                            pltpu.VMEM((n_heads * t, LANE), F32), pltpu.VMEM((n_heads * t, LANE), F32),
                            pltpu.VMEM((n_heads * t, HEAD_DIM), F32)],
        ),
        out_shape=jax.ShapeDtypeStruct((b * t, wq), F32),
        compiler_params=pltpu.CompilerParams(dimension_semantics=("arbitrary", "arbitrary")),
        name="dsa_s_attn",
    )(page_table, scores_past, scores_new, q, k_new, v_new, *([pk] * pages_per_step), *([pv] * pages_per_step))


def _in_proj_layout(d_model, kv_b, idx_dim):
    d_mix = d_model
    h_a = (3 * d_mix) // (8 * HEAD_DIM)
    h_b = h_a
    h_c = d_mix // HEAD_DIM - h_a - h_b
    sizes = (3 * h_a * HEAD_DIM, h_a * HEAD_DIM, h_a, h_a,
             h_b * HEAD_DIM, kv_b * HEAD_DIM, kv_b * HEAD_DIM, IDX_HEADS * idx_dim, idx_dim, IDX_HEADS,
             h_c * HEAD_DIM, h_c * HEAD_DIM, h_c * HEAD_DIM, h_c * HEAD_DIM)
    names = ('qkv_a', 'za', 'aa', 'ba', 'qb', 'kb', 'vb', 'iq', 'ik', 'iw', 'qc', 'kc', 'vc', 'gc')
    src, size, off = {}, {}, 0
    for nm, s in zip(names, sizes):
        src[nm] = off
        size[nm] = s
        off += s
    wide = ('iq', 'qc', 'kc', 'vc', 'gc', 'qb', 'za', 'qkv_a', 'kb', 'vb')
    narrow = ('ik', 'iw', 'aa', 'ba')
    pos, cur = {}, 0
    for nm in wide:
        cur = -(-cur // size[nm]) * size[nm]
        pos[nm] = cur
        cur += size[nm]
    small = -(-cur // LANE) * LANE
    cur = small
    for nm in narrow:
        pos[nm] = cur
        cur += size[nm]
    assert cur <= small + LANE
    lay = dict(src=src, size=size, pos=pos, small=small, order=wide + narrow, n_src=off)
    return (h_a, h_b, h_c), lay


def _rearrange_w_in(w_in, lay, bn):
    cols, cur = [], 0
    for nm in lay['order']:
        gap = lay['pos'][nm] - cur
        if gap:
            cols.append(jnp.zeros(w_in.shape[:-1] + (gap,), w_in.dtype))
        cols.append(w_in[..., lay['src'][nm]:lay['src'][nm] + lay['size'][nm]])
        cur = lay['pos'][nm] + lay['size'][nm]
    n_pad = -(-(lay['small'] + LANE) // bn) * bn
    cols.append(jnp.zeros(w_in.shape[:-1] + (n_pad - cur,), w_in.dtype))
    return jnp.concatenate(cols, axis=-1).astype(BF16)


def _proj_part(p, lay, nm, b, t):
    return p[:, lay['pos'][nm]:lay['pos'][nm] + lay['size'][nm]].reshape(b, t, lay['size'][nm])


def _prompt_mixer(p, lay, heads, kv_b, idx_dim, b, t, conv_w, a_log, dt_bias, gdn_g):
    h_a, h_b, h_c = heads
    pos_ids = jnp.arange(t, dtype=jnp.int32)
    tab_head = _rope_tables(pos_ids, HEAD_DIM, LANE)
    tab_idx = _rope_tables(pos_ids, idx_dim, LANE)

    q, kf, kh, vh, iqlo, iqhi, kif, ki2 = _dsa_prep(p, lay, t, tab_head + tab_idx)
    ob = _dsa_prompt(q, kh, vh, iqlo, iqhi, ki2, p, lay, b, t, kv_b, idx_dim)

    n_conv = lay['size']['qkv_a']
    qn, kn, vv = _gdn_prep(p, lay, t, jnp.zeros((b, 8, n_conv), F32), conv_w)
    c = min(CHUNK, t)
    aa = _proj_part(p, lay, 'aa', b, t)
    ba = _proj_part(p, lay, 'ba', b, t)
    g = -jnp.exp(a_log) * jax.nn.softplus(aa + dt_bias)
    oa, s_gdn = _gdn(qn, kn, vv, _gdn_gate_rows(g, b, t, h_a, c), _gdn_gate_rows(jax.nn.sigmoid(ba), b, t, h_a, c),
                     p, lay, gdn_g, jnp.zeros((b, h_a, HEAD_DIM, HEAD_DIM), F32), b, t)

    oc, s_ret = _retention(p, lay, tab_head, jnp.zeros((b, h_c, HEAD_DIM, HEAD_DIM), F32), b, t,
                           hp=4, chunk=min(2 * CHUNK, t))

    o = jnp.concatenate([oa, ob, oc], axis=-1)
    vb = _proj_part(p, lay, 'vb', b, t).reshape(b, t, kv_b, HEAD_DIM)
    conv_new = _proj_part(p, lay, 'qkv_a', b, t)[:, t - (CONV_W - 1):]
    state = (kf.reshape(b, t, kv_b, HEAD_DIM), vb, kif.reshape(b, t, idx_dim), s_gdn, conv_new, s_ret)
    return o, state


def _sample_mixer(p, lay, heads, kv_b, idx_dim, b, t, conv_w, a_log, dt_bias, gdn_g, past, layer):
    h_a, h_b, h_c = heads
    cache_k, cache_v, pool_ki2, state_gdn, state_gdn_conv, state_ret, page_table = past
    page = cache_k.shape[2]
    n_past = page_table.shape[1] * page
    m = b * t
    assert t <= CHUNK and t >= CONV_W - 1
    pos_ids = n_past + jnp.arange(t, dtype=jnp.int32)
    per_row = lambda tabs: tuple(jnp.tile(x, (b, 1)) for x in tabs)
    tables = per_row(_rope_tables(pos_ids, HEAD_DIM, LANE)) + per_row(_rope_tables(pos_ids, idx_dim, LANE))
    q, kf, kh, vh, iqlo, iqhi, kif, ki2 = _dsa_prep(p, lay, m, tables)

    f32 = lambda x: x.astype(F32)
    to_page = lambda x: jnp.pad(x.reshape(b, t, -1), ((0, 0), (0, page - t), (0, 0)))
    sm = p[:, lay['small']:lay['small'] + LANE]
    iw_lane = lay['pos']['iw'] - lay['small']
    scores_past = _dsa_s_scores(page_table, f32(iqlo), f32(iqhi), sm, pool_ki2, layer, t, idx_dim, iw_lane,
                                PAGES_PER_STEP)
    own = jnp.arange(b, dtype=jnp.int32).reshape(b, 1)
    scores_new = _dsa_s_scores(own, f32(iqlo), f32(iqhi), sm, to_page(f32(ki2))[None], 0, t, idx_dim, iw_lane, 1)
    k_sel = max(1, min(TOPK_MAX, (n_past + t) // 4))
    vb = _proj_part(p, lay, 'vb', b, t)
    ob = _dsa_s_attn(page_table, scores_past, scores_new, f32(q), to_page(kf), to_page(vb), cache_k, cache_v,
                     layer, t, kv_b, k_sel, PAGES_PER_STEP)

    tp = CHUNK
    lead = tp - t
    padl = lambda x: jnp.pad(x.reshape(b, t, -1), ((0, 0), (lead, 0), (0, 0)))
    rows = lambda x: padl(x).reshape(b * tp, -1)
    p_pad = rows(p)
    buf8 = jnp.pad(state_gdn_conv[layer], ((0, 0), (8 - (CONV_W - 1), 0), (0, 0)))
    qn, kn, vv = _gdn_prep(p, lay, t, buf8, conv_w)
    aa = _proj_part(p, lay, 'aa', b, t)
    ba = _proj_part(p, lay, 'ba', b, t)
    g = -jnp.exp(a_log) * jax.nn.softplus(aa + dt_bias)
    oa, s_gdn = _gdn(rows(qn), rows(kn), rows(vv), _gdn_gate_rows(padl(g), b, tp, h_a, tp),
                     _gdn_gate_rows(padl(jax.nn.sigmoid(ba)), b, tp, h_a, tp), p_pad, lay, gdn_g,
                     state_gdn[layer], b, tp)
    log_gamma = jnp.log1p(-jnp.exp2(RET_GAMMA_BASE - jnp.arange(h_c, dtype=F32)))
    s_ret0 = state_ret[layer] * jnp.exp(-lead * log_gamma)[None, :, None, None]
    tab_pad = _rope_tables(n_past - lead + jnp.arange(tp, dtype=jnp.int32), HEAD_DIM, LANE)
    oc, s_ret = _retention(p_pad, lay, tab_pad, s_ret0, b, tp, hp=2, chunk=tp)
    tail = lambda x: x.reshape(b, tp, -1)[:, lead:].reshape(m, -1)

    o = jnp.concatenate([tail(oa), ob.astype(BF16), tail(oc)], axis=-1)
    conv_new = _proj_part(p, lay, 'qkv_a', b, t)[:, t - (CONV_W - 1):]
    state = (kf.reshape(b, t, kv_b, HEAD_DIM), vb.reshape(b, t, kv_b, HEAD_DIM), kif.reshape(b, t, idx_dim),
             s_gdn, conv_new, s_ret)
    return o, state


def _forward(xs, mods, weights, past, cfg):
    (norm_pre, norm_post, w_in_r, gdn_conv_w, gdn_a_log, gdn_dt_bias, gdn_norm,
     w_out, ffn_w_gate, ffn_w_up, ffn_w_down) = weights
    heads, lay, kv_b, idx_dim = cfg
    depth = w_out.shape[0]
    d = xs[0].shape[-1]
    bt = [x.shape[:2] for x in xs]
    m = bt[0][0] * bt[0][1]
    bm = min(1024, m)
    bm_down = min(256, m)
    hs = [x.reshape(-1, d) for x in xs]
    vec = lambda g, l, s, j: mods[g][l, :, s, j].reshape(bt[g][0], 1, d)
    gpre = lambda l, s: norm_pre[l, s].reshape(1, d)
    gpost = lambda l, s: norm_post[l, s].reshape(1, d)

    def resid(ys, l, s, nxt, res_w):
        out = [_resid(hs[g], ys[g], vec(g, l, s, 2), gpost(l, s),
                      None if nxt is None else (gpre(*nxt), vec(g, *nxt, 1), vec(g, *nxt, 0)), bt[g][1], res_w)
               for g in range(2)]
        return [o[0] for o in out], [o[1] for o in out]

    def ffn(us, l, j):
        hf = _ffn_in(us[0], us[1], ffn_w_gate, ffn_w_up, (l, j), bm=bm, bn=MXU_COLS)
        return _matmul(hf[0], hf[1], ffn_w_down, (l, j), bm=bm_down, bn=2 * MXU_COLS, out_dtype=F32,
                       name="ffn_down", single_buffer_w=True)

    states = ([], [])
    us = [_norm_mod(hs[g], gpre(0, 0), vec(g, 0, 0, 1), vec(g, 0, 0, 0), bt[g][1]) for g in range(2)]
    for l in range(depth):
        hs, us = resid(ffn(us, l, 0), l, 0, (l, 1), FFN_RES)

        ps = _matmul(us[0], us[1], w_in_r, (l,), bm=bm, bn=2 * MXU_COLS, out_dtype=F32, name="in_proj")
        mix = (gdn_conv_w[l], gdn_a_log[l], gdn_dt_bias[l], gdn_norm[l])
        o_p, st_p = _prompt_mixer(ps[0], lay, heads, kv_b, idx_dim, *bt[0], *mix)
        o_s, st_s = _sample_mixer(ps[1], lay, heads, kv_b, idx_dim, *bt[1], *mix, past, l)
        states[0].append(st_p)
        states[1].append(st_s)
        ys = _matmul(o_p, o_s, w_out, (l,), bm=bm, bn=MXU_COLS, out_dtype=F32, name="out_proj")
        hs, us = resid(ys, l, 1, (l, 2), 1.0)

        hs, us = resid(ffn(us, l, 1), l, 2, (l + 1, 0) if l + 1 < depth else None, FFN_RES)
    new_state = [[jnp.stack([s[i] for s in states[g]]) for i in range(6)] for g in range(2)]
    return [hs[g].reshape(bt[g] + (d,)) for g in range(2)], new_state


def kernel(x_prompt, x_sample, cache_k, cache_v, cache_kidx, state_gdn, state_gdn_conv, state_ret, page_table,
           c_prompt, c_sample, w_ada, b_ada, norm_pre, norm_post, w_in, gdn_conv_w, gdn_a_log, gdn_dt_bias,
           gdn_norm, w_out, ffn_w_gate, ffn_w_up, ffn_w_down):
    d = x_prompt.shape[-1]
    depth = w_ada.shape[0]
    kv_b = cache_k.shape[3]
    idx_dim = cache_kidx.shape[-1]
    heads, lay = _in_proj_layout(d, kv_b, idx_dim)
    assert lay['n_src'] == w_in.shape[-1] and 2 * idx_dim == LANE
    bp, bs = c_prompt.shape[0], c_sample.shape[0]

    rows = -(-(bp + bs) // 8) * 8
    c_rows = jnp.concatenate([c_prompt, c_sample, jnp.zeros((rows - bp - bs, d), F32)], axis=0)
    mod = _adaln(c_rows, w_ada, b_ada).reshape(depth, rows, N_SUB, 3, d)

    w_in_r = _rearrange_w_in(w_in, lay, 2 * MXU_COLS)
    weights = (norm_pre, norm_post, w_in_r, gdn_conv_w, gdn_a_log, gdn_dt_bias, gdn_norm,
               w_out, ffn_w_gate, ffn_w_up, ffn_w_down)
    cfg = (heads, lay, kv_b, idx_dim)
    pool_ki2 = jnp.concatenate([cache_kidx, cache_kidx], axis=-1)
    past = (cache_k, cache_v, pool_ki2, state_gdn, state_gdn_conv, state_ret, page_table)
    (y_p, y_s), (st_p, st_s) = _forward((x_prompt, x_sample), (mod[:, :bp], mod[:, bp:bp + bs]), weights, past, cfg)
    return (y_p, y_s, *st_p, *st_s)
```

```python
import functools

import jax
import jax.numpy as jnp
from jax import lax
from jax.experimental import pallas as pl
from jax.experimental.pallas import tpu as pltpu

F32 = jnp.float32
BF16 = jnp.bfloat16

HEAD_DIM = 128
IDX_HEADS = 32
TOPK_MAX = 256
CONV_W = 4
FFN_RES = 0.5
N_SUB = 3
CHUNK = 64
Q_BLOCK = 128
ROPE_THETA = 10000.0
RET_GAMMA_BASE = -5.0
EPS = 1e-6
L2_EPS = 1e-6

V7X_VMEM_BYTES = 64 * 1024 * 1024
V7X_VMEM_BUDGET = V7X_VMEM_BYTES - 8 * 1024 * 1024
LANE = 128
MXU_COLS = 256


def _vmem_limit(block_bytes, scratch_bytes):
    need = 2 * block_bytes + scratch_bytes + 8 * 1024 * 1024
    assert need <= V7X_VMEM_BUDGET, need
    return need


def _nbytes(shape, dtype):
    n = 1
    for s in shape:
        n *= s
    return n * jnp.dtype(dtype).itemsize


def _adaln_kernel(c_ref, w_ref, b_ref, o_ref):
    c = c_ref[...]
    cs = (c * jax.nn.sigmoid(c)).astype(BF16)
    w = w_ref[...].astype(BF16)
    o_ref[...] = jnp.dot(cs, w, preferred_element_type=F32) + b_ref[...]


def _adaln(c_rows, w_ada, b_ada):
    depth, d, n = w_ada.shape
    rows = c_rows.shape[0]
    bn = 1024
    assert n % bn == 0
    blocks = _nbytes((rows, d), F32) + _nbytes((d, bn), F32) + _nbytes((1, bn), F32) + _nbytes((rows, bn), F32)
    return pl.pallas_call(
        _adaln_kernel,
        grid=(depth, n // bn),
        in_specs=[
            pl.BlockSpec((rows, d), lambda l, j: (0, 0)),
            pl.BlockSpec((None, d, bn), lambda l, j: (l, 0, j)),
            pl.BlockSpec((None, 1, bn), lambda l, j: (l, 0, j)),
        ],
        out_specs=pl.BlockSpec((None, rows, bn), lambda l, j: (l, 0, j)),
        out_shape=jax.ShapeDtypeStruct((depth, rows, n), F32),
        compiler_params=pltpu.CompilerParams(
            dimension_semantics=("arbitrary", "arbitrary"),
            vmem_limit_bytes=_vmem_limit(blocks, _nbytes((d, bn), BF16))),
        name="adaln",
    )(c_rows, w_ada, b_ada.reshape(depth, 1, n))


def _mm_kernel(x_ref, xs_ref, w_ref, o_ref, os_ref, *scratch, cast_w):
    first = pl.program_id(1) == 0
    if cast_w:
        (wb_ref,) = scratch

        @pl.when(first)
        def _():
            wb_ref[...] = w_ref[...].astype(BF16)

        w_src = wb_ref
    else:
        w_src = w_ref

    @pl.when(first)
    def _():
        os_ref[...] = jnp.dot(xs_ref[...], w_src[...], preferred_element_type=F32).astype(os_ref.dtype)

    o_ref[...] = jnp.dot(x_ref[...], w_src[...], preferred_element_type=F32).astype(o_ref.dtype)


def _matmul(x, xs, w, w_index, *, bm, bn, out_dtype, name, single_buffer_w=False):
    m, k = x.shape
    ms = xs.shape[0]
    n = w.shape[-1]
    assert w.shape[-2] == k and m % bm == 0 and n % bn == 0, (x.shape, w.shape, bm, bn)
    cast_w = w.dtype != BF16
    lead = len(w_index)
    scratch = [pltpu.VMEM((k, bn), BF16)] if cast_w else []
    w_bytes = _nbytes((k, bn), w.dtype)
    blocks = (_nbytes((bm, k), x.dtype) + _nbytes((bm, bn), out_dtype) + _nbytes((ms, bn), out_dtype)
              + (0 if single_buffer_w else w_bytes))
    fixed = ((_nbytes((k, bn), BF16) if cast_w else 0) + (w_bytes if single_buffer_w else 0)
             + _nbytes((ms, k), xs.dtype))
    w_mode = dict(pipeline_mode=pl.Buffered(1)) if single_buffer_w else {}
    return pl.pallas_call(
        functools.partial(_mm_kernel, cast_w=cast_w),
        grid=(n // bn, m // bm),
        in_specs=[
            pl.BlockSpec((bm, k), lambda j, i: (i, 0)),
            pl.BlockSpec((ms, k), lambda j, i: (0, 0), pipeline_mode=pl.Buffered(1)),
            pl.BlockSpec((None,) * lead + (k, bn), lambda j, i: tuple(w_index) + (0, j), **w_mode),
        ],
        out_specs=[pl.BlockSpec((bm, bn), lambda j, i: (i, j)), pl.BlockSpec((ms, bn), lambda j, i: (0, j))],
        out_shape=[jax.ShapeDtypeStruct((m, n), out_dtype), jax.ShapeDtypeStruct((ms, n), out_dtype)],
        scratch_shapes=scratch,
        compiler_params=pltpu.CompilerParams(
            dimension_semantics=("arbitrary", "arbitrary"),
            vmem_limit_bytes=_vmem_limit(blocks, fixed)),
        name=name,
    )(x, xs, w)


def _ffn_in_kernel(x_ref, xs_ref, wg_ref, wu_ref, o_ref, os_ref, wgb_ref, wub_ref):
    def swiglu(x):
        g = jnp.dot(x, wgb_ref[...], preferred_element_type=F32)
        u = jnp.dot(x, wub_ref[...], preferred_element_type=F32)
        return g * jax.nn.sigmoid(g) * u

    @pl.when(pl.program_id(1) == 0)
    def _():
        wgb_ref[...] = wg_ref[...].astype(BF16)
        wub_ref[...] = wu_ref[...].astype(BF16)
        os_ref[...] = swiglu(xs_ref[...]).astype(os_ref.dtype)

    o_ref[...] = swiglu(x_ref[...]).astype(o_ref.dtype)


def _ffn_in(x, xs, wg, wu, w_index, *, bm, bn):
    m, k = x.shape
    ms = xs.shape[0]
    n = wg.shape[-1]
    assert m % bm == 0 and n % bn == 0
    lead = len(w_index)
    wspec = pl.BlockSpec((None,) * lead + (k, bn), lambda j, i: tuple(w_index) + (0, j))
    blocks = _nbytes((bm, k), BF16) + 2 * _nbytes((k, bn), F32) + _nbytes((bm, bn), BF16) + _nbytes((ms, bn), BF16)
    return pl.pallas_call(
        _ffn_in_kernel,
        grid=(n // bn, m // bm),
        in_specs=[pl.BlockSpec((bm, k), lambda j, i: (i, 0)),
                  pl.BlockSpec((ms, k), lambda j, i: (0, 0), pipeline_mode=pl.Buffered(1)), wspec, wspec],
        out_specs=[pl.BlockSpec((bm, bn), lambda j, i: (i, j)), pl.BlockSpec((ms, bn), lambda j, i: (0, j))],
        out_shape=[jax.ShapeDtypeStruct((m, n), BF16), jax.ShapeDtypeStruct((ms, n), BF16)],
        scratch_shapes=[pltpu.VMEM((k, bn), BF16), pltpu.VMEM((k, bn), BF16)],
        compiler_params=pltpu.CompilerParams(
            dimension_semantics=("arbitrary", "arbitrary"),
            vmem_limit_bytes=_vmem_limit(blocks, 2 * _nbytes((k, bn), BF16) + _nbytes((ms, k), BF16))),
        name="ffn_in",
    )(x, xs, wg, wu)


def _rms_rows(x):
    return x * lax.rsqrt(jnp.mean(x * x, axis=-1, keepdims=True) + EPS)


def _norm_mod_kernel(h_ref, gpre_ref, scale_ref, shift_ref, u_ref):
    u = _rms_rows(h_ref[...]) * gpre_ref[...] * (1.0 + scale_ref[...]) + shift_ref[...]
    u_ref[...] = u.astype(u_ref.dtype)


def _resid_kernel(h_ref, y_ref, gate_ref, gpost_ref, *rest, res_w, with_next):
    yn = _rms_rows(y_ref[...]) * gpost_ref[...]
    h = h_ref[...] + res_w * gate_ref[...] * yn
    if with_next:
        gpre_ref, scale_ref, shift_ref, hn_ref, u_ref = rest
        u = _rms_rows(h) * gpre_ref[...] * (1.0 + scale_ref[...]) + shift_ref[...]
        u_ref[...] = u.astype(u_ref.dtype)
    else:
        (hn_ref,) = rest
    hn_ref[...] = h


def _row_block(rows_per_seq):
    return min(256, rows_per_seq)


def _seq_vec_spec(d, bm, rows_per_seq):
    return pl.BlockSpec((None, 1, d), lambda i: (i * bm // rows_per_seq, 0, 0))


def _norm_mod(h, gpre, scale, shift, rows_per_seq):
    m, d = h.shape
    bm = _row_block(rows_per_seq)
    row = pl.BlockSpec((bm, d), lambda i: (i, 0))
    vec = pl.BlockSpec((1, d), lambda i: (0, 0))
    sv = _seq_vec_spec(d, bm, rows_per_seq)
    return pl.pallas_call(
        _norm_mod_kernel,
        grid=(m // bm,),
        in_specs=[row, vec, sv, sv],
        out_specs=row,
        out_shape=jax.ShapeDtypeStruct((m, d), BF16),
        compiler_params=pltpu.CompilerParams(dimension_semantics=("arbitrary",)),
        name="norm_mod",
    )(h, gpre, scale, shift)


def _resid(h, y, gate, gpost, nxt, rows_per_seq, res_w):
    m, d = h.shape
    bm = _row_block(rows_per_seq)
    row = pl.BlockSpec((bm, d), lambda i: (i, 0))
    vec = pl.BlockSpec((1, d), lambda i: (0, 0))
    sv = _seq_vec_spec(d, bm, rows_per_seq)
    with_next = nxt is not None
    in_specs = [row, row, sv, vec] + ([vec, sv, sv] if with_next else [])
    out_specs = [row, row] if with_next else [row]
    out_shape = [jax.ShapeDtypeStruct((m, d), F32)] + ([jax.ShapeDtypeStruct((m, d), BF16)] if with_next else [])
    outs = pl.pallas_call(
        functools.partial(_resid_kernel, res_w=res_w, with_next=with_next),
        grid=(m // bm,),
        in_specs=in_specs,
        out_specs=out_specs,
        out_shape=out_shape,
        compiler_params=pltpu.CompilerParams(dimension_semantics=("arbitrary",)),
        name="resid",
    )(h, y, gate, gpost, *(nxt if with_next else ()))
    return (outs[0], outs[1]) if with_next else (outs[0], None)


def _l2_norm(x):
    return x * lax.rsqrt(jnp.sum(x * x, axis=-1, keepdims=True) + L2_EPS)


def _rope(x, pos):
    half = x.shape[-1] // 2
    inv = ROPE_THETA ** (-jnp.arange(half, dtype=F32) / half)
    ang = pos.astype(F32)[:, None] * inv[None, :]
    cos = jnp.cos(ang)[None, :, None, :]
    sin = jnp.sin(ang)[None, :, None, :]
    x1, x2 = x[..., :half], x[..., half:]
    return jnp.concatenate([x1 * cos - x2 * sin, x2 * cos + x1 * sin], axis=-1)


def _causal_conv(x, buf, w):
    t = x.shape[1]
    xp = jnp.concatenate([buf, x], axis=1)
    y = xp[:, 0:t] * w[0]
    for j in range(1, CONV_W):
        y = y + xp[:, j:j + t] * w[j]
    return jax.nn.silu(y), xp[:, t:]


def _to_chunks(x, c, pad):
    x = jnp.pad(x, [(0, 0), (0, pad)] + [(0, 0)] * (x.ndim - 2))
    b, tp = x.shape[:2]
    x = x.reshape((b, tp // c, c) + x.shape[2:])
    return jnp.transpose(x, (1, 0, 3, 2) + tuple(range(4, x.ndim)))


def _from_chunks(o, t):
    n, b, h, c, d = o.shape
    return jnp.transpose(o, (1, 0, 3, 2, 4)).reshape(b, n * c, h, d)[:, :t]


def _chunk_decay(gc):
    c = gc.shape[-1]
    tri = jnp.tril(jnp.ones((c, c), bool))
    diff = gc[..., :, None] - gc[..., None, :]
    return jnp.where(tri, jnp.exp(jnp.where(tri, diff, 0.0)), 0.0)


def _gated_delta_rule(q, k, v, g, beta, s0):
    t, dk = q.shape[1], q.shape[-1]
    dv = v.shape[-1]
    c = min(CHUNK, t)
    pad = (-t) % c
    q = _to_chunks(q * dk ** -0.5, c, pad)
    k = _to_chunks(k, c, pad)
    v = _to_chunks(v, c, pad)
    beta = _to_chunks(beta, c, pad)
    gc = jnp.cumsum(_to_chunks(g, c, pad), axis=-1)
    decay = _chunk_decay(gc)
    kb = k * beta[..., None]
    strict = jnp.tril(jnp.ones((c, c), bool), -1)
    lower = jnp.where(strict, jnp.einsum('nbhik,nbhjk->nbhij', kb, k) * decay, 0.0) + jnp.eye(c, dtype=F32)
    rhs = jnp.concatenate([v * beta[..., None], kb * jnp.exp(gc)[..., None]], axis=-1)
    sol = lax.linalg.triangular_solve(lower, rhs, left_side=True, lower=True, unit_diagonal=True)
    u, w = sol[..., :dv], sol[..., dv:]

    def step(S, xs):
        qc, kc, uc, wc, gcc, dc = xs
        v_new = uc - jnp.einsum('bhik,bhkv->bhiv', wc, S)
        intra = jnp.einsum('bhik,bhjk->bhij', qc, kc) * dc
        o = (jnp.einsum('bhik,bhkv->bhiv', qc * jnp.exp(gcc)[..., None], S)
             + jnp.einsum('bhij,bhjv->bhiv', intra, v_new))
        gl = gcc[..., -1:]
        S = S * jnp.exp(gl)[..., None] + jnp.einsum('bhjk,bhjv->bhkv', kc * jnp.exp(gl - gcc)[..., None], v_new)
        return S, o

    S, o = lax.scan(step, s0, (q, k, u, w, gc, decay))
    return _from_chunks(o, t), S


def _scan_retention(q, k, v, logd, s0):
    t = q.shape[1]
    c = min(CHUNK, t)
    pad = (-t) % c
    q = _to_chunks(q, c, pad)
    k = _to_chunks(k, c, pad)
    v = _to_chunks(v, c, pad)
    gc = jnp.cumsum(_to_chunks(logd, c, pad), axis=-1)
    decay = _chunk_decay(gc)

    def step(S, xs):
        qc, kc, vc, gcc, dc = xs
        intra = jnp.einsum('bhik,bhjk->bhij', qc, kc) * dc
        o = (jnp.einsum('bhik,bhkv->bhiv', qc * jnp.exp(gcc)[..., None], S)
             + jnp.einsum('bhij,bhjv->bhiv', intra, vc))
        gl = gcc[..., -1:]
        S = S * jnp.exp(gl)[..., None] + jnp.einsum('bhjk,bhjv->bhkv', kc * jnp.exp(gl - gcc)[..., None], vc)
        return S, o

    S, o = lax.scan(step, s0, (q, k, v, gc, decay))
    return _from_chunks(o, t), S


def _gather_rows(rows, idx):
    return jax.vmap(lambda r, i: r[i])(rows, idx)


def _paged_gather(pool, page_table, new_rows, idx):
    page = pool.shape[1]
    n_past = page_table.shape[1] * page
    t = new_rows.shape[1]
    pidx = jnp.minimum(idx, n_past - 1)
    phys = jax.vmap(lambda pt, i: pt[i // page])(page_table, pidx)
    past_rows = pool[phys, pidx % page]
    new_r = _gather_rows(new_rows, jnp.clip(idx - n_past, 0, t - 1))
    is_past = (idx < n_past).reshape(idx.shape + (1,) * (past_rows.ndim - idx.ndim))
    return jnp.where(is_past, past_rows, new_r)


def _sparse_attention(q, iq, iw, kidx_all, qpos, gather_kv, k_sel):
    b, t, h_b, _ = q.shape
    qb_sz = min(Q_BLOCK, t)
    pad = (-t) % qb_sz
    nb = (t + pad) // qb_sz

    def blocks(a):
        a = jnp.pad(a, [(0, 0), (0, pad)] + [(0, 0)] * (a.ndim - 2))
        return jnp.moveaxis(a.reshape((b, nb, qb_sz) + a.shape[2:]), 1, 0)

    qpos_b = jnp.pad(qpos, (0, pad), mode='edge').reshape(nb, qb_sz)
    kpos = jnp.arange(kidx_all.shape[1], dtype=jnp.int32)

    def one_block(args):
        qb, iqb, iwb, pb = args
        rel = jax.nn.relu(jnp.einsum('bqhd,bsd->bqhs', iqb, kidx_all))
        score = jnp.einsum('bqhs,bqh->bqs', rel, iwb)
        visible = kpos[None, :] <= pb[:, None]
        score = jnp.where(visible[None], score, -jnp.inf)
        _, idx = lax.top_k(score, k_sel)
        valid = idx <= pb[None, :, None]
        kr, vr = gather_kv(idx)
        kv_b = kr.shape[3]
        qg = qb.reshape(b, qb_sz, kv_b, h_b // kv_b, HEAD_DIM)
        logits = jnp.einsum('bqngd,bqsnd->bqngs', qg, kr) * HEAD_DIM ** -0.5
        logits = jnp.where(valid[:, :, None, None, :], logits, -jnp.inf)
        p = jax.nn.softmax(logits, axis=-1)
        o = jnp.einsum('bqngs,bqsnd->bqngd', p, vr)
        return o.reshape(b, qb_sz, h_b, HEAD_DIM)

    o = lax.map(one_block, (blocks(q), blocks(iq), blocks(iw), qpos_b))
    return jnp.moveaxis(o, 0, 1).reshape(b, nb * qb_sz, h_b, HEAD_DIM)[:, :t]


def _token_mixer(parts, dims, conv_w, a_log, dt_bias, gdn_g, past):
    (qkv_a, za, qb, kb, vb, iq, qc, kc, vc, gc, aa, ba, ik, iw) = parts
    h_a, h_b, h_c, kv_b, idx_dim = dims
    b, t = qkv_a.shape[:2]
    if past is None:
        conv_buf = jnp.zeros((b, CONV_W - 1, qkv_a.shape[-1]), F32)
        s_gdn = jnp.zeros((b, h_a, HEAD_DIM, HEAD_DIM), F32)
        s_ret = jnp.zeros((b, h_c, HEAD_DIM, HEAD_DIM), F32)
        n_past = 0
    else:
        conv_buf, s_gdn, s_ret, pool_k, pool_v, pool_kidx, page_table = past
        n_past = page_table.shape[1] * pool_k.shape[1]
    pos = n_past + jnp.arange(t, dtype=jnp.int32)

    conv_out, conv_new = _causal_conv(qkv_a, conv_buf, conv_w)
    qa, ka, va = [x.reshape(b, t, h_a, HEAD_DIM) for x in jnp.split(conv_out, 3, axis=-1)]
    g = -jnp.exp(a_log) * jax.nn.softplus(aa + dt_bias)
    oa, s_gdn_new = _gated_delta_rule(_l2_norm(qa), _l2_norm(ka), va, g, jax.nn.sigmoid(ba), s_gdn)
    oa = _rms_rows(oa) * gdn_g * jax.nn.silu(za.reshape(b, t, h_a, HEAD_DIM))

    qb = _rope(qb.reshape(b, t, h_b, HEAD_DIM), pos)
    kb = _rope(kb.reshape(b, t, kv_b, HEAD_DIM), pos)
    vb = vb.reshape(b, t, kv_b, HEAD_DIM)
    iq = _rope(iq.reshape(b, t, IDX_HEADS, idx_dim), pos)
    ik = _rope(ik.reshape(b, t, 1, idx_dim), pos)[:, :, 0]
    iw = iw * (IDX_HEADS ** -0.5 * idx_dim ** -0.5)
    if past is None:
        kidx_all = ik
        gather_kv = lambda idx: (_gather_rows(kb, idx), _gather_rows(vb, idx))
    else:
        past_kidx = pool_kidx[page_table].reshape(b, n_past, idx_dim)
        kidx_all = jnp.concatenate([past_kidx, ik], axis=1)
        gather_kv = lambda idx: (_paged_gather(pool_k, page_table, kb, idx),
                                 _paged_gather(pool_v, page_table, vb, idx))
    k_sel = max(1, min(TOPK_MAX, kidx_all.shape[1] // 4))
    ob = _sparse_attention(qb, iq, iw, kidx_all, pos, gather_kv, k_sel)

    qc = _rope(qc.reshape(b, t, h_c, HEAD_DIM), pos)
    kc = _rope(kc.reshape(b, t, h_c, HEAD_DIM), pos) * HEAD_DIM ** -0.5
    vc = vc.reshape(b, t, h_c, HEAD_DIM)
    log_gamma = jnp.log1p(-jnp.exp2(RET_GAMMA_BASE - jnp.arange(h_c, dtype=F32)))
    logd = jnp.broadcast_to(log_gamma, (b, t, h_c))
    oc, s_ret_new = _scan_retention(qc, kc, vc, logd, s_ret)
    oc = _rms_rows(oc) * jax.nn.silu(gc.reshape(b, t, h_c, HEAD_DIM))

    o = jnp.concatenate([oa.reshape(b, t, -1), ob.reshape(b, t, -1), oc.reshape(b, t, -1)], axis=-1)
    return o, (kb, vb, ik, s_gdn_new, conv_new, s_ret_new)


def _rope_tables(pos, head_dim, lanes):
    half = head_dim // 2
    inv = ROPE_THETA ** (-jnp.arange(half, dtype=F32) / half)
    ang = pos.astype(F32)[:, None] * inv[None, :]
    cos, sin = jnp.cos(ang), jnp.sin(ang)
    reps = lanes // head_dim
    return (jnp.tile(jnp.concatenate([cos, cos], axis=-1), (1, reps)),
            jnp.tile(jnp.concatenate([-sin, sin], axis=-1), (1, reps)))


def _rope_head(x, cos_t, sin_t):
    return x * cos_t + pltpu.roll(x, HEAD_DIM // 2, axis=1) * sin_t


def _rope_half_heads(x, cos_t, sin_t):
    lane = lax.broadcasted_iota(jnp.int32, x.shape, 1)
    fwd = pltpu.roll(x, 32, axis=1)
    bwd = pltpu.roll(x, LANE - 32, axis=1)
    return x * cos_t + jnp.where(lane % 64 < 32, bwd, fwd) * sin_t


def _dsa_prep_kernel(qb_ref, kb_ref, vb_ref, iq_ref, sm_ref, cos_ref, sin_ref, cosi_ref, sini_ref,
                     q_ref, kf_ref, kh_ref, vh_ref, iqlo_ref, iqhi_ref, kif_ref, ki2_ref):
    cos_t, sin_t = cos_ref[...], sin_ref[...]
    cosi_t, sini_t = cosi_ref[...], sini_ref[...]
    for h in range(qb_ref.shape[1] // HEAD_DIM):
        sl = slice(h * HEAD_DIM, (h + 1) * HEAD_DIM)
        q_ref[:, sl] = _rope_head(qb_ref[:, sl], cos_t, sin_t).astype(BF16)
    for h in range(kb_ref.shape[1] // HEAD_DIM):
        sl = slice(h * HEAD_DIM, (h + 1) * HEAD_DIM)
        kr = _rope_head(kb_ref[:, sl], cos_t, sin_t)
        kf_ref[:, sl] = kr
        kh_ref[:, sl] = kr.astype(BF16)
    vh_ref[...] = vb_ref[...].astype(BF16)
    lane = lax.broadcasted_iota(jnp.int32, cos_t.shape, 1)
    low = lane < 64
    for c in range(iq_ref.shape[1] // LANE):
        sl = slice(c * LANE, (c + 1) * LANE)
        r = _rope_half_heads(iq_ref[:, sl], cosi_t, sini_t)
        iqlo_ref[:, sl] = jnp.where(low, r, 0.0).astype(BF16)
        iqhi_ref[:, sl] = jnp.where(low, 0.0, r).astype(BF16)
    r = _rope_half_heads(sm_ref[...], cosi_t, sini_t)
    kif_ref[...] = r[:, :64]
    ki2_ref[...] = jnp.where(low, r, pltpu.roll(r, 64, axis=1)).astype(BF16)


def _col_spec(bt, width, off):
    assert off % width == 0, (off, width)
    return pl.BlockSpec((bt, width), lambda i: (i, off // width))


def _dsa_prep(p, lay, t, tables):
    m = p.shape[0]
    bt = min(256, t)
    nt = t // bt
    pos, wid = lay['pos'], lay['size']
    tab = pl.BlockSpec((bt, LANE), lambda i: (i % nt, 0))
    row = lambda w: pl.BlockSpec((bt, w), lambda i: (i, 0))
    wq, wk, wi = wid['qb'], wid['kb'], wid['iq']
    outs = pl.pallas_call(
        _dsa_prep_kernel,
        grid=(m // bt,),
        in_specs=[_col_spec(bt, wq, pos['qb']), _col_spec(bt, wk, pos['kb']), _col_spec(bt, wk, pos['vb']),
                  _col_spec(bt, wi, pos['iq']), _col_spec(bt, LANE, lay['small']), tab, tab, tab, tab],
        out_specs=[row(wq), row(wk), row(wk), row(wk), row(wi), row(wi), row(64), row(LANE)],
        out_shape=[jax.ShapeDtypeStruct((m, wq), BF16), jax.ShapeDtypeStruct((m, wk), F32),
                   jax.ShapeDtypeStruct((m, wk), BF16), jax.ShapeDtypeStruct((m, wk), BF16),
                   jax.ShapeDtypeStruct((m, wi), BF16), jax.ShapeDtypeStruct((m, wi), BF16),
                   jax.ShapeDtypeStruct((m, 64), F32), jax.ShapeDtypeStruct((m, LANE), BF16)],
        compiler_params=pltpu.CompilerParams(dimension_semantics=("arbitrary",)),
        name="dsa_prep",
    )(p, p, p, p, p, *tables)
    return outs


INT32_MIN = -2 ** 31


def _sortable_key(x):
    b = lax.bitcast_convert_type(x, jnp.int32)
    return b ^ ((b >> 31) & jnp.int32(0x7FFFFFFF))


def _kth_largest_key(keys_ref, k_sel):
    kf = jnp.float32(k_sel)

    def count_ge(t):
        return jnp.sum(jnp.where(keys_ref[...] >= t, 1.0, 0.0), axis=1, keepdims=True)

    rows = keys_ref.shape[0]
    zero = jnp.zeros((rows, 1), jnp.int32)
    t0 = jnp.where(count_ge(zero) >= kf, zero, jnp.int32(INT32_MIN))

    def body(i, t):
        cand = t | (jnp.int32(1) << (30 - i))
        return jnp.where(count_ge(cand) >= kf, cand, t)

    return lax.fori_loop(0, 31, body, t0)


def _tie_cut(count_tied_below, need, n_index):
    bits = max(1, (n_index - 1).bit_length())

    def body(i, base):
        cand = base + (jnp.int32(1) << (bits - 1 - i))
        return jnp.where(count_tied_below(cand) < need, cand, base)

    return lax.fori_loop(0, bits, body, jnp.zeros_like(need, dtype=jnp.int32)) + 1


def _chosen(keys, pos, thr, cut):
    bound = jnp.where(keys > thr, jnp.int32(2 ** 31 - 1), jnp.where(keys == thr, cut, jnp.int32(0)))
    return pos < bound


def _dsa_kernel(q_ref, k_ref, v_ref, iqlo_ref, iqhi_ref, ki2_ref, sm_ref, o_ref, score_ref, keys_ref, cut_ref,
                *, k_sel, kv_b, idx_dim, iw_lane, q_block0):
    qb = q_ref.shape[0]
    n_keys = k_ref.shape[0]
    group = q_ref.shape[1] // HEAD_DIM // kv_b
    q0 = (q_block0 + pl.program_id(1)) * qb
    ki2 = ki2_ref[...]
    w_scale = IDX_HEADS ** -0.5 * idx_dim ** -0.5
    heads_per_dot = 8
    score_ref[...] = jnp.zeros_like(score_ref)
    for gi in range(IDX_HEADS // heads_per_dot):
        rows = []
        for c in range(heads_per_dot // 2):
            sl = slice((gi * heads_per_dot // 2 + c) * LANE, (gi * heads_per_dot // 2 + c + 1) * LANE)
            rows.append(iqlo_ref[:, sl])
            rows.append(iqhi_ref[:, sl])
        lhs = jnp.concatenate(rows, axis=0)
        d = lax.dot_general(lhs, ki2, (((1,), (1,)), ((), ())), preferred_element_type=F32)
        acc = score_ref[...]
        for hh in range(heads_per_dot):
            lane = iw_lane + gi * heads_per_dot + hh
            w = sm_ref[:, lane:lane + 1] * w_scale
            acc = acc + jnp.maximum(d[hh * qb:(hh + 1) * qb], 0.0) * w
        score_ref[...] = acc

    qpos = q0 + lax.broadcasted_iota(jnp.int32, (qb, n_keys), 0)
    kpos = lax.broadcasted_iota(jnp.int32, (qb, n_keys), 1)
    visible = kpos <= qpos
    keys_ref[...] = jnp.where(visible, _sortable_key(score_ref[...] + 0.0), jnp.int32(INT32_MIN))
    thr = _kth_largest_key(keys_ref, k_sel)
    count = lambda mask: jnp.sum(jnp.where(mask, 1.0, 0.0), axis=1, keepdims=True)
    need = jnp.float32(k_sel) - count(keys_ref[...] > thr)
    surplus = jnp.max(count(keys_ref[...] == thr) - need)
    cut_ref[...] = jnp.full(cut_ref.shape, n_keys, jnp.int32)

    @pl.when(surplus > 0.5)
    def _():
        tied_pos = lambda: jnp.where(keys_ref[...] == thr, kpos, jnp.int32(n_keys))
        cut = _tie_cut(lambda c: count(tied_pos() < c), need, n_keys)
        cut_ref[...] = jnp.broadcast_to(cut, cut_ref.shape)

    neg = jnp.float32(-jnp.inf)
    keys = keys_ref[...]
    score_ref[...] = jnp.where(visible, jnp.where(_chosen(keys, kpos, thr, cut_ref[:, 0:1]), 0.0, neg), neg)

    scale = HEAD_DIM ** -0.5
    for n in range(kv_b):
        q3 = jnp.concatenate(
            [q_ref[:, (n * group + g) * HEAD_DIM:(n * group + g + 1) * HEAD_DIM] for g in range(group)], axis=0)
        kn = k_ref[:, n * HEAD_DIM:(n + 1) * HEAD_DIM]
        lg = lax.dot_general(q3, kn, (((1,), (1,)), ((), ())), preferred_element_type=F32) * scale
        lg = lg.reshape(group, qb, n_keys) + score_ref[...][None]
        mx = jnp.max(lg, axis=-1, keepdims=True)
        pr = jnp.exp(lg - mx)
        den = jnp.sum(pr, axis=-1, keepdims=True)
        o = jnp.dot(pr.reshape(group * qb, n_keys).astype(BF16), v_ref[:, n * HEAD_DIM:(n + 1) * HEAD_DIM],
                    preferred_element_type=F32)
        o = o / den.reshape(group * qb, 1)
        for g in range(group):
            sl = slice((n * group + g) * HEAD_DIM, (n * group + g + 1) * HEAD_DIM)
            o_ref[:, sl] = o[g * qb:(g + 1) * qb].astype(o_ref.dtype)


def _dsa_prompt(q, kh, vh, iqlo, iqhi, ki2, p, lay, b, t, kv_b, idx_dim):
    m, wq = q.shape
    wk, wi = kh.shape[1], iqlo.shape[1]
    qb = min(Q_BLOCK, t)
    nq = t // qb
    k_sel = max(1, min(TOPK_MAX, t // 4))
    n_bands = next(n for n in (8, 4, 2, 1) if nq % n == 0)
    per_band = nq // n_bands
    kh3, vh3, ki3 = (x.reshape(b, t, x.shape[1]) for x in (kh, vh, ki2))
    outs = []
    for band in range(n_bands):
        first = band * per_band
        n_keys = (first + per_band) * qb
        qrow = lambda w, first=first: pl.BlockSpec((qb, w), lambda bi, qi: (bi * nq + first + qi, 0))
        seq = lambda w, n_keys=n_keys: pl.BlockSpec((None, n_keys, w), lambda bi, qi: (bi, 0, 0))
        small = pl.BlockSpec((qb, LANE), lambda bi, qi, first=first: (bi * nq + first + qi, lay['small'] // LANE))
        blocks = (_nbytes((qb, wq), BF16) * 2 + 2 * _nbytes((n_keys, wk), BF16) + 2 * _nbytes((qb, wi), BF16)
                  + _nbytes((n_keys, LANE), BF16) + _nbytes((qb, LANE), F32))
        temps = 2 * _nbytes((qb, n_keys), F32) + 2 * _nbytes((8 * qb, n_keys), F32)
        outs.append(pl.pallas_call(
            functools.partial(_dsa_kernel, k_sel=k_sel, kv_b=kv_b, idx_dim=idx_dim,
                              iw_lane=lay['pos']['iw'] - lay['small'], q_block0=first),
            grid=(b, per_band),
            in_specs=[qrow(wq), seq(wk), seq(wk), qrow(wi), qrow(wi), seq(LANE), small],
            out_specs=pl.BlockSpec((None, qb, wq), lambda bi, qi: (bi, qi, 0)),
            out_shape=jax.ShapeDtypeStruct((b, per_band * qb, wq), BF16),
            scratch_shapes=[pltpu.VMEM((qb, n_keys), F32), pltpu.VMEM((qb, n_keys), jnp.int32),
                            pltpu.VMEM((qb, LANE), jnp.int32)],
            compiler_params=pltpu.CompilerParams(
                dimension_semantics=("arbitrary", "arbitrary"),
                vmem_limit_bytes=_vmem_limit(blocks, temps)),
            name="dsa",
        )(q, kh3, vh3, iqlo, iqhi, ki3, p))
    return jnp.concatenate(outs, axis=1).reshape(m, wq)


def _gdn_prep_kernel(x_ref, prev_ref, buf_ref, w_ref, q_ref, k_ref, v_ref, xp_ref, *, blocks_per_seq):
    bt = x_ref.shape[0]
    first = pl.program_id(0) % blocks_per_seq == 0
    xp_ref[0:8, :] = jnp.where(first, buf_ref[...], prev_ref[...])
    xp_ref[8:, :] = x_ref[...]
    n_heads = q_ref.shape[1] // HEAD_DIM
    for part, out_ref in enumerate((q_ref, k_ref, v_ref)):
        for h in range(n_heads):
            src = slice((part * n_heads + h) * HEAD_DIM, (part * n_heads + h + 1) * HEAD_DIM)
            y = xp_ref[8 - (CONV_W - 1):8 - (CONV_W - 1) + bt, src] * w_ref[0:1, src]
            for j in range(1, CONV_W):
                y = y + xp_ref[8 - (CONV_W - 1) + j:8 - (CONV_W - 1) + j + bt, src] * w_ref[j:j + 1, src]
            y = y * jax.nn.sigmoid(y)
            if part < 2:
                y = y * lax.rsqrt(jnp.sum(y * y, axis=-1, keepdims=True) + L2_EPS)
            if part == 0:
                y = y * HEAD_DIM ** -0.5
            out_ref[:, h * HEAD_DIM:(h + 1) * HEAD_DIM] = y


def _gdn_prep(p, lay, t, buf8, conv_w):
    m = p.shape[0]
    bt = min(256, t)
    nt = t // bt
    c = lay['size']['qkv_a']
    off = lay['pos']['qkv_a']
    assert off % c == 0
    cb = off // c
    wq = c // 3
    row = pl.BlockSpec((bt, wq), lambda i: (i, 0))
    blocks = _nbytes((bt, c), F32) + 2 * _nbytes((8, c), F32) + _nbytes((CONV_W, c), F32) + 3 * _nbytes((bt, wq), F32)
    return pl.pallas_call(
        functools.partial(_gdn_prep_kernel, blocks_per_seq=nt),
        grid=(m // bt,),
        in_specs=[pl.BlockSpec((bt, c), lambda i: (i, cb)),
                  pl.BlockSpec((8, c), lambda i: (jnp.maximum(i * (bt // 8) - 1, 0), cb)),
                  pl.BlockSpec((None, 8, c), lambda i: (i // nt, 0, 0)),
                  pl.BlockSpec((CONV_W, c), lambda i: (0, 0))],
        out_specs=[row, row, row],
        out_shape=[jax.ShapeDtypeStruct((m, wq), F32)] * 3,
        scratch_shapes=[pltpu.VMEM((bt + 8, c), F32)],
        compiler_params=pltpu.CompilerParams(
            dimension_semantics=("arbitrary",),
            vmem_limit_bytes=_vmem_limit(blocks, _nbytes((bt + 8, c), F32))),
        name="gdn_prep",
    )(p, p, buf8, conv_w)


def _dot_hi(a, b):
    ah, bh = a.astype(BF16), b.astype(BF16)
    al = (a - ah.astype(F32)).astype(BF16)
    bl = (b - bh.astype(F32)).astype(BF16)
    dot = functools.partial(jnp.dot, preferred_element_type=F32)
    return dot(ah, bh) + (dot(al, bh) + dot(ah, bl))


def _dot_nt(a, b):
    return lax.dot_general(a, b, (((1,), (1,)), ((), ())), preferred_element_type=F32)


def _dot_tn(a, b):
    return lax.dot_general(a, b, (((0,), (0,)), ((), ())), preferred_element_type=F32)


def _gdn_kernel(q_ref, k_ref, v_ref, g_ref, beta_ref, z_ref, gn_ref, s0_ref, o_ref, s_ref, *, chunk):
    t = q_ref.shape[0]
    hp = q_ref.shape[1] // HEAD_DIM
    c = chunk
    ii = lax.broadcasted_iota(jnp.int32, (c, c), 0)
    jj = lax.broadcasted_iota(jnp.int32, (c, c), 1)
    tril, strict, eye = ii >= jj, ii > jj, ii == jj

    @pl.when(pl.program_id(1) == 0)
    def _():
        s_ref[...] = s0_ref[...]

    gn = gn_ref[...]

    heads = range(hp)
    cols = [slice(j * HEAD_DIM, (j + 1) * HEAD_DIM) for j in heads]

    def body(i, carry):
        r0 = pl.multiple_of(i * c, c)
        g_row = [g_ref[j, pl.ds(i, 1), :] for j in heads]
        gc_col = [jnp.sum(jnp.where(tril, g_row[j], 0.0), axis=1, keepdims=True) for j in heads]
        gc_row = [jnp.sum(jnp.where(eye, gc_col[j], 0.0), axis=0, keepdims=True) for j in heads]
        beta_col = [jnp.sum(jnp.where(eye, beta_ref[j, pl.ds(i, 1), :], 0.0), axis=1, keepdims=True) for j in heads]
        decay = [jnp.where(tril, jnp.exp(jnp.where(tril, gc_col[j] - gc_row[j], 0.0)), 0.0) for j in heads]
        k = [k_ref[pl.ds(r0, c), cols[j]] for j in heads]
        kb = [k[j] * beta_col[j] for j in heads]
        kh = [k[j].astype(BF16) for j in heads]
        a = [jnp.where(strict, _dot_nt(kb[j].astype(BF16), kh[j]) * decay[j], 0.0) for j in heads]
        inv = [jnp.where(eye, 1.0, 0.0) - a[j] for j in heads]
        pw = a
        span = 2
        while span < c:
            pw = [_dot_hi(pw[j], pw[j]) for j in heads]
            inv = [inv[j] + _dot_hi(inv[j], pw[j]) for j in heads]
            span *= 2
        egc = [jnp.exp(gc_col[j]) for j in heads]
        u = [_dot_hi(inv[j], v_ref[pl.ds(r0, c), cols[j]] * beta_col[j]) for j in heads]
        w = [_dot_hi(inv[j], kb[j] * egc[j]) for j in heads]
        s = [s_ref[j] for j in heads]
        sh = [s[j].astype(BF16) for j in heads]
        vnh = [(u[j] - jnp.dot(w[j].astype(BF16), sh[j], preferred_element_type=F32)).astype(BF16) for j in heads]
        gl = [jnp.sum(g_row[j], axis=1, keepdims=True) for j in heads]
        for j in heads:
            s_ref[j] = s[j] * jnp.exp(gl[j]) + _dot_tn((k[j] * jnp.exp(gl[j] - gc_col[j])).astype(BF16), vnh[j])
        q = [q_ref[pl.ds(r0, c), cols[j]] for j in heads]
        intra = [_dot_nt(q[j].astype(BF16), kh[j]) * decay[j] for j in heads]
        o = [jnp.dot((q[j] * egc[j]).astype(BF16), sh[j], preferred_element_type=F32)
             + jnp.dot(intra[j].astype(BF16), vnh[j], preferred_element_type=F32) for j in heads]
        for j in heads:
            z = z_ref[pl.ds(r0, c), cols[j]]
            o_ref[pl.ds(r0, c), cols[j]] = (_rms_rows(o[j]) * gn * (z * jax.nn.sigmoid(z))).astype(o_ref.dtype)
        return carry

    lax.fori_loop(0, t // c, body, 0)


def _gdn_time_block(t):
    return min(4 * CHUNK, t)


def _gdn_gate_rows(x, b, t, h_a, c):
    tb = _gdn_time_block(t)
    x = jnp.transpose(x, (0, 2, 1)).reshape(b, h_a, t // tb, tb // c, c)
    return jnp.transpose(x, (0, 2, 1, 3, 4))


def _gdn(qn, kn, vv, g, beta, p, lay, gdn_g, s0, b, t):
    m, w = qn.shape
    h_a = w // HEAD_DIM
    c = g.shape[-1]
    tb = _gdn_time_block(t)
    nt = t // tb
    zoff = lay['pos']['za']
    assert zoff % w == 0
    seq = pl.BlockSpec((tb, w), lambda bi, ti: (bi * nt + ti, 0))
    gspec = pl.BlockSpec((None, None, h_a, tb // c, c), lambda bi, ti: (bi, ti, 0, 0, 0))
    sspec = pl.BlockSpec((None, h_a, HEAD_DIM, HEAD_DIM), lambda bi, ti: (bi, 0, 0, 0))
    blocks = 4 * _nbytes((tb, w), F32) + _nbytes((tb, w), BF16) + 2 * _nbytes((h_a, HEAD_DIM, HEAD_DIM), F32)
    return pl.pallas_call(
        functools.partial(_gdn_kernel, chunk=c),
        grid=(b, nt),
        in_specs=[seq, seq, seq, gspec, gspec,
                  pl.BlockSpec((tb, w), lambda bi, ti: (bi * nt + ti, zoff // w)),
                  pl.BlockSpec((1, HEAD_DIM), lambda bi, ti: (0, 0)), sspec],
        out_specs=[seq, sspec],
        out_shape=[jax.ShapeDtypeStruct((m, w), BF16), jax.ShapeDtypeStruct(s0.shape, F32)],
        compiler_params=pltpu.CompilerParams(
            dimension_semantics=("arbitrary", "arbitrary"),
            vmem_limit_bytes=_vmem_limit(blocks, 0)),
        name="gdn",
    )(qn, kn, vv, g, beta, p, gdn_g.reshape(1, HEAD_DIM), s0)


def _retention_kernel(q_ref, k_ref, v_ref, gate_ref, cos_ref, sin_ref, lg_ref, s0_ref, o_ref, s_ref, *, chunk):
    t = q_ref.shape[0]
    hp = q_ref.shape[1] // HEAD_DIM
    c = chunk
    ii = lax.broadcasted_iota(jnp.int32, (c, c), 0)
    jj = lax.broadcasted_iota(jnp.int32, (c, c), 1)
    tril = ii >= jj
    dist = (ii - jj).astype(F32)
    step = lax.broadcasted_iota(jnp.int32, (c, 1), 0).astype(F32)
    s_ref[...] = s0_ref[...]

    def body(i, carry):
        r0 = pl.multiple_of(i * c, c)
        cos_t = cos_ref[pl.ds(r0, c), :]
        sin_t = sin_ref[pl.ds(r0, c), :]
        for j in range(hp):
            sl = slice(j * HEAD_DIM, (j + 1) * HEAD_DIM)
            lg = lg_ref[j, 0:1, 0:1]
            decay = jnp.where(tril, jnp.exp(jnp.where(tril, dist * lg, 0.0)), 0.0)
            q = _rope_head(q_ref[pl.ds(r0, c), sl], cos_t, sin_t)
            k = _rope_head(k_ref[pl.ds(r0, c), sl], cos_t, sin_t) * HEAD_DIM ** -0.5
            vh = v_ref[pl.ds(r0, c), sl].astype(BF16)
            s = s_ref[j]
            intra = _dot_nt(q.astype(BF16), k.astype(BF16)) * decay
            o = (jnp.dot((q * jnp.exp((step + 1.0) * lg)).astype(BF16), s.astype(BF16), preferred_element_type=F32)
                 + jnp.dot(intra.astype(BF16), vh, preferred_element_type=F32))
            s_ref[j] = s * jnp.exp(c * lg) + _dot_tn((k * jnp.exp((c - 1.0 - step) * lg)).astype(BF16), vh)
            gt = gate_ref[pl.ds(r0, c), sl]
            o_ref[pl.ds(r0, c), sl] = (_rms_rows(o) * (gt * jax.nn.sigmoid(gt))).astype(o_ref.dtype)
        return carry

    lax.fori_loop(0, t // c, body, 0)


def _retention(p, lay, tables, s0, b, t, hp, chunk):
    m = p.shape[0]
    h_c = s0.shape[1]
    assert h_c % hp == 0 and t % chunk == 0
    wb = hp * HEAD_DIM
    pos = lay['pos']
    log_gamma = jnp.log1p(-jnp.exp2(RET_GAMMA_BASE - jnp.arange(h_c, dtype=F32)))
    lg = jnp.broadcast_to(log_gamma[:, None, None], (h_c, 8, LANE))

    def col(nm):
        assert pos[nm] % wb == 0
        return pl.BlockSpec((t, wb), lambda bi, hi: (bi, pos[nm] // wb + hi))

    tab = pl.BlockSpec((t, LANE), lambda bi, hi: (0, 0))
    sspec = pl.BlockSpec((None, hp, HEAD_DIM, HEAD_DIM), lambda bi, hi: (bi, hi, 0, 0))
    blocks = (4 * _nbytes((t, wb), F32) + 2 * _nbytes((t, LANE), F32) + _nbytes((t, wb), BF16)
              + 2 * _nbytes((hp, HEAD_DIM, HEAD_DIM), F32))
    return pl.pallas_call(
        functools.partial(_retention_kernel, chunk=chunk),
        grid=(b, h_c // hp),
        in_specs=[col('qc'), col('kc'), col('vc'), col('gc'), tab, tab,
                  pl.BlockSpec((hp, 8, LANE), lambda bi, hi: (hi, 0, 0)), sspec],
        out_specs=[pl.BlockSpec((t, wb), lambda bi, hi: (bi, hi)), sspec],
        out_shape=[jax.ShapeDtypeStruct((m, h_c * HEAD_DIM), BF16), jax.ShapeDtypeStruct(s0.shape, F32)],
        compiler_params=pltpu.CompilerParams(
            dimension_semantics=("arbitrary", "arbitrary"),
            vmem_limit_bytes=_vmem_limit(blocks, 0)),
        name="retention",
    )(p, p, p, p, tables[0], tables[1], lg, s0)


PAGES_PER_STEP = 8
MASKED = -1e30


def _stack_idx_heads(iqlo_ref, iqhi_ref):
    rows = []
    for c in range(iqlo_ref.shape[1] // LANE):
        sl = slice(c * LANE, (c + 1) * LANE)
        rows.append(iqlo_ref[:, sl])
        rows.append(iqhi_ref[:, sl])
    return jnp.concatenate(rows, axis=0).astype(BF16)


def _idx_scores(lhs, ki2, sm_ref, rows, iw_lane, w_scale):
    d = _dot_nt(lhs, ki2.astype(BF16))
    acc = jnp.zeros((rows, ki2.shape[0]), F32)
    for h in range(IDX_HEADS):
        w = sm_ref[:, iw_lane + h:iw_lane + h + 1] * w_scale
        acc = acc + jnp.maximum(d[h * rows:(h + 1) * rows], 0.0) * w
    return acc


def _dsa_s_scores_kernel(pt_ref, iqlo_ref, iqhi_ref, sm_ref, *rest, idx_dim, iw_lane):
    page_refs, score_ref = rest[:-1], rest[-1]
    rows = iqlo_ref.shape[0]
    page = page_refs[0].shape[0]
    lhs = _stack_idx_heads(iqlo_ref, iqhi_ref)
    w_scale = IDX_HEADS ** -0.5 * idx_dim ** -0.5
    for i, pr in enumerate(page_refs):
        score_ref[:, i * page:(i + 1) * page] = _idx_scores(lhs, pr[...], sm_ref, rows, iw_lane, w_scale)


def _dsa_s_scores(page_table, iqlo, iqhi, sm, pool_ki2, layer, t, idx_dim, iw_lane, pages_per_step):
    b, n_pages = page_table.shape
    page = pool_ki2.shape[2]
    wi = iqlo.shape[1]
    assert n_pages % pages_per_step == 0
    row = lambda w: pl.BlockSpec((t, w), lambda bi, gi, pt: (bi, 0))
    pages = [pl.BlockSpec((None, None, page, LANE),
                          lambda bi, gi, pt, i=i: (layer, pt[bi, gi * pages_per_step + i], 0, 0))
             for i in range(pages_per_step)]
    return pl.pallas_call(
        functools.partial(_dsa_s_scores_kernel, idx_dim=idx_dim, iw_lane=iw_lane),
        grid_spec=pltpu.PrefetchScalarGridSpec(
            num_scalar_prefetch=1,
            grid=(b, n_pages // pages_per_step),
            in_specs=[row(wi), row(wi), row(LANE)] + pages,
            out_specs=pl.BlockSpec((None, t, pages_per_step * page), lambda bi, gi, pt: (bi, 0, gi)),
        ),
        out_shape=jax.ShapeDtypeStruct((b, t, n_pages * page), F32),
        compiler_params=pltpu.CompilerParams(dimension_semantics=("arbitrary", "arbitrary")),
        name="dsa_s_scores",
    )(page_table, iqlo, iqhi, sm, *([pool_ki2] * pages_per_step))


def _dsa_s_attn_kernel(pt_ref, sp_ref, sn_ref, q_ref, kn_ref, vn_ref, *rest, k_sel, kv_b):
    n_pg = (len(rest) - 8) // 2
    k_refs, v_refs = rest[:n_pg], rest[n_pg:2 * n_pg]
    o_ref, keysp_ref, keysn_ref, thr_ref, cut_ref, m_ref, l_ref, acc_ref = rest[2 * n_pg:]
    t = q_ref.shape[0]
    page = k_refs[0].shape[0]
    n_past = sp_ref.shape[1]
    n_heads = q_ref.shape[1] // HEAD_DIM
    group = n_heads // kv_b
    gi = pl.program_id(1)
    scale = HEAD_DIM ** -0.5
    qi = lax.broadcasted_iota(jnp.int32, (t, page), 0)
    kj = lax.broadcasted_iota(jnp.int32, (t, page), 1)
    new_visible = kj <= qi

    @pl.when(gi == 0)
    def _():
        keysp_ref[...] = _sortable_key(sp_ref[...] + 0.0)
        keysn_ref[...] = jnp.where(new_visible, _sortable_key(sn_ref[...] + 0.0), jnp.int32(INT32_MIN))
        m_ref[...] = jnp.full_like(m_ref, MASKED)
        l_ref[...] = jnp.zeros_like(l_ref)
        acc_ref[...] = jnp.zeros_like(acc_ref)

        kf = jnp.float32(k_sel)

        def count_ge(thr):
            return (jnp.sum(jnp.where(keysp_ref[...] >= thr, 1.0, 0.0), axis=1, keepdims=True)
                    + jnp.sum(jnp.where(keysn_ref[...] >= thr, 1.0, 0.0), axis=1, keepdims=True))

        zero = jnp.zeros((t, 1), jnp.int32)
        thr0 = jnp.where(count_ge(zero) >= kf, zero, jnp.int32(INT32_MIN))

        def search(i, thr):
            cand = thr | (jnp.int32(1) << (30 - i))
            return jnp.where(count_ge(cand) >= kf, cand, thr)

        thr = lax.fori_loop(0, 31, search, thr0)
        thr_ref[...] = jnp.broadcast_to(thr, thr_ref.shape)

        count = lambda mask: jnp.sum(jnp.where(mask, 1.0, 0.0), axis=1, keepdims=True)
        need = kf - count(keysp_ref[...] > thr) - count(keysn_ref[...] > thr)
        n_index = n_past + page
        pos_p = lax.broadcasted_iota(jnp.int32, keysp_ref.shape, 1)
        tied_p = lambda: jnp.where(keysp_ref[...] == thr, pos_p, jnp.int32(n_index))
        tied_n = lambda: jnp.where(keysn_ref[...] == thr, n_past + kj, jnp.int32(n_index))
        cut = _tie_cut(lambda c: count(tied_p() < c) + count(tied_n() < c), need, n_index)
        cut_ref[...] = jnp.broadcast_to(cut, cut_ref.shape)

    thr = thr_ref[:, 0:1]
    cut = cut_ref[:, 0:1]

    chosen = lambda keys, pos: _chosen(keys, pos, thr, cut)

    q = q_ref[...]
    q3 = [jnp.concatenate([q[:, (n * group + g) * HEAD_DIM:(n * group + g + 1) * HEAD_DIM] for g in range(group)],
                          axis=0).astype(BF16) for n in range(kv_b)]

    def update(k_heads, v_heads, bias):
        n_keys = bias.shape[1]
        for n in range(kv_b):
            rows = slice(n * group * t, (n + 1) * group * t)
            lg = _dot_nt(q3[n], k_heads[n].astype(BF16)) * scale
            lg = (lg.reshape(group, t, n_keys) + bias[None]).reshape(group * t, n_keys)
            m_old = m_ref[rows, 0:1]
            m_new = jnp.maximum(m_old, jnp.max(lg, axis=-1, keepdims=True))
            alpha = jnp.exp(m_old - m_new)
            pr = jnp.exp(lg - m_new)
            l_ref[rows, :] = alpha * l_ref[rows, :] + jnp.sum(pr, axis=-1, keepdims=True)
            acc_ref[rows, :] = alpha * acc_ref[rows, :] + jnp.dot(pr.astype(BF16), v_heads[n].astype(BF16),
                                                                  preferred_element_type=F32)
            m_ref[rows, :] = jnp.broadcast_to(m_new, (group * t, LANE))

    biases = []
    for i in range(n_pg):
        start = pl.multiple_of((gi * n_pg + i) * page, page)
        biases.append(jnp.where(chosen(keysp_ref[:, pl.ds(start, page)], start + kj), 0.0, MASKED))
    update([jnp.concatenate([r[:, n, :] for r in k_refs], axis=0) for n in range(kv_b)],
           [jnp.concatenate([r[:, n, :] for r in v_refs], axis=0) for n in range(kv_b)],
           jnp.concatenate(biases, axis=1))

    @pl.when(gi == pl.num_programs(1) - 1)
    def _():
        bias_new = jnp.where(new_visible, jnp.where(chosen(keysn_ref[...], n_past + kj), 0.0, MASKED), MASKED)
        heads_of = lambda ref: [ref[:, n * HEAD_DIM:(n + 1) * HEAD_DIM] for n in range(kv_b)]
        update(heads_of(kn_ref), heads_of(vn_ref), bias_new)
        for h in range(n_heads):
            rows = slice(h * t, (h + 1) * t)
            o_ref[:, h * HEAD_DIM:(h + 1) * HEAD_DIM] = acc_ref[rows, :] / l_ref[rows, 0:1]


def _dsa_s_attn(page_table, scores_past, scores_new, q, k_new, v_new, pool_k, pool_v, layer, t, kv_b, k_sel,
                pages_per_step):
    b, n_pages = page_table.shape
    page = pool_k.shape[2]
    wq = q.shape[1]
    wk = k_new.shape[2]
    n_past = n_pages * page
    pk, pv = pool_k, pool_v
    pages = [pl.BlockSpec((None, None, page, kv_b, HEAD_DIM),
                          lambda bi, gi, pt, i=i: (layer, pt[bi, gi * pages_per_step + i], 0, 0, 0))
             for i in range(pages_per_step)]
    per_seq = lambda rows, w: pl.BlockSpec((None, rows, w), lambda bi, gi, pt: (bi, 0, 0))
    n_heads = wq // HEAD_DIM
    return pl.pallas_call(
        functools.partial(_dsa_s_attn_kernel, k_sel=k_sel, kv_b=kv_b),
        grid_spec=pltpu.PrefetchScalarGridSpec(
            num_scalar_prefetch=1,
            grid=(b, n_pages // pages_per_step),
            in_specs=[per_seq(t, n_past), per_seq(t, page), pl.BlockSpec((t, wq), lambda bi, gi, pt: (bi, 0)),
                      per_seq(page, wk), per_seq(page, wk)] + pages + pages,
            out_specs=pl.BlockSpec((t, wq), lambda bi, gi, pt: (bi, 0)),
            scratch_shapes=[pltpu.VMEM((t, n_past), jnp.int32), pltpu.VMEM((t, page), jnp.int32),
                            pltpu.VMEM((t, LANE), jnp.int32), pltpu.VMEM((t, LANE), jnp.int32),
                            pltpu.VMEM((n_heads * t, LANE), F32), pltpu.VMEM((n_heads * t, LANE), F32),
                            pltpu.VMEM((n_heads * t, HEAD_DIM), F32)],
        ),
        out_shape=jax.ShapeDtypeStruct((b * t, wq), F32),
        compiler_params=pltpu.CompilerParams(dimension_semantics=("arbitrary", "arbitrary")),
        name="dsa_s_attn",
    )(page_table, scores_past, scores_new, q, k_new, v_new, *([pk] * pages_per_step), *([pv] * pages_per_step))


def _in_proj_layout(d_model, kv_b, idx_dim):
    d_mix = d_model
    h_a = (3 * d_mix) // (8 * HEAD_DIM)
    h_b = h_a
    h_c = d_mix // HEAD_DIM - h_a - h_b
    sizes = (3 * h_a * HEAD_DIM, h_a * HEAD_DIM, h_a, h_a,
             h_b * HEAD_DIM, kv_b * HEAD_DIM, kv_b * HEAD_DIM, IDX_HEADS * idx_dim, idx_dim, IDX_HEADS,
             h_c * HEAD_DIM, h_c * HEAD_DIM, h_c * HEAD_DIM, h_c * HEAD_DIM)
    names = ('qkv_a', 'za', 'aa', 'ba', 'qb', 'kb', 'vb', 'iq', 'ik', 'iw', 'qc', 'kc', 'vc', 'gc')
    src, size, off = {}, {}, 0
    for nm, s in zip(names, sizes):
        src[nm] = off
        size[nm] = s
        off += s
    wide = ('iq', 'qc', 'kc', 'vc', 'gc', 'qb', 'za', 'qkv_a', 'kb', 'vb')
    narrow = ('ik', 'iw', 'aa', 'ba')
    pos, cur = {}, 0
    for nm in wide:
        cur = -(-cur // size[nm]) * size[nm]
        pos[nm] = cur
        cur += size[nm]
    small = -(-cur // LANE) * LANE
    cur = small
    for nm in narrow:
        pos[nm] = cur
        cur += size[nm]
    assert cur <= small + LANE
    lay = dict(src=src, size=size, pos=pos, small=small, order=wide + narrow, n_src=off)
    return (h_a, h_b, h_c), lay


def _rearrange_w_in(w_in, lay, bn):
    cols, cur = [], 0
    for nm in lay['order']:
        gap = lay['pos'][nm] - cur
        if gap:
            cols.append(jnp.zeros(w_in.shape[:-1] + (gap,), w_in.dtype))
        cols.append(w_in[..., lay['src'][nm]:lay['src'][nm] + lay['size'][nm]])
        cur = lay['pos'][nm] + lay['size'][nm]
    n_pad = -(-(lay['small'] + LANE) // bn) * bn
    cols.append(jnp.zeros(w_in.shape[:-1] + (n_pad - cur,), w_in.dtype))
    return jnp.concatenate(cols, axis=-1).astype(BF16)


def _proj_part(p, lay, nm, b, t):
    return p[:, lay['pos'][nm]:lay['pos'][nm] + lay['size'][nm]].reshape(b, t, lay['size'][nm])


def _prompt_mixer(p, lay, heads, kv_b, idx_dim, b, t, conv_w, a_log, dt_bias, gdn_g):
    h_a, h_b, h_c = heads
    pos_ids = jnp.arange(t, dtype=jnp.int32)
    tab_head = _rope_tables(pos_ids, HEAD_DIM, LANE)
    tab_idx = _rope_tables(pos_ids, idx_dim, LANE)

    q, kf, kh, vh, iqlo, iqhi, kif, ki2 = _dsa_prep(p, lay, t, tab_head + tab_idx)
    ob = _dsa_prompt(q, kh, vh, iqlo, iqhi, ki2, p, lay, b, t, kv_b, idx_dim)

    n_conv = lay['size']['qkv_a']
    qn, kn, vv = _gdn_prep(p, lay, t, jnp.zeros((b, 8, n_conv), F32), conv_w)
    c = min(CHUNK, t)
    aa = _proj_part(p, lay, 'aa', b, t)
    ba = _proj_part(p, lay, 'ba', b, t)
    g = -jnp.exp(a_log) * jax.nn.softplus(aa + dt_bias)
    oa, s_gdn = _gdn(qn, kn, vv, _gdn_gate_rows(g, b, t, h_a, c), _gdn_gate_rows(jax.nn.sigmoid(ba), b, t, h_a, c),
                     p, lay, gdn_g, jnp.zeros((b, h_a, HEAD_DIM, HEAD_DIM), F32), b, t)

    oc, s_ret = _retention(p, lay, tab_head, jnp.zeros((b, h_c, HEAD_DIM, HEAD_DIM), F32), b, t,
                           hp=4, chunk=min(2 * CHUNK, t))

    o = jnp.concatenate([oa, ob, oc], axis=-1)
    vb = _proj_part(p, lay, 'vb', b, t).reshape(b, t, kv_b, HEAD_DIM)
    conv_new = _proj_part(p, lay, 'qkv_a', b, t)[:, t - (CONV_W - 1):]
    state = (kf.reshape(b, t, kv_b, HEAD_DIM), vb, kif.reshape(b, t, idx_dim), s_gdn, conv_new, s_ret)
    return o, state


def _sample_mixer(p, lay, heads, kv_b, idx_dim, b, t, conv_w, a_log, dt_bias, gdn_g, past, layer):
    h_a, h_b, h_c = heads
    cache_k, cache_v, pool_ki2, state_gdn, state_gdn_conv, state_ret, page_table = past
    page = cache_k.shape[2]
    n_past = page_table.shape[1] * page
    m = b * t
    assert t <= CHUNK and t >= CONV_W - 1
    pos_ids = n_past + jnp.arange(t, dtype=jnp.int32)
    per_row = lambda tabs: tuple(jnp.tile(x, (b, 1)) for x in tabs)
    tables = per_row(_rope_tables(pos_ids, HEAD_DIM, LANE)) + per_row(_rope_tables(pos_ids, idx_dim, LANE))
    q, kf, kh, vh, iqlo, iqhi, kif, ki2 = _dsa_prep(p, lay, m, tables)

    f32 = lambda x: x.astype(F32)
    to_page = lambda x: jnp.pad(x.reshape(b, t, -1), ((0, 0), (0, page - t), (0, 0)))
    sm = p[:, lay['small']:lay['small'] + LANE]
    iw_lane = lay['pos']['iw'] - lay['small']
    scores_past = _dsa_s_scores(page_table, f32(iqlo), f32(iqhi), sm, pool_ki2, layer, t, idx_dim, iw_lane,
                                PAGES_PER_STEP)
    own = jnp.arange(b, dtype=jnp.int32).reshape(b, 1)
    scores_new = _dsa_s_scores(own, f32(iqlo), f32(iqhi), sm, to_page(f32(ki2))[None], 0, t, idx_dim, iw_lane, 1)
    k_sel = max(1, min(TOPK_MAX, (n_past + t) // 4))
    vb = _proj_part(p, lay, 'vb', b, t)
    ob = _dsa_s_attn(page_table, scores_past, scores_new, f32(q), to_page(kf), to_page(vb), cache_k, cache_v,
                     layer, t, kv_b, k_sel, PAGES_PER_STEP)

    tp = CHUNK
    lead = tp - t
    padl = lambda x: jnp.pad(x.reshape(b, t, -1), ((0, 0), (lead, 0), (0, 0)))
    rows = lambda x: padl(x).reshape(b * tp, -1)
    p_pad = rows(p)
    buf8 = jnp.pad(state_gdn_conv[layer], ((0, 0), (8 - (CONV_W - 1), 0), (0, 0)))
    qn, kn, vv = _gdn_prep(p, lay, t, buf8, conv_w)
    aa = _proj_part(p, lay, 'aa', b, t)
    ba = _proj_part(p, lay, 'ba', b, t)
    g = -jnp.exp(a_log) * jax.nn.softplus(aa + dt_bias)
    oa, s_gdn = _gdn(rows(qn), rows(kn), rows(vv), _gdn_gate_rows(padl(g), b, tp, h_a, tp),
                     _gdn_gate_rows(padl(jax.nn.sigmoid(ba)), b, tp, h_a, tp), p_pad, lay, gdn_g,
                     state_gdn[layer], b, tp)
    log_gamma = jnp.log1p(-jnp.exp2(RET_GAMMA_BASE - jnp.arange(h_c, dtype=F32)))
    s_ret0 = state_ret[layer] * jnp.exp(-lead * log_gamma)[None, :, None, None]
    tab_pad = _rope_tables(n_past - lead + jnp.arange(tp, dtype=jnp.int32), HEAD_DIM, LANE)
    oc, s_ret = _retention(p_pad, lay, tab_pad, s_ret0, b, tp, hp=2, chunk=tp)
    tail = lambda x: x.reshape(b, tp, -1)[:, lead:].reshape(m, -1)

    o = jnp.concatenate([tail(oa), ob.astype(BF16), tail(oc)], axis=-1)
    conv_new = _proj_part(p, lay, 'qkv_a', b, t)[:, t - (CONV_W - 1):]
    state = (kf.reshape(b, t, kv_b, HEAD_DIM), vb.reshape(b, t, kv_b, HEAD_DIM), kif.reshape(b, t, idx_dim),
             s_gdn, conv_new, s_ret)
    return o, state


def _forward(xs, mods, weights, past, cfg):
    (norm_pre, norm_post, w_in_r, gdn_conv_w, gdn_a_log, gdn_dt_bias, gdn_norm,
     w_out, ffn_w_gate, ffn_w_up, ffn_w_down) = weights
    heads, lay, kv_b, idx_dim = cfg
    depth = w_out.shape[0]
    d = xs[0].shape[-1]
    bt = [x.shape[:2] for x in xs]
    m = bt[0][0] * bt[0][1]
    bm = min(1024, m)
    bm_down = min(256, m)
    hs = [x.reshape(-1, d) for x in xs]
    vec = lambda g, l, s, j: mods[g][l, :, s, j].reshape(bt[g][0], 1, d)
    gpre = lambda l, s: norm_pre[l, s].reshape(1, d)
    gpost = lambda l, s: norm_post[l, s].reshape(1, d)

    def resid(ys, l, s, nxt, res_w):
        out = [_resid(hs[g], ys[g], vec(g, l, s, 2), gpost(l, s),
                      None if nxt is None else (gpre(*nxt), vec(g, *nxt, 1), vec(g, *nxt, 0)), bt[g][1], res_w)
               for g in range(2)]
        return [o[0] for o in out], [o[1] for o in out]

    def ffn(us, l, j):
        hf = _ffn_in(us[0], us[1], ffn_w_gate, ffn_w_up, (l, j), bm=bm, bn=MXU_COLS)
        return _matmul(hf[0], hf[1], ffn_w_down, (l, j), bm=bm_down, bn=2 * MXU_COLS, out_dtype=F32,
                       name="ffn_down", single_buffer_w=True)

    states = ([], [])
    us = [_norm_mod(hs[g], gpre(0, 0), vec(g, 0, 0, 1), vec(g, 0, 0, 0), bt[g][1]) for g in range(2)]
    for l in range(depth):
        hs, us = resid(ffn(us, l, 0), l, 0, (l, 1), FFN_RES)

        ps = _matmul(us[0], us[1], w_in_r, (l,), bm=bm, bn=4 * MXU_COLS, out_dtype=F32, name="in_proj")
        mix = (gdn_conv_w[l], gdn_a_log[l], gdn_dt_bias[l], gdn_norm[l])
        o_p, st_p = _prompt_mixer(ps[0], lay, heads, kv_b, idx_dim, *bt[0], *mix)
        o_s, st_s = _sample_mixer(ps[1], lay, heads, kv_b, idx_dim, *bt[1], *mix, past, l)
        states[0].append(st_p)
        states[1].append(st_s)
        ys = _matmul(o_p, o_s, w_out, (l,), bm=bm, bn=2 * MXU_COLS, out_dtype=F32, name="out_proj")
        hs, us = resid(ys, l, 1, (l, 2), 1.0)

        hs, us = resid(ffn(us, l, 1), l, 2, (l + 1, 0) if l + 1 < depth else None, FFN_RES)
    new_state = [[jnp.stack([s[i] for s in states[g]]) for i in range(6)] for g in range(2)]
    return [hs[g].reshape(bt[g] + (d,)) for g in range(2)], new_state


def kernel(x_prompt, x_sample, cache_k, cache_v, cache_kidx, state_gdn, state_gdn_conv, state_ret, page_table,
           c_prompt, c_sample, w_ada, b_ada, norm_pre, norm_post, w_in, gdn_conv_w, gdn_a_log, gdn_dt_bias,
           gdn_norm, w_out, ffn_w_gate, ffn_w_up, ffn_w_down):
    d = x_prompt.shape[-1]
    depth = w_ada.shape[0]
    kv_b = cache_k.shape[3]
    idx_dim = cache_kidx.shape[-1]
    heads, lay = _in_proj_layout(d, kv_b, idx_dim)
    assert lay['n_src'] == w_in.shape[-1] and 2 * idx_dim == LANE
    bp, bs = c_prompt.shape[0], c_sample.shape[0]

    rows = -(-(bp + bs) // 8) * 8
    c_rows = jnp.concatenate([c_prompt, c_sample, jnp.zeros((rows - bp - bs, d), F32)], axis=0)
    mod = _adaln(c_rows, w_ada, b_ada).reshape(depth, rows, N_SUB, 3, d)

    w_in_r = _rearrange_w_in(w_in, lay, 4 * MXU_COLS)
    weights = (norm_pre, norm_post, w_in_r, gdn_conv_w, gdn_a_log, gdn_dt_bias, gdn_norm,
               w_out, ffn_w_gate, ffn_w_up, ffn_w_down)
    cfg = (heads, lay, kv_b, idx_dim)
    pool_ki2 = jnp.concatenate([cache_kidx, cache_kidx], axis=-1)
    past = (cache_k, cache_v, pool_ki2, state_gdn, state_gdn_conv, state_ret, page_table)
    (y_p, y_s), (st_p, st_s) = _forward((x_prompt, x_sample), (mod[:, :bp], mod[:, bp:bp + bs]), weights, past, cfg)
    return (y_p, y_s, *st_p, *st_s)
```
